```python
import jax, jax.numpy as jnp
from jax import lax
import numpy as np

D_MODEL = 1024
BATCH = 4
SEQ = 4096
DEPTH = 4
DEC_BATCH = 32
DEC_SEQ = 8
PAST_LEN = 8192
PAGE_SIZE = 128

N_MIXERS = 3
N_LAYERS_A = (DEPTH + 2) // 3
N_LAYERS_B = (DEPTH + 1) // 3
N_LAYERS_C = DEPTH // 3
HEAD_DIM = 64
RMS_EPS = 1e-6

A_HEADS = D_MODEL // HEAD_DIM
A_WIDTH = A_HEADS * HEAD_DIM
A_GROUPS = ((128, 1), (512, 4), (2048, 16))
A_IN = 3 * len(A_GROUPS) * A_WIDTH + A_WIDTH

B_HEADS = D_MODEL // HEAD_DIM
B_KV_HEADS = B_HEADS // 4
B_WIDTH = B_HEADS * HEAD_DIM
B_KV_WIDTH = B_KV_HEADS * HEAD_DIM
B_BLOCK = 64
B_TOPK = 16
B_WINDOW = 512
B_QBLOCK = 64
B_IN = 2 * B_WIDTH + 6 * B_KV_WIDTH + 3 * B_HEADS

C_HEADS = D_MODEL // HEAD_DIM
C_WIDTH = C_HEADS * HEAD_DIM
C_QBLOCK = 128
C_IN = 4 * C_WIDTH + C_HEADS

kernel_name = 'hybrid_dilated_nsa_fox_step'


def rms_norm(x, g):
    xf = x.astype(jnp.float32)
    y = xf * lax.rsqrt(jnp.mean(xf * xf, axis=-1, keepdims=True) + RMS_EPS)
    return (y * g.astype(jnp.float32)).astype(x.dtype)


def alibi_slopes(n):
    return jnp.asarray(2.0 ** (-8.0 * np.arange(1, n + 1) / n), dtype=jnp.float32)


def masked_softmax(s, mask, axis=-1):
    s = jnp.where(mask, s, -jnp.inf)
    m = jnp.max(s, axis=axis, keepdims=True)
    m = jnp.where(jnp.isfinite(m), m, 0.0)
    e = jnp.exp(s - m)
    den = jnp.sum(e, axis=axis, keepdims=True)
    p = e / jnp.where(den > 0, den, 1.0)
    lse = jnp.squeeze(m + jnp.log(den), axis=axis)
    return p, lse


def gather_pages(pool, page_table):
    g = pool[page_table]
    return g.reshape((g.shape[0], g.shape[1] * g.shape[2]) + g.shape[3:])


def dilated_attn_prompt(q, k, v, window, dilation, slopes):
    b, t, h, hd = q.shape
    nk = window // dilation
    ln = t // dilation
    nb = -(-ln // nk)
    lp = nb * nk

    def streams(z):
        z = z.reshape(b, ln, dilation, h, hd).transpose(0, 2, 1, 3, 4)
        z = jnp.pad(z, ((0, 0), (0, 0), (0, lp - ln), (0, 0), (0, 0)))
        return z.reshape(b, dilation, nb, nk, h, hd)

    def with_prev(z):
        prev = jnp.pad(z, ((0, 0), (0, 0), (1, 0), (0, 0), (0, 0), (0, 0)))[:, :, :nb]
        return jnp.concatenate([prev, z], axis=3)

    qs = streams(q)
    kb = with_prev(streams(k))
    vb = with_prev(streams(v))
    s = jnp.einsum('brnqhd,brnkhd->brnhqk', qs, kb).astype(jnp.float32) * (hd ** -0.5)
    step = (nk + jnp.arange(nk))[:, None] - jnp.arange(2 * nk)[None, :]
    first = ((jnp.arange(nb) - 1)[:, None, None] * nk + jnp.arange(2 * nk)[None, None, :]) >= 0
    mask = (step >= 0) & (step <= nk) & first
    s = s - slopes[:, None, None] * (step * dilation).astype(jnp.float32)
    p, lse = masked_softmax(s, mask[None, None, :, None])
    o = jnp.einsum('brnhqk,brnkhd->brnqhd', p.astype(v.dtype), vb)
    o = o.reshape(b, dilation, lp, h, hd)[:, :, :ln].transpose(0, 2, 1, 3, 4).reshape(b, t, h, hd)
    lse = lse.transpose(0, 1, 2, 4, 3).reshape(b, dilation, lp, h)[:, :, :ln]
    lse = lse.transpose(0, 2, 1, 3).reshape(b, t, h)
    return o, lse


def dilated_attn_step(q, k_all, v_all, window, dilation, slopes):
    b, tq, h, hd = q.shape
    lk = k_all.shape[1]
    nk = window // dilation
    steps = jnp.arange(nk + 1)
    idx = (lk - tq + jnp.arange(tq))[:, None] - dilation * steps[None, :]
    valid = idx >= 0
    idx = jnp.maximum(idx, 0)
    kg = k_all[:, idx]
    vg = v_all[:, idx]
    s = jnp.einsum('bqhd,bqjhd->bhqj', q, kg).astype(jnp.float32) * (hd ** -0.5)
    s = s - slopes[:, None, None] * (dilation * steps).astype(jnp.float32)
    p, lse = masked_softmax(s, valid)
    o = jnp.einsum('bhqj,bqjhd->bqhd', p.astype(v_all.dtype), vg)
    return o, lse.transpose(0, 2, 1)


def mixer_a(h, w_in, w_out, cache):
    b, t, _ = h.shape
    n_g = len(A_GROUPS)
    proj = jnp.einsum('btd,de->bte', h, w_in)
    qkv = proj[..., :3 * n_g * A_WIDTH].reshape(b, t, n_g, 3, A_HEADS, HEAD_DIM)
    gate = proj[..., 3 * n_g * A_WIDTH:]
    slopes = alibi_slopes(A_HEADS)
    outs, lses, new_state = [], [], []
    for g, (window, dilation) in enumerate(A_GROUPS):
        q = qkv[:, :, g, 0]
        kv_new = qkv[:, :, g, 1:]
        if cache is None:
            o, lse = dilated_attn_prompt(q, kv_new[:, :, 0], kv_new[:, :, 1], window, dilation, slopes)
            new_state.append(kv_new[:, t - min(window, t):])
        else:
            kv_all = jnp.concatenate([cache[g], kv_new], axis=1)
            o, lse = dilated_attn_step(q, kv_all[:, :, 0], kv_all[:, :, 1], window, dilation, slopes)
            new_state.append(kv_all[:, t:])
        outs.append(o)
        lses.append(lse)
    wts = jax.nn.softmax(jnp.stack(lses), axis=0)
    o = jnp.einsum('gbth,gbthd->bthd', wts, jnp.stack(outs).astype(jnp.float32))
    o = o.reshape(b, t, A_WIDTH).astype(h.dtype) * jax.nn.silu(gate)
    return jnp.einsum('bte,ed->btd', o, w_out), new_state


def nsa_attention(q, kv_full, kv_win, gates, pos_emb, w_cmp):
    b, tq, hq, hd = q.shape
    l = kv_full.shape[1]
    lw = kv_win.shape[1]
    g = B_KV_HEADS
    r = hq // g
    scale = hd ** -0.5
    dt = q.dtype
    slopes = alibi_slopes(hq).reshape(g, r)
    nblk = -(-l // B_BLOCK)
    lp = nblk * B_BLOCK
    blocks = jnp.pad(kv_full, ((0, 0), (0, lp - l), (0, 0), (0, 0), (0, 0))).reshape(b, nblk, B_BLOCK, 4, g, hd)
    pooled = jnp.mean(blocks[:, :, :, :2] + pos_emb.transpose(1, 0, 2, 3), axis=2)
    cmp = jnp.einsum('bncgd,cgde->bncge', pooled, w_cmp)
    k_cmp, v_cmp = cmp[:, :, 0], cmp[:, :, 1]
    sel_blocks = blocks[:, :, :, 2:].transpose(0, 4, 1, 2, 3, 5)
    kw_p = jnp.pad(kv_win, ((0, 0), (B_WINDOW, 0), (0, 0), (0, 0), (0, 0)))
    blk_end = (jnp.arange(nblk) + 1) * B_BLOCK - 1
    bidx = jnp.arange(nblk)
    n_sel = min(B_TOPK, nblk)
    qb_size = B_QBLOCK if tq % B_QBLOCK == 0 else tq
    nq = tq // qb_size

    def one_block(args):
        qb, gb, j = args
        pos = l - tq + j * qb_size + jnp.arange(qb_size)
        qg = qb.reshape(b, qb_size, g, r, hd)
        s_c = jnp.einsum('bqgrd,bngd->bgrqn', qg, k_cmp).astype(jnp.float32) * scale
        s_c = s_c - slopes[:, :, None, None] * (pos[:, None] - blk_end[None, :]).astype(jnp.float32)
        p_c, _ = masked_softmax(s_c, blk_end[None, :] <= pos[:, None])
        o_c = jnp.einsum('bgrqn,bngd->bqgrd', p_c.astype(dt), v_cmp)
        cur = pos // B_BLOCK
        forced = (bidx[None, :] == 0) | (bidx[None, :] >= cur[:, None] - 1)
        score = jnp.where(bidx[None, :] > cur[:, None], -jnp.inf,
                          jnp.where(forced, jnp.inf, jnp.sum(p_c, axis=2)))
        _, sel = lax.top_k(score, n_sel)
        kv_sel = jax.vmap(jax.vmap(lambda blk, ix: blk[ix]))(sel_blocks, sel)
        kpos = sel[..., None] * B_BLOCK + jnp.arange(B_BLOCK)
        dist_s = pos[None, None, :, None, None] - kpos
        s_s = jnp.einsum('bqgrd,bgqnkd->bgrqnk', qg, kv_sel[..., 0, :]).astype(jnp.float32) * scale
        s_s = s_s - slopes[None, :, :, None, None, None] * dist_s[:, :, None].astype(jnp.float32)
        p_s, _ = masked_softmax(s_s, (dist_s >= 0)[:, :, None], axis=(-2, -1))
        o_s = jnp.einsum('bgrqnk,bgqnkd->bqgrd', p_s.astype(dt), kv_sel[..., 1, :])
        start = lw - tq + j * qb_size
        kvw = lax.dynamic_slice_in_dim(kw_p, start, B_WINDOW + qb_size, axis=1)
        kpos_w = l - tq + j * qb_size - B_WINDOW + jnp.arange(B_WINDOW + qb_size)
        dist_w = pos[:, None] - kpos_w[None, :]
        mask_w = (dist_w >= 0) & (dist_w <= B_WINDOW) & (kpos_w[None, :] >= l - lw)
        s_w = jnp.einsum('bqgrd,bkgd->bgrqk', qg, kvw[:, :, 0]).astype(jnp.float32) * scale
        s_w = s_w - slopes[:, :, None, None] * dist_w.astype(jnp.float32)
        p_w, _ = masked_softmax(s_w, mask_w)
        o_w = jnp.einsum('bgrqk,bkgd->bqgrd', p_w.astype(dt), kvw[:, :, 1])
        gb = gb.reshape(b, qb_size, g, r, 3)
        o = gb[..., 0:1] * o_c + gb[..., 1:2] * o_s + gb[..., 2:3] * o_w
        return o.reshape(b, qb_size, hq, hd)

    qs = q.reshape(b, nq, qb_size, hq, hd).transpose(1, 0, 2, 3, 4)
    gs = gates.reshape(b, nq, qb_size, hq, 3).transpose(1, 0, 2, 3, 4)
    out = lax.map(one_block, (qs, gs, jnp.arange(nq)))
    return out.transpose(1, 0, 2, 3, 4).reshape(b, tq, hq, hd)


def mixer_b(h, w_in, pos_emb, w_cmp, w_out, cache):
    b, t, _ = h.shape
    proj = jnp.einsum('btd,de->bte', h, w_in)
    q = proj[..., :B_WIDTH].reshape(b, t, B_HEADS, HEAD_DIM)
    o1 = B_WIDTH + 6 * B_KV_WIDTH
    kv = proj[..., B_WIDTH:o1].reshape(b, t, 6, B_KV_HEADS, HEAD_DIM)
    branch_gates = jax.nn.sigmoid(proj[..., o1:o1 + 3 * B_HEADS].reshape(b, t, B_HEADS, 3))
    gate = proj[..., o1 + 3 * B_HEADS:]
    kv_full_new, kv_win_new = kv[:, :, :4], kv[:, :, 4:]
    if cache is None:
        kv_full, kv_win = kv_full_new, kv_win_new
        new_win = kv_win_new[:, t - min(B_WINDOW, t):]
    else:
        kv_full = jnp.concatenate([cache[0], kv_full_new], axis=1)
        kv_win = jnp.concatenate([cache[1], kv_win_new], axis=1)
        new_win = kv_win[:, t:]
    o = nsa_attention(q, kv_full, kv_win, branch_gates, pos_emb, w_cmp)
    o = o.reshape(b, t, B_WIDTH) * jax.nn.silu(gate)
    return jnp.einsum('bte,ed->btd', o, w_out), [kv_full_new, new_win]


def fox_attention(q, k, v, cum):
    b, tq, h, hd = q.shape
    l = k.shape[1]
    scale = hd ** -0.5
    qb_size = C_QBLOCK if tq % C_QBLOCK == 0 else tq
    nq = tq // qb_size
    cum_k = cum.transpose(0, 2, 1)
    cum_q = cum[:, l - tq:].reshape(b, nq, qb_size, h).transpose(1, 0, 3, 2)
    kpos = jnp.arange(l)

    def one_block(args):
        qb, cq, j = args
        pos = l - tq + j * qb_size + jnp.arange(qb_size)
        s = jnp.einsum('bqhd,bkhd->bhqk', qb, k).astype(jnp.float32) * scale
        s = s + cq[..., None] - cum_k[:, :, None, :]
        p, _ = masked_softmax(s, kpos[None, :] <= pos[:, None])
        return jnp.einsum('bhqk,bkhd->bqhd', p.astype(v.dtype), v)

    qs = q.reshape(b, nq, qb_size, h, hd).transpose(1, 0, 2, 3, 4)
    out = lax.map(one_block, (qs, cum_q, jnp.arange(nq)))
    return out.transpose(1, 0, 2, 3, 4).reshape(b, tq, h, hd)


def mixer_c(h, w_in, b_forget, w_out, cache):
    b, t, _ = h.shape
    proj = jnp.einsum('btd,de->bte', h, w_in)
    qkv = proj[..., :3 * C_WIDTH].reshape(b, t, 3, C_HEADS, HEAD_DIM)
    logf = jax.nn.log_sigmoid(proj[..., 3 * C_WIDTH:3 * C_WIDTH + C_HEADS].astype(jnp.float32)
                              + b_forget.astype(jnp.float32))
    gate = proj[..., 3 * C_WIDTH + C_HEADS:]
    kv_new = qkv[:, :, 1:]
    if cache is None:
        kv_all, logf_all = kv_new, logf
    else:
        kv_all = jnp.concatenate([cache[0], kv_new], axis=1)
        logf_all = jnp.concatenate([cache[1].astype(jnp.float32), logf], axis=1)
    cum = jnp.cumsum(logf_all, axis=1)
    o = fox_attention(qkv[:, :, 0], kv_all[:, :, 0], kv_all[:, :, 1], cum)
    o = o.reshape(b, t, C_WIDTH) * jax.nn.silu(gate)
    return jnp.einsum('bte,ed->btd', o, w_out), [kv_new, logf.astype(h.dtype)]


def run_trunk(x, c, layer_cache, weights):
    (ada_w, ada_b, norm_pre, norm_post, a_w_in, a_w_out, b_w_in, b_pos_emb, b_w_cmp, b_w_out,
     c_w_in, c_b_forget, c_w_out) = weights
    states = []
    for i in range(DEPTH):
        kind, j = i % N_MIXERS, i // N_MIXERS
        mod = jnp.einsum('bd,de->be', jax.nn.silu(c), ada_w[i]) + ada_b[i]
        shift, scale, gate = jnp.split(mod[:, None, :], 3, axis=-1)
        hmod = rms_norm(x, norm_pre[i]) * (1 + scale) + shift
        cache = layer_cache(i)
        if kind == 0:
            y, st = mixer_a(hmod, a_w_in[j], a_w_out[j], cache)
        elif kind == 1:
            y, st = mixer_b(hmod, b_w_in[j], b_pos_emb[j], b_w_cmp[j], b_w_out[j], cache)
        else:
            y, st = mixer_c(hmod, c_w_in[j], c_b_forget[j], c_w_out[j], cache)
        x = x + gate * rms_norm(y, norm_post[i])
        states.append(st)
    return x, states


def stack_state(states, kind, k):
    return jnp.stack([states[i][k] for i in range(DEPTH) if i % N_MIXERS == kind])


def setup_inputs(seed: int = 0) -> dict:
    key = jax.random.key(seed)
    ks = jax.random.split(key, 32)
    f32 = jnp.float32

    def nrm(k, shape, s=1.0):
        return s * jax.random.normal(k, shape, f32)

    n_pages = PAST_LEN // PAGE_SIZE
    n_used = DEC_BATCH * n_pages
    n_phys = n_used + n_used // 4
    page_table = jax.random.permutation(ks[11], n_phys)[:n_used].reshape(DEC_BATCH, n_pages).astype(jnp.int32)
    wa = [min(w, PAST_LEN) for w, _ in A_GROUPS]
    return {
        'x_prompt': nrm(ks[0], (BATCH, SEQ, D_MODEL)),
        'x_sample': nrm(ks[1], (DEC_BATCH, DEC_SEQ, D_MODEL)),
        'c_prompt': nrm(ks[2], (BATCH, D_MODEL)),
        'c_sample': nrm(ks[3], (DEC_BATCH, D_MODEL)),
        'cache_a_w128': nrm(ks[4], (N_LAYERS_A, DEC_BATCH, wa[0], 2, A_HEADS, HEAD_DIM)),
        'cache_a_w512': nrm(ks[5], (N_LAYERS_A, DEC_BATCH, wa[1], 2, A_HEADS, HEAD_DIM)),
        'cache_a_w2048': nrm(ks[6], (N_LAYERS_A, DEC_BATCH, wa[2], 2, A_HEADS, HEAD_DIM)),
        'cache_b_kv': nrm(ks[7], (N_LAYERS_B, n_phys, PAGE_SIZE, 4, B_KV_HEADS, HEAD_DIM)),
        'cache_b_win': nrm(ks[8], (N_LAYERS_B, DEC_BATCH, min(B_WINDOW, PAST_LEN), 2, B_KV_HEADS, HEAD_DIM)),
        'cache_c_kv': nrm(ks[9], (N_LAYERS_C, n_phys, PAGE_SIZE, 2, C_HEADS, HEAD_DIM)),
        'cache_c_logf': jax.nn.log_sigmoid(nrm(ks[10], (N_LAYERS_C, n_phys, PAGE_SIZE, C_HEADS)) + 2.0),
        'page_table': page_table,
        'ada_w': nrm(ks[12], (DEPTH, D_MODEL, 3 * D_MODEL), 0.5 * D_MODEL ** -0.5),
        'ada_b': nrm(ks[13], (DEPTH, 3 * D_MODEL), 0.02),
        'norm_pre': 1.0 + nrm(ks[14], (DEPTH, D_MODEL), 0.05),
        'norm_post': 1.0 + nrm(ks[15], (DEPTH, D_MODEL), 0.05),
        'a_w_in': nrm(ks[16], (N_LAYERS_A, D_MODEL, A_IN), D_MODEL ** -0.5),
        'a_w_out': nrm(ks[17], (N_LAYERS_A, A_WIDTH, D_MODEL), A_WIDTH ** -0.5),
        'b_w_in': nrm(ks[18], (N_LAYERS_B, D_MODEL, B_IN), D_MODEL ** -0.5),
        'b_pos_emb': nrm(ks[19], (N_LAYERS_B, 2, B_BLOCK, B_KV_HEADS, HEAD_DIM), 0.5),
        'b_w_cmp': nrm(ks[20], (N_LAYERS_B, 2, B_KV_HEADS, HEAD_DIM, HEAD_DIM), HEAD_DIM ** -0.5),
        'b_w_out': nrm(ks[21], (N_LAYERS_B, B_WIDTH, D_MODEL), B_WIDTH ** -0.5),
        'c_w_in': nrm(ks[22], (N_LAYERS_C, D_MODEL, C_IN), D_MODEL ** -0.5),
        'c_b_forget': 1.0 + nrm(ks[23], (N_LAYERS_C, C_HEADS), 0.5),
        'c_w_out': nrm(ks[24], (N_LAYERS_C, C_WIDTH, D_MODEL), C_WIDTH ** -0.5),
    }


def reference(x_prompt, x_sample, c_prompt, c_sample, cache_a_w128, cache_a_w512, cache_a_w2048,
              cache_b_kv, cache_b_win, cache_c_kv, cache_c_logf, page_table,
              ada_w, ada_b, norm_pre, norm_post, a_w_in, a_w_out, b_w_in, b_pos_emb, b_w_cmp, b_w_out,
              c_w_in, c_b_forget, c_w_out):
    weights = (ada_w, ada_b, norm_pre, norm_post, a_w_in, a_w_out, b_w_in, b_pos_emb, b_w_cmp, b_w_out,
               c_w_in, c_b_forget, c_w_out)

    def prompt_cache(i):
        return None

    def sample_cache(i):
        kind, j = i % N_MIXERS, i // N_MIXERS
        if kind == 0:
            return [cache_a_w128[j], cache_a_w512[j], cache_a_w2048[j]]
        if kind == 1:
            return [gather_pages(cache_b_kv[j], page_table), cache_b_win[j]]
        return [gather_pages(cache_c_kv[j], page_table), gather_pages(cache_c_logf[j], page_table)]

    y_prompt, st_p = run_trunk(x_prompt, c_prompt, prompt_cache, weights)
    y_sample, st_s = run_trunk(x_sample, c_sample, sample_cache, weights)

    a128_p = stack_state(st_p, 0, 0)
    a512_p = stack_state(st_p, 0, 1)
    a2048_p = stack_state(st_p, 0, 2)
    bkv_p = stack_state(st_p, 1, 0)
    bwin_p = stack_state(st_p, 1, 1)
    ckv_p = stack_state(st_p, 2, 0)
    clogf_p = stack_state(st_p, 2, 1)
    a128_s = stack_state(st_s, 0, 0)
    a512_s = stack_state(st_s, 0, 1)
    a2048_s = stack_state(st_s, 0, 2)
    bkv_s = stack_state(st_s, 1, 0)
    bwin_s = stack_state(st_s, 1, 1)
    ckv_s = stack_state(st_s, 2, 0)
    clogf_s = stack_state(st_s, 2, 1)
    return (y_prompt, y_sample, a128_p, a512_p, a2048_p, bkv_p, bwin_p, ckv_p, clogf_p,
            a128_s, a512_s, a2048_s, bkv_s, bwin_s, ckv_s, clogf_s)
```

```python
import functools

import jax
import jax.numpy as jnp
import numpy as np
from jax import lax
from jax.experimental import pallas as pl
from jax.experimental.pallas import tpu as pltpu

D_MODEL = 1024
DEPTH = 4
N_MIXERS = 3
HEAD_DIM = 64
RMS_EPS = 1e-6
N_HEADS = D_MODEL // HEAD_DIM
A_GROUPS = ((128, 1), (512, 4), (2048, 16))
A_WIDTH = D_MODEL
B_KV_HEADS = N_HEADS // 4
B_KV_WIDTH = B_KV_HEADS * HEAD_DIM
B_BLOCK = 64
B_TOPK = 16
B_WINDOW = 512
B_QBLOCK = 64
C_QBLOCK = 128
PAGE_SIZE = 128

VMEM_LIMIT = 56 * 1024 * 1024

F32 = jnp.float32
BF16 = jnp.bfloat16


def _sigmoid(x):
    return 1.0 / (1.0 + jnp.exp(-x))


def _ada_kernel(c_ref, w_ref, b_ref, o_ref):
    c = c_ref[...]
    s = c * _sigmoid(c)
    o_ref[...] = jnp.dot(s, w_ref[...], preferred_element_type=F32,
                         precision=lax.Precision.HIGHEST) + b_ref[...]


def ada_modulation(c_all, ada_w, ada_b):
    nb = c_all.shape[0]
    tn = 1024
    return pl.pallas_call(
        _ada_kernel,
        out_shape=jax.ShapeDtypeStruct((DEPTH, nb, 3 * D_MODEL), F32),
        grid=(DEPTH, 3 * D_MODEL // tn),
        in_specs=[
            pl.BlockSpec((nb, D_MODEL), lambda i, n: (0, 0)),
            pl.BlockSpec((None, D_MODEL, tn), lambda i, n: (i, 0, n)),
            pl.BlockSpec((None, 1, tn), lambda i, n: (i, 0, n)),
        ],
        out_specs=pl.BlockSpec((None, nb, tn), lambda i, n: (i, 0, n)),
        compiler_params=pltpu.CompilerParams(vmem_limit_bytes=VMEM_LIMIT),
        name="ada_modulation",
    )(c_all, ada_w, ada_b.reshape(DEPTH, 1, 3 * D_MODEL))


def _inproj_kernel(x_ref, shift_ref, scale_ref, g_ref, w_ref, o_ref, h_ref):
    bt, tt, d = x_ref.shape

    @pl.when(pl.program_id(1) == 0)
    def _():
        x = x_ref[...]
        ms = jnp.mean(x * x, axis=-1, keepdims=True)
        y = x * lax.rsqrt(ms + RMS_EPS) * g_ref[...]
        h = y * (1.0 + scale_ref[...]) + shift_ref[...]
        h_ref[...] = h.reshape(bt * tt, d).astype(BF16)

    o_ref[...] = jnp.dot(h_ref[...], w_ref[...],
                         preferred_element_type=F32).reshape(o_ref.shape)


def in_projection(x, mod, g, w_bf16, *, bt, tt, tn):
    b, t, d = x.shape
    e = w_bf16.shape[1]
    assert b % bt == 0 and t % tt == 0 and e % tn == 0
    nt = t // tt
    return pl.pallas_call(
        _inproj_kernel,
        out_shape=jax.ShapeDtypeStruct((b, t, e), F32),
        grid=(b // bt * nt, e // tn),
        in_specs=[
            pl.BlockSpec((bt, tt, d), lambda m, n: (m // nt, m % nt, 0)),
            pl.BlockSpec((bt, 1, d), lambda m, n: (m // nt, 0, 0)),
            pl.BlockSpec((bt, 1, d), lambda m, n: (m // nt, 0, 1)),
            pl.BlockSpec((1, d), lambda m, n: (0, 0)),
            pl.BlockSpec((d, tn), lambda m, n: (0, n)),
        ],
        out_specs=pl.BlockSpec((bt, tt, tn), lambda m, n: (m // nt, m % nt, n)),
        scratch_shapes=[pltpu.VMEM((bt * tt, d), BF16)],
        compiler_params=pltpu.CompilerParams(
            dimension_semantics=("arbitrary", "arbitrary"),
            vmem_limit_bytes=VMEM_LIMIT),
        name="in_projection",
    )(x, mod, mod, g.reshape(1, d), w_bf16)


def _outproj_kernel(o_ref, gate_ref, x_ref, mg_ref, g_ref, w_ref, out_ref):
    bt, tt, d = x_ref.shape
    gt = gate_ref[...]
    og = (o_ref[...] * (gt * _sigmoid(gt))).reshape(bt * tt, d).astype(BF16)
    y = jnp.dot(og, w_ref[...], preferred_element_type=F32)
    ms = jnp.mean(y * y, axis=-1, keepdims=True)
    yn = (y * lax.rsqrt(ms + RMS_EPS) * g_ref[...]).reshape(bt, tt, d)
    out_ref[...] = x_ref[...] + mg_ref[...] * yn


def out_projection(o, proj, gate_col, x, mod, g, w_bf16, *, bt, tt):
    b, t, d = x.shape
    nt = t // tt
    return pl.pallas_call(
        _outproj_kernel,
        out_shape=jax.ShapeDtypeStruct((b, t, d), F32),
        grid=(b // bt * nt,),
        in_specs=[
            pl.BlockSpec((bt, tt, d), lambda m: (m // nt, m % nt, 0)),
            pl.BlockSpec((bt, tt, d), lambda m: (m // nt, m % nt, gate_col)),
            pl.BlockSpec((bt, tt, d), lambda m: (m // nt, m % nt, 0)),
            pl.BlockSpec((bt, 1, d), lambda m: (m // nt, 0, 2)),
            pl.BlockSpec((1, d), lambda m: (0, 0)),
            pl.BlockSpec((d, d), lambda m: (0, 0)),
        ],
        out_specs=pl.BlockSpec((bt, tt, d), lambda m: (m // nt, m % nt, 0)),
        compiler_params=pltpu.CompilerParams(
            dimension_semantics=("arbitrary",),
            vmem_limit_bytes=VMEM_LIMIT),
        name="out_projection",
    )(o, proj, x, mod, g.reshape(1, d), w_bf16)


def alibi_slopes(n):
    return jnp.asarray(2.0 ** (-8.0 * np.arange(1, n + 1) / n), dtype=jnp.float32)


def masked_softmax(s, mask, axis=-1):
    s = jnp.where(mask, s, -jnp.inf)
    m = jnp.max(s, axis=axis, keepdims=True)
    m = jnp.where(jnp.isfinite(m), m, 0.0)
    e = jnp.exp(s - m)
    den = jnp.sum(e, axis=axis, keepdims=True)
    p = e / jnp.where(den > 0, den, 1.0)
    lse = jnp.squeeze(m + jnp.log(den), axis=axis)
    return p, lse


def gather_pages(pool, page_table):
    g = pool[page_table]
    return g.reshape((g.shape[0], g.shape[1] * g.shape[2]) + g.shape[3:])


def dilated_attn_prompt(q, k, v, window, dilation, slopes):
    b, t, h, hd = q.shape
    nk = window // dilation
    ln = t // dilation
    nb = -(-ln // nk)
    lp = nb * nk

    def streams(z):
        z = z.reshape(b, ln, dilation, h, hd).transpose(0, 2, 1, 3, 4)
        z = jnp.pad(z, ((0, 0), (0, 0), (0, lp - ln), (0, 0), (0, 0)))
        return z.reshape(b, dilation, nb, nk, h, hd)

    def with_prev(z):
        prev = jnp.pad(z, ((0, 0), (0, 0), (1, 0), (0, 0), (0, 0), (0, 0)))[:, :, :nb]
        return jnp.concatenate([prev, z], axis=3)

    qs = streams(q)
    kb = with_prev(streams(k))
    vb = with_prev(streams(v))
    s = jnp.einsum('brnqhd,brnkhd->brnhqk', qs, kb).astype(jnp.float32) * (hd ** -0.5)
    step = (nk + jnp.arange(nk))[:, None] - jnp.arange(2 * nk)[None, :]
    first = ((jnp.arange(nb) - 1)[:, None, None] * nk + jnp.arange(2 * nk)[None, None, :]) >= 0
    mask = (step >= 0) & (step <= nk) & first
    s = s - slopes[:, None, None] * (step * dilation).astype(jnp.float32)
    p, lse = masked_softmax(s, mask[None, None, :, None])
    o = jnp.einsum('brnhqk,brnkhd->brnqhd', p.astype(v.dtype), vb)
    o = o.reshape(b, dilation, lp, h, hd)[:, :, :ln].transpose(0, 2, 1, 3, 4).reshape(b, t, h, hd)
    lse = lse.transpose(0, 1, 2, 4, 3).reshape(b, dilation, lp, h)[:, :, :ln]
    lse = lse.transpose(0, 2, 1, 3).reshape(b, t, h)
    return o, lse


def dilated_attn_step(q, k_all, v_all, window, dilation, slopes):
    b, tq, h, hd = q.shape
    lk = k_all.shape[1]
    nk = window // dilation
    steps = jnp.arange(nk + 1)
    idx = (lk - tq + jnp.arange(tq))[:, None] - dilation * steps[None, :]
    valid = idx >= 0
    idx = jnp.maximum(idx, 0)
    kg = k_all[:, idx]
    vg = v_all[:, idx]
    s = jnp.einsum('bqhd,bqjhd->bhqj', q, kg).astype(jnp.float32) * (hd ** -0.5)
    s = s - slopes[:, None, None] * (dilation * steps).astype(jnp.float32)
    p, lse = masked_softmax(s, valid)
    o = jnp.einsum('bhqj,bqjhd->bqhd', p.astype(v_all.dtype), vg)
    return o, lse.transpose(0, 2, 1)


def mixer_a_jax(proj, cache):
    b, t, _ = proj.shape
    n_g = len(A_GROUPS)
    qkv = proj[..., :3 * n_g * A_WIDTH].reshape(b, t, n_g, 3, N_HEADS, HEAD_DIM)
    slopes = alibi_slopes(N_HEADS)
    outs, lses, new_state = [], [], []
    for g, (window, dilation) in enumerate(A_GROUPS):
        q = qkv[:, :, g, 0]
        kv_new = qkv[:, :, g, 1:]
        if cache is None:
            o, lse = dilated_attn_prompt(q, kv_new[:, :, 0], kv_new[:, :, 1], window, dilation, slopes)
            new_state.append(kv_new[:, t - min(window, t):])
        else:
            kv_all = jnp.concatenate([cache[g], kv_new], axis=1)
            o, lse = dilated_attn_step(q, kv_all[:, :, 0], kv_all[:, :, 1], window, dilation, slopes)
            new_state.append(kv_all[:, t:])
        outs.append(o)
        lses.append(lse)
    wts = jax.nn.softmax(jnp.stack(lses), axis=0)
    o = jnp.einsum('gbth,gbthd->bthd', wts, jnp.stack(outs).astype(jnp.float32))
    return o.reshape(b, t, A_WIDTH), new_state


def nsa_attention(q, kv_full, kv_win, gates, pos_emb, w_cmp):
    b, tq, hq, hd = q.shape
    l = kv_full.shape[1]
    lw = kv_win.shape[1]
    g = B_KV_HEADS
    r = hq // g
    scale = hd ** -0.5
    dt = q.dtype
    slopes = alibi_slopes(hq).reshape(g, r)
    nblk = -(-l // B_BLOCK)
    lp = nblk * B_BLOCK
    blocks = jnp.pad(kv_full, ((0, 0), (0, lp - l), (0, 0), (0, 0), (0, 0))).reshape(b, nblk, B_BLOCK, 4, g, hd)
    pooled = jnp.mean(blocks[:, :, :, :2] + pos_emb.transpose(1, 0, 2, 3), axis=2)
    cmp = jnp.einsum('bncgd,cgde->bncge', pooled, w_cmp)
    k_cmp, v_cmp = cmp[:, :, 0], cmp[:, :, 1]
    sel_blocks = blocks[:, :, :, 2:].transpose(0, 4, 1, 2, 3, 5)
    kw_p = jnp.pad(kv_win, ((0, 0), (B_WINDOW, 0), (0, 0), (0, 0), (0, 0)))
    blk_end = (jnp.arange(nblk) + 1) * B_BLOCK - 1
    bidx = jnp.arange(nblk)
    n_sel = min(B_TOPK, nblk)
    qb_size = B_QBLOCK if tq % B_QBLOCK == 0 else tq
    nq = tq // qb_size

    def one_block(args):
        qb, gb, j = args
        pos = l - tq + j * qb_size + jnp.arange(qb_size)
        qg = qb.reshape(b, qb_size, g, r, hd)
        s_c = jnp.einsum('bqgrd,bngd->bgrqn', qg, k_cmp).astype(jnp.float32) * scale
        s_c = s_c - slopes[:, :, None, None] * (pos[:, None] - blk_end[None, :]).astype(jnp.float32)
        p_c, _ = masked_softmax(s_c, blk_end[None, :] <= pos[:, None])
        o_c = jnp.einsum('bgrqn,bngd->bqgrd', p_c.astype(dt), v_cmp)
        cur = pos // B_BLOCK
        forced = (bidx[None, :] == 0) | (bidx[None, :] >= cur[:, None] - 1)
        score = jnp.where(bidx[None, :] > cur[:, None], -jnp.inf,
                          jnp.where(forced, jnp.inf, jnp.sum(p_c, axis=2)))
        _, sel = lax.top_k(score, n_sel)
        kv_sel = jax.vmap(jax.vmap(lambda blk, ix: blk[ix]))(sel_blocks, sel)
        kpos = sel[..., None] * B_BLOCK + jnp.arange(B_BLOCK)
        dist_s = pos[None, None, :, None, None] - kpos
        s_s = jnp.einsum('bqgrd,bgqnkd->bgrqnk', qg, kv_sel[..., 0, :]).astype(jnp.float32) * scale
        s_s = s_s - slopes[None, :, :, None, None, None] * dist_s[:, :, None].astype(jnp.float32)
        p_s, _ = masked_softmax(s_s, (dist_s >= 0)[:, :, None], axis=(-2, -1))
        o_s = jnp.einsum('bgrqnk,bgqnkd->bqgrd', p_s.astype(dt), kv_sel[..., 1, :])
        start = lw - tq + j * qb_size
        kvw = lax.dynamic_slice_in_dim(kw_p, start, B_WINDOW + qb_size, axis=1)
        kpos_w = l - tq + j * qb_size - B_WINDOW + jnp.arange(B_WINDOW + qb_size)
        dist_w = pos[:, None] - kpos_w[None, :]
        mask_w = (dist_w >= 0) & (dist_w <= B_WINDOW) & (kpos_w[None, :] >= l - lw)
        s_w = jnp.einsum('bqgrd,bkgd->bgrqk', qg, kvw[:, :, 0]).astype(jnp.float32) * scale
        s_w = s_w - slopes[:, :, None, None] * dist_w.astype(jnp.float32)
        p_w, _ = masked_softmax(s_w, mask_w)
        o_w = jnp.einsum('bgrqk,bkgd->bqgrd', p_w.astype(dt), kvw[:, :, 1])
        gb = gb.reshape(b, qb_size, g, r, 3)
        o = gb[..., 0:1] * o_c + gb[..., 1:2] * o_s + gb[..., 2:3] * o_w
        return o.reshape(b, qb_size, hq, hd)

    qs = q.reshape(b, nq, qb_size, hq, hd).transpose(1, 0, 2, 3, 4)
    gs = gates.reshape(b, nq, qb_size, hq, 3).transpose(1, 0, 2, 3, 4)
    out = lax.map(one_block, (qs, gs, jnp.arange(nq)))
    return out.transpose(1, 0, 2, 3, 4).reshape(b, tq, hq, hd)


def mixer_b_jax(proj, pos_emb, w_cmp, cache):
    b, t, _ = proj.shape
    q = proj[..., :D_MODEL].reshape(b, t, N_HEADS, HEAD_DIM)
    o1 = D_MODEL + 6 * B_KV_WIDTH
    kv = lax.optimization_barrier(proj[..., D_MODEL:o1].reshape(b, t, 6, B_KV_HEADS, HEAD_DIM))
    branch_gates = jax.nn.sigmoid(proj[..., o1:o1 + 3 * N_HEADS].reshape(b, t, N_HEADS, 3))
    kv_full_new, kv_win_new = kv[:, :, :4], kv[:, :, 4:]
    if cache is None:
        kv_full, kv_win = kv_full_new, kv_win_new
        new_win = kv_win_new[:, t - min(B_WINDOW, t):]
    else:
        kv_full = jnp.concatenate([cache[0], kv_full_new], axis=1)
        kv_win = jnp.concatenate([cache[1], kv_win_new], axis=1)
        new_win = kv_win[:, t:]
    o = nsa_attention(q, kv_full, kv_win, branch_gates, pos_emb, w_cmp)
    return o.reshape(b, t, D_MODEL), [kv_full_new, new_win]


def fox_attention(q, k, v, cum):
    b, tq, h, hd = q.shape
    l = k.shape[1]
    scale = hd ** -0.5
    qb_size = C_QBLOCK if tq % C_QBLOCK == 0 else tq
    nq = tq // qb_size
    cum_k = cum.transpose(0, 2, 1)
    cum_q = cum[:, l - tq:].reshape(b, nq, qb_size, h).transpose(1, 0, 3, 2)
    kpos = jnp.arange(l)

    def one_block(args):
        qb, cq, j = args
        pos = l - tq + j * qb_size + jnp.arange(qb_size)
        s = jnp.einsum('bqhd,bkhd->bhqk', qb, k).astype(jnp.float32) * scale
        s = s + cq[..., None] - cum_k[:, :, None, :]
        p, _ = masked_softmax(s, kpos[None, :] <= pos[:, None])
        return jnp.einsum('bhqk,bkhd->bqhd', p.astype(v.dtype), v)

    qs = q.reshape(b, nq, qb_size, h, hd).transpose(1, 0, 2, 3, 4)
    out = lax.map(one_block, (qs, cum_q, jnp.arange(nq)))
    return out.transpose(1, 0, 2, 3, 4).reshape(b, tq, h, hd)


def mixer_c_jax(proj, b_forget, cache):
    b, t, _ = proj.shape
    qkv = proj[..., :3 * D_MODEL].reshape(b, t, 3, N_HEADS, HEAD_DIM)
    logf = jax.nn.log_sigmoid(proj[..., 3 * D_MODEL:3 * D_MODEL + N_HEADS].astype(jnp.float32)
                              + b_forget.astype(jnp.float32))
    kv_new = qkv[:, :, 1:]
    if cache is None:
        kv_all, logf_all = kv_new, logf
    else:
        kv_all = jnp.concatenate([cache[0], kv_new], axis=1)
        logf_all = jnp.concatenate([cache[1].astype(jnp.float32), logf], axis=1)
    cum = jnp.cumsum(logf_all, axis=1)
    o = fox_attention(qkv[:, :, 0], kv_all[:, :, 0], kv_all[:, :, 1], cum)
    return o.reshape(b, t, D_MODEL), [kv_new, logf]


def _pad_cols(w, e_pad):
    return jnp.pad(w, ((0, 0), (0, e_pad - w.shape[1])))


def run_trunk(x, mods, layer_cache, weights, *, bt, tt):
    (norm_pre, norm_post, a_w_in, a_w_out, b_w_in, b_pos_emb, b_w_cmp, b_w_out,
     c_w_in, c_b_forget, c_w_out) = weights
    states = []
    for i in range(DEPTH):
        kind, j = i % N_MIXERS, i // N_MIXERS
        mod = mods[i][:, None, :]
        cache = layer_cache(i)
        if kind == 0:
            w_in = a_w_in[j].astype(BF16)
            proj = in_projection(x, mod, norm_pre[i], w_in, bt=bt, tt=tt, tn=1024)
            o, st = mixer_a_jax(proj, cache)
            gate_proj, gate_col, w_out = proj, 9, a_w_out[j]
        elif kind == 1:
            w = b_w_in[j]
            w_in = jnp.concatenate([_pad_cols(w, 4096), w[:, 2608:3632]], axis=1).astype(BF16)
            proj = in_projection(x, mod, norm_pre[i], w_in, bt=bt, tt=tt, tn=1024)
            o, st = mixer_b_jax(proj, b_pos_emb[j], b_w_cmp[j], cache)
            gate_proj, gate_col, w_out = proj, 4, b_w_out[j]
        else:
            w = c_w_in[j]
            w_in = jnp.concatenate([w[:, :3072], w[:, 3088:4112], _pad_cols(w[:, 3072:3088], 1024)],
                                   axis=1).astype(BF16)
            proj = in_projection(x, mod, norm_pre[i], w_in, bt=bt, tt=tt, tn=1024)
            proj_orig = jnp.concatenate([proj[..., :3072], proj[..., 4096:4112]], axis=-1)
            o, st = mixer_c_jax(proj_orig, c_b_forget[j], cache)
            gate_proj, gate_col, w_out = proj, 3, c_w_out[j]
        x = out_projection(o, gate_proj, gate_col, x, mod, norm_post[i], w_out.astype(BF16),
                           bt=bt, tt=tt)
        states.append(st)
    return x, states


def stack_state(states, kind, k):
    return jnp.stack([states[i][k] for i in range(DEPTH) if i % N_MIXERS == kind])


def kernel(x_prompt, x_sample, c_prompt, c_sample, cache_a_w128, cache_a_w512, cache_a_w2048,
           cache_b_kv, cache_b_win, cache_c_kv, cache_c_logf, page_table,
           ada_w, ada_b, norm_pre, norm_post, a_w_in, a_w_out, b_w_in, b_pos_emb, b_w_cmp, b_w_out,
           c_w_in, c_b_forget, c_w_out):
    weights = (norm_pre, norm_post, a_w_in, a_w_out, b_w_in, b_pos_emb, b_w_cmp, b_w_out,
               c_w_in, c_b_forget, c_w_out)
    nbp, nbs = x_prompt.shape[0], x_sample.shape[0]
    nb_pad = -(-(nbp + nbs) // 8) * 8
    c_all = jnp.concatenate([c_prompt, c_sample,
                             jnp.zeros((nb_pad - nbp - nbs, D_MODEL), F32)], axis=0)
    mods = ada_modulation(c_all, ada_w, ada_b)
    mods_p = mods[:, :nbp]
    mods_s = mods[:, nbp:nbp + nbs]

    def prompt_cache(i):
        return None

    def sample_cache(i):
        kind, j = i % N_MIXERS, i // N_MIXERS
        if kind == 0:
            return [cache_a_w128[j], cache_a_w512[j], cache_a_w2048[j]]
        if kind == 1:
            return [gather_pages(cache_b_kv[j], page_table), cache_b_win[j]]
        return [gather_pages(cache_c_kv[j], page_table), gather_pages(cache_c_logf[j], page_table)]

    y_prompt, st_p = run_trunk(x_prompt, mods_p, prompt_cache, weights, bt=1, tt=1024)
    y_sample, st_s = run_trunk(x_sample, mods_s, sample_cache, weights, bt=nbs, tt=x_sample.shape[1])

    outs = [y_prompt, y_sample]
    for st in (st_p, st_s):
        outs += [stack_state(st, 0, 0), stack_state(st, 0, 1), stack_state(st, 0, 2),
                 stack_state(st, 1, 0), stack_state(st, 1, 1),
                 stack_state(st, 2, 0), stack_state(st, 2, 1)]
    return tuple(outs)
```

```python
import functools

import jax
import jax.numpy as jnp
import numpy as np
from jax import lax
from jax.experimental import pallas as pl
from jax.experimental.pallas import tpu as pltpu

D_MODEL = 1024
DEPTH = 4
N_MIXERS = 3
HEAD_DIM = 64
RMS_EPS = 1e-6
N_HEADS = D_MODEL // HEAD_DIM
A_GROUPS = ((128, 1), (512, 4), (2048, 16))
A_WIDTH = D_MODEL
B_KV_HEADS = N_HEADS // 4
B_KV_WIDTH = B_KV_HEADS * HEAD_DIM
B_BLOCK = 64
B_TOPK = 16
B_WINDOW = 512
B_QBLOCK = 64
C_QBLOCK = 128
PAGE_SIZE = 128

VMEM_LIMIT = 56 * 1024 * 1024

F32 = jnp.float32
BF16 = jnp.bfloat16


def _sigmoid(x):
    return 1.0 / (1.0 + jnp.exp(-x))


def _ada_kernel(c_ref, w_ref, b_ref, o_ref):
    c = c_ref[...]
    s = c * _sigmoid(c)
    o_ref[...] = jnp.dot(s, w_ref[...], preferred_element_type=F32,
                         precision=lax.Precision.HIGHEST) + b_ref[...]


def ada_modulation(c_all, ada_w, ada_b):
    nb = c_all.shape[0]
    tn = 1024
    return pl.pallas_call(
        _ada_kernel,
        out_shape=jax.ShapeDtypeStruct((DEPTH, nb, 3 * D_MODEL), F32),
        grid=(DEPTH, 3 * D_MODEL // tn),
        in_specs=[
            pl.BlockSpec((nb, D_MODEL), lambda i, n: (0, 0)),
            pl.BlockSpec((None, D_MODEL, tn), lambda i, n: (i, 0, n)),
            pl.BlockSpec((None, 1, tn), lambda i, n: (i, 0, n)),
        ],
        out_specs=pl.BlockSpec((None, nb, tn), lambda i, n: (i, 0, n)),
        compiler_params=pltpu.CompilerParams(vmem_limit_bytes=VMEM_LIMIT),
        name="ada_modulation",
    )(c_all, ada_w, ada_b.reshape(DEPTH, 1, 3 * D_MODEL))


def _inproj_kernel(x_ref, shift_ref, scale_ref, g_ref, w_ref, o_ref, h_ref):
    bt, tt, d = x_ref.shape

    @pl.when(pl.program_id(1) == 0)
    def _():
        x = x_ref[...]
        ms = jnp.mean(x * x, axis=-1, keepdims=True)
        y = x * lax.rsqrt(ms + RMS_EPS) * g_ref[...]
        h = y * (1.0 + scale_ref[...]) + shift_ref[...]
        h_ref[...] = h.reshape(bt * tt, d).astype(BF16)

    o_ref[...] = jnp.dot(h_ref[...], w_ref[...],
                         preferred_element_type=F32).reshape(o_ref.shape)


def in_projection(x, mod, g, w_bf16, *, bt, tt, tn):
    b, t, d = x.shape
    e = w_bf16.shape[1]
    assert b % bt == 0 and t % tt == 0 and e % tn == 0
    nt = t // tt
    return pl.pallas_call(
        _inproj_kernel,
        out_shape=jax.ShapeDtypeStruct((b, t, e), F32),
        grid=(b // bt * nt, e // tn),
        in_specs=[
            pl.BlockSpec((bt, tt, d), lambda m, n: (m // nt, m % nt, 0)),
            pl.BlockSpec((bt, 1, d), lambda m, n: (m // nt, 0, 0)),
            pl.BlockSpec((bt, 1, d), lambda m, n: (m // nt, 0, 1)),
            pl.BlockSpec((1, d), lambda m, n: (0, 0)),
            pl.BlockSpec((d, tn), lambda m, n: (0, n)),
        ],
        out_specs=pl.BlockSpec((bt, tt, tn), lambda m, n: (m // nt, m % nt, n)),
        scratch_shapes=[pltpu.VMEM((bt * tt, d), BF16)],
        compiler_params=pltpu.CompilerParams(
            dimension_semantics=("arbitrary", "arbitrary"),
            vmem_limit_bytes=VMEM_LIMIT),
        name="in_projection",
    )(x, mod, mod, g.reshape(1, d), w_bf16)


def _outproj_kernel(o_ref, gate_ref, x_ref, mg_ref, g_ref, w_ref, out_ref):
    bt, tt, d = x_ref.shape
    gt = gate_ref[...]
    og = (o_ref[...] * (gt * _sigmoid(gt))).reshape(bt * tt, d).astype(BF16)
    y = jnp.dot(og, w_ref[...], preferred_element_type=F32)
    ms = jnp.mean(y * y, axis=-1, keepdims=True)
    yn = (y * lax.rsqrt(ms + RMS_EPS) * g_ref[...]).reshape(bt, tt, d)
    out_ref[...] = x_ref[...] + mg_ref[...] * yn


def out_projection(o, proj, gate_col, x, mod, g, w_bf16, *, bt, tt):
    b, t, d = x.shape
    nt = t // tt
    return pl.pallas_call(
        _outproj_kernel,
        out_shape=jax.ShapeDtypeStruct((b, t, d), F32),
        grid=(b // bt * nt,),
        in_specs=[
            pl.BlockSpec((bt, tt, d), lambda m: (m // nt, m % nt, 0)),
            pl.BlockSpec((bt, tt, d), lambda m: (m // nt, m % nt, gate_col)),
            pl.BlockSpec((bt, tt, d), lambda m: (m // nt, m % nt, 0)),
            pl.BlockSpec((bt, 1, d), lambda m: (m // nt, 0, 2)),
            pl.BlockSpec((1, d), lambda m: (0, 0)),
            pl.BlockSpec((d, d), lambda m: (0, 0)),
        ],
        out_specs=pl.BlockSpec((bt, tt, d), lambda m: (m // nt, m % nt, 0)),
        compiler_params=pltpu.CompilerParams(
            dimension_semantics=("arbitrary",),
            vmem_limit_bytes=VMEM_LIMIT),
        name="out_projection",
    )(o, proj, x, mod, g.reshape(1, d), w_bf16)


MXU = BF16
NEG = -1e30
HIGHEST = lax.Precision.HIGHEST


def _slope(h):
    return float(np.float32(2.0 ** (-8.0 * (h + 1) / N_HEADS)))


def _dot_nt(a, b):
    return lax.dot_general(a, b, (((1,), (1,)), ((), ())), preferred_element_type=F32)


def _dot_tn(a, b):
    return lax.dot_general(a, b, (((0,), (0,)), ((), ())), preferred_element_type=F32)


def _flash_rows(q4, slope4, pos4, k_b, v_b, j_lo, j_hi, mask_fn, m_ref, l_ref, acc_ref, tk=128):
    m_ref[...] = jnp.full(m_ref.shape, NEG, F32)
    l_ref[...] = jnp.zeros(l_ref.shape, F32)
    acc_ref[...] = jnp.zeros(acc_ref.shape, F32)

    def body(j, c):
        off = pl.multiple_of(j * tk, tk)
        kt = k_b[pl.ds(off, tk), :]
        vt = v_b[pl.ds(off, tk), :]
        s = _dot_nt(q4, kt)
        kpos = off + lax.broadcasted_iota(jnp.int32, (1, tk), 1)
        dist = pos4 - kpos
        s = s - slope4 * dist.astype(F32)
        valid = mask_fn(dist, off)
        sm = jnp.where(valid, s, NEG)
        m_old = m_ref[...]
        m_new = jnp.maximum(m_old, jnp.max(sm, axis=1, keepdims=True))
        p = jnp.where(valid, jnp.exp(sm - m_new), 0.0)
        alpha = jnp.exp(m_old - m_new)
        l_ref[...] = alpha * l_ref[...] + jnp.sum(p, axis=1, keepdims=True)
        acc_ref[...] = alpha * acc_ref[...] + jnp.dot(p.astype(MXU), vt, preferred_element_type=F32)
        m_ref[...] = m_new
        return c

    lax.fori_loop(j_lo, j_hi, body, 0)
    return acc_ref[...] / l_ref[...]


B_E = 4096
B_KV_OFF = 2048
B_BG_OFF = 3584


def _b_in_perm():
    perm = np.zeros(3632, np.int64)
    for r in range(4):
        for g in range(4):
            h = 4 * g + r
            for d in range(64):
                perm[r * 256 + g * 64 + d] = h * 64 + d
                perm[1024 + r * 256 + g * 64 + d] = 2608 + h * 64 + d
    perm[2048:3584] = 1024 + np.arange(1536)
    for br in range(3):
        for r in range(4):
            for g in range(4):
                perm[3584 + br * 16 + r * 4 + g] = 2560 + (4 * g + r) * 3 + br
    return perm


def _b_out_perm():
    perm = np.zeros(1024, np.int64)
    for r in range(4):
        for g in range(4):
            for d in range(64):
                perm[r * 256 + g * 64 + d] = (4 * g + r) * 64 + d
    return perm


def _nsa_cmp_kernel(x_ref, pe_ref, w_ref, o_ref):
    tt = x_ref.shape[0]
    nb = tt // B_BLOCK
    x = x_ref[...].reshape(nb, B_BLOCK, 512) + pe_ref[...][None]
    pooled = jnp.sum(x, axis=1) * (1.0 / B_BLOCK)
    o_ref[...] = jnp.dot(pooled, w_ref[...], preferred_element_type=F32, precision=HIGHEST)


def _nsa_cmp_weights(pos_emb, w_cmp):
    pe = pos_emb.transpose(1, 0, 2, 3).reshape(B_BLOCK, 512)
    wbd = jnp.zeros((512, 512), F32)
    for c in range(2):
        for g in range(4):
            o = c * 256 + g * 64
            wbd = wbd.at[o:o + 64, o:o + 64].set(w_cmp[c, g])
    return pe, wbd


def nsa_compress_prompt(proj, pe, wbd, *, tt=512):
    b, t, _ = proj.shape
    return pl.pallas_call(
        _nsa_cmp_kernel,
        out_shape=jax.ShapeDtypeStruct((b, t // B_BLOCK, 512), F32),
        grid=(b, t // tt),
        in_specs=[
            pl.BlockSpec((None, tt, 512), lambda i, j: (i, j, B_KV_OFF // 512)),
            pl.BlockSpec((B_BLOCK, 512), lambda i, j: (0, 0)),
            pl.BlockSpec((512, 512), lambda i, j: (0, 0)),
        ],
        out_specs=pl.BlockSpec((None, tt // B_BLOCK, 512), lambda i, j: (i, j, 0)),
        compiler_params=pltpu.CompilerParams(vmem_limit_bytes=VMEM_LIMIT),
        name="nsa_compress_prompt",
    )(proj, pe, wbd)


def _nsa_prompt_kernel(q_ref, ks_ref, vs_ref, kw_ref, vw_ref, cmp_ref, bg_ref, ex_ref, o_ref,
                       ksb, vsb, kwb, vwb, msk_ref, m_ref, l_ref, acc_ref, *, tq, t):
    qi = pl.program_id(1)
    nblk = t // B_BLOCK
    wt = B_WINDOW // tq

    @pl.when(qi == 0)
    def _():
        ksb[...] = ks_ref[...].astype(MXU)
        vsb[...] = vs_ref[...].astype(MXU)
        kwb[...] = kw_ref[...].astype(MXU)
        vwb[...] = vw_ref[...].astype(MXU)

    pos = qi * tq + lax.broadcasted_iota(jnp.int32, (tq, 1), 0)
    pos4 = jnp.concatenate([pos] * 4, axis=0)
    lane = lax.broadcasted_iota(jnp.int32, (1, 256), 1)
    bidx = lax.broadcasted_iota(jnp.int32, (1, nblk), 1)
    blk_end = (bidx + 1) * B_BLOCK - 1
    cur = pos // B_BLOCK
    bgate = _sigmoid(bg_ref[...])

    for g in range(4):
        gmask = (lane >= 64 * g) & (lane < 64 * (g + 1))
        q4 = jnp.concatenate(
            [jnp.where(gmask, q_ref[:, r * 256:(r + 1) * 256], 0.0) for r in range(4)], axis=0)
        q4 = (q4 * (HEAD_DIM ** -0.5)).astype(MXU)
        slope4 = jnp.concatenate(
            [jnp.full((tq, 1), _slope(4 * g + r), F32) for r in range(4)], axis=0)

        kc = cmp_ref[:, 0:256].astype(MXU)
        vc = cmp_ref[:, 256:512].astype(MXU)
        s_c = _dot_nt(q4, kc) - slope4 * (pos4 - blk_end).astype(F32)
        sm = jnp.where(blk_end <= pos4, s_c, -jnp.inf)
        mx = jnp.max(sm, axis=1, keepdims=True)
        mx = jnp.where(mx > -jnp.inf, mx, 0.0)
        e = jnp.exp(sm - mx)
        den = jnp.sum(e, axis=1, keepdims=True)
        p_c = e / jnp.where(den > 0, den, 1.0)
        o_c = jnp.dot(p_c.astype(MXU), vc, preferred_element_type=F32)

        sc = p_c[0:tq] + p_c[tq:2 * tq] + p_c[2 * tq:3 * tq] + p_c[3 * tq:4 * tq]
        forced = (bidx == 0) | (bidx >= cur - 1)
        score = jnp.where(bidx > cur, -jnp.inf, jnp.where(forced, jnp.inf, sc))
        rank = jnp.zeros((tq, nblk), F32)
        for n in range(nblk):
            col = score[:, n:n + 1]
            beats = (col > score) | ((col == score) & (bidx > n))
            rank = rank + jnp.where(beats, 1.0, 0.0)
        sel = jnp.where(rank < float(min(B_TOPK, nblk)), 1.0, 0.0)
        msk_ref[...] = jnp.dot(sel.astype(MXU), ex_ref[...], preferred_element_type=F32)

        def sel_mask(dist, off):
            mt = msk_ref[:, pl.ds(off, 128)]
            m4 = jnp.concatenate([mt] * 4, axis=0)
            return (dist >= 0) & (m4 > 0.5)

        def win_mask(dist, off):
            return (dist >= 0) & (dist <= B_WINDOW)

        o_s = _flash_rows(q4, slope4, pos4, ksb, vsb, 0, qi + 1, sel_mask, m_ref, l_ref, acc_ref)
        o_w = _flash_rows(q4, slope4, pos4, kwb, vwb, jnp.maximum(qi - wt, 0), qi + 1, win_mask,
                          m_ref, l_ref, acc_ref)

        for r in range(4):
            rs = slice(r * tq, (r + 1) * tq)
            c0 = r * 4 + g
            o_rg = (bgate[:, c0:c0 + 1] * o_c[rs] + bgate[:, 16 + c0:17 + c0] * o_s[rs]
                    + bgate[:, 32 + c0:33 + c0] * o_w[rs])
            contrib = jnp.where(gmask, o_rg, 0.0)
            if g == 0:
                o_ref[:, r * 256:(r + 1) * 256] = contrib
            else:
                o_ref[:, r * 256:(r + 1) * 256] += contrib


def _nsa_expand(t):
    nblk = t // B_BLOCK
    ex = (np.arange(t)[None, :] // B_BLOCK == np.arange(nblk)[:, None])
    return jnp.asarray(ex, MXU)


def nsa_attention_prompt(proj, cmp, *, tq=128):
    b, t, _ = proj.shape
    assert tq == 128 and t % tq == 0
    kv = lambda c: pl.BlockSpec((None, t, 256), lambda i, j, c=c: (i, 0, B_KV_OFF // 256 + c))
    return pl.pallas_call(
        functools.partial(_nsa_prompt_kernel, tq=tq, t=t),
        out_shape=jax.ShapeDtypeStruct((b, t, D_MODEL), F32),
        grid=(b, t // tq),
        in_specs=[
            pl.BlockSpec((None, tq, 1024), lambda i, j: (i, j, 0)),
            kv(2), kv(3), kv(4), kv(5),
            pl.BlockSpec((None, t // B_BLOCK, 512), lambda i, j: (i, 0, 0)),
            pl.BlockSpec((None, tq, 128), lambda i, j: (i, j, B_BG_OFF // 128)),
            pl.BlockSpec((t // B_BLOCK, t), lambda i, j: (0, 0)),
        ],
        out_specs=pl.BlockSpec((None, tq, D_MODEL), lambda i, j: (i, j, 0)),
        scratch_shapes=[pltpu.VMEM((t, 256), MXU)] * 4 + [
            pltpu.VMEM((tq, t), F32),
            pltpu.VMEM((4 * tq, 1), F32), pltpu.VMEM((4 * tq, 1), F32),
            pltpu.VMEM((4 * tq, 256), F32)],
        compiler_params=pltpu.CompilerParams(
            dimension_semantics=("arbitrary", "arbitrary"),
            vmem_limit_bytes=VMEM_LIMIT),
        name="nsa_attention_prompt",
    )(proj, proj, proj, proj, proj, cmp, proj, _nsa_expand(t))


def mixer_b_prompt(proj, pos_emb, w_cmp):
    b, t, _ = proj.shape
    pe, wbd = _nsa_cmp_weights(pos_emb, w_cmp)
    cmp = nsa_compress_prompt(proj, pe, wbd)
    o = nsa_attention_prompt(proj, cmp)
    kv_full_new = proj[..., B_KV_OFF:B_KV_OFF + 1024].reshape(b, t, 4, B_KV_HEADS, HEAD_DIM)
    new_win = proj[:, t - min(B_WINDOW, t):, B_KV_OFF + 1024:B_KV_OFF + 1536].reshape(
        b, min(B_WINDOW, t), 2, B_KV_HEADS, HEAD_DIM)
    return o, [kv_full_new, new_win]


C_E = 4224
C_F_OFF = 4096


def _log_sigmoid(x):
    return jnp.minimum(x, 0.0) - jnp.log(1.0 + jnp.exp(-jnp.abs(x)))


def _fox_prep_kernel(x_ref, bf_ref, logf_ref, cum_ref, cumt_ref, carry_ref):
    @pl.when(pl.program_id(1) == 0)
    def _():
        carry_ref[...] = jnp.zeros(carry_ref.shape, F32)

    n = x_ref.shape[0]
    logf = _log_sigmoid(x_ref[...] + bf_ref[...])
    tri = jnp.where(lax.broadcasted_iota(jnp.int32, (n, n), 1) <= lax.broadcasted_iota(jnp.int32, (n, n), 0),
                    1.0, 0.0)
    cum = jnp.dot(tri, logf, preferred_element_type=F32, precision=HIGHEST) + carry_ref[...]
    logf_ref[...] = logf
    cum_ref[...] = cum
    cumt_ref[...] = cum.T[0:N_HEADS, :]
    carry_ref[...] = cum[n - 1:n, :]


def fox_prep(proj, b_forget, *, tt=128):
    b, t, _ = proj.shape
    bf = jnp.pad(b_forget.astype(F32), (0, 128 - N_HEADS)).reshape(1, 128)
    return pl.pallas_call(
        _fox_prep_kernel,
        out_shape=[jax.ShapeDtypeStruct((b, t, 128), F32), jax.ShapeDtypeStruct((b, t, 128), F32),
                   jax.ShapeDtypeStruct((b, N_HEADS, t), F32)],
        grid=(b, t // tt),
        in_specs=[pl.BlockSpec((None, tt, 128), lambda i, j: (i, j, C_F_OFF // 128)),
                  pl.BlockSpec((1, 128), lambda i, j: (0, 0))],
        out_specs=[pl.BlockSpec((None, tt, 128), lambda i, j: (i, j, 0)),
                   pl.BlockSpec((None, tt, 128), lambda i, j: (i, j, 0)),
                   pl.BlockSpec((None, N_HEADS, tt), lambda i, j: (i, 0, j))],
        scratch_shapes=[pltpu.VMEM((1, 128), F32)],
        compiler_params=pltpu.CompilerParams(
            dimension_semantics=("arbitrary", "arbitrary"), vmem_limit_bytes=VMEM_LIMIT),
        name="fox_prep",
    )(proj, bf)


def _fox_prompt_kernel(q_ref, k_ref, v_ref, cq_ref, ck_ref, o_ref, kb, vb, m_ref, l_ref, acc_ref, *, tq):
    hp = pl.program_id(1)
    qi = pl.program_id(2)

    @pl.when(qi == 0)
    def _():
        kb[...] = k_ref[...].astype(MXU)
        vb[...] = v_ref[...].astype(MXU)

    lane = lax.broadcasted_iota(jnp.int32, (1, 128), 1)
    pos = qi * tq + lax.broadcasted_iota(jnp.int32, (tq, 1), 0)
    cqb = cq_ref[...]
    outs = []
    for hh in range(2):
        h = 2 * hp + hh
        hmask = (lane >= 64 * hh) & (lane < 64 * (hh + 1))
        qm = (jnp.where(hmask, q_ref[...], 0.0) * (HEAD_DIM ** -0.5)).astype(MXU)
        cq = jnp.sum(jnp.where(lane == h, cqb, 0.0), axis=1, keepdims=True)
        m_ref[...] = jnp.full(m_ref.shape, NEG, F32)
        l_ref[...] = jnp.zeros(l_ref.shape, F32)
        acc_ref[...] = jnp.zeros(acc_ref.shape, F32)

        def body(j, c):
            off = pl.multiple_of(j * tq, tq)
            kt = kb[pl.ds(off, tq), :]
            vt = vb[pl.ds(off, tq), :]
            ck = ck_ref[pl.ds(h, 1), pl.ds(off, tq)]
            s = (_dot_nt(qm, kt) + cq) - ck
            valid = (off + lax.broadcasted_iota(jnp.int32, (1, tq), 1)) <= pos
            sm = jnp.where(valid, s, NEG)
            m_old = m_ref[...]
            m_new = jnp.maximum(m_old, jnp.max(sm, axis=1, keepdims=True))
            p = jnp.where(valid, jnp.exp(sm - m_new), 0.0)
            alpha = jnp.exp(m_old - m_new)
            l_ref[...] = alpha * l_ref[...] + jnp.sum(p, axis=1, keepdims=True)
            acc_ref[...] = alpha * acc_ref[...] + jnp.dot(p.astype(MXU), vt, preferred_element_type=F32)
            m_ref[...] = m_new
            return c

        lax.fori_loop(0, qi + 1, body, 0)
        outs.append(acc_ref[...] / l_ref[...])
    o_ref[...] = jnp.where(lane < 64, outs[0], outs[1])


def fox_attention_prompt(proj, cum, cumt, *, tq=256):
    b, t, _ = proj.shape
    return pl.pallas_call(
        functools.partial(_fox_prompt_kernel, tq=tq),
        out_shape=jax.ShapeDtypeStruct((b, t, D_MODEL), F32),
        grid=(b, N_HEADS // 2, t // tq),
        in_specs=[
            pl.BlockSpec((None, tq, 128), lambda i, p, j: (i, j, p)),
            pl.BlockSpec((None, t, 128), lambda i, p, j: (i, 0, 8 + p)),
            pl.BlockSpec((None, t, 128), lambda i, p, j: (i, 0, 16 + p)),
            pl.BlockSpec((None, tq, 128), lambda i, p, j: (i, j, 0)),
            pl.BlockSpec((None, N_HEADS, t), lambda i, p, j: (i, 0, 0)),
        ],
        out_specs=pl.BlockSpec((None, tq, 128), lambda i, p, j: (i, j, p)),
        scratch_shapes=[pltpu.VMEM((t, 128), MXU), pltpu.VMEM((t, 128), MXU),
                        pltpu.VMEM((tq, 1), F32), pltpu.VMEM((tq, 1), F32), pltpu.VMEM((tq, 128), F32)],
        compiler_params=pltpu.CompilerParams(
            dimension_semantics=("arbitrary", "arbitrary", "arbitrary"), vmem_limit_bytes=VMEM_LIMIT),
        name="fox_attention_prompt",
    )(proj, proj, proj, cum, cumt)


def mixer_c_prompt(proj, b_forget):
    b, t, _ = proj.shape
    logf, cum, cumt = fox_prep(proj, b_forget)
    o = fox_attention_prompt(proj, cum, cumt)
    kv_new = proj[..., 1024:3072].reshape(b, t, 2, N_HEADS, HEAD_DIM)
    return o, [kv_new, logf[..., :N_HEADS]]


A_E = 10240
A_NK = 128


def _dil_prompt_kernel(*refs, first, last, dil):
    if first:
        q_ref, kc_ref, kp_ref, vc_ref, vp_ref = refs[:5]
        outs = refs[5:]
    else:
        q_ref, kc_ref, kp_ref, vc_ref, vp_ref, m_in, l_in, acc_in = refs[:8]
        outs = refs[8:]
    if last:
        (acc_out,) = outs
    else:
        m_out, l_out, acc_out = outs
    i = pl.program_id(2)
    nk = A_NK
    step = (nk + lax.broadcasted_iota(jnp.int32, (nk, 1), 0)) - lax.broadcasted_iota(jnp.int32, (1, 2 * nk), 1)
    kcol = lax.broadcasted_iota(jnp.int32, (1, 2 * nk), 1)
    valid = (step >= 0) & (step <= nk) & ((kcol >= nk) | (i > 0))
    distf = (step * dil).astype(F32)
    lane = lax.broadcasted_iota(jnp.int32, (1, 128), 1)
    if not last:
        m_out[...] = jnp.zeros(m_out.shape, F32)
        l_out[...] = jnp.zeros(l_out.shape, F32)
    for hp in range(N_HEADS // 2):
        cs = slice(128 * hp, 128 * (hp + 1))
        q2 = q_ref[:, cs]
        k2 = jnp.concatenate([kp_ref[:, cs], kc_ref[:, cs]], axis=0).astype(MXU)
        v2 = jnp.concatenate([vp_ref[:, cs], vc_ref[:, cs]], axis=0).astype(MXU)
        res = []
        for hh in range(2):
            h = 2 * hp + hh
            hmask = (lane >= 64 * hh) & (lane < 64 * (hh + 1))
            qm = (jnp.where(hmask, q2, 0.0) * (HEAD_DIM ** -0.5)).astype(MXU)
            s = _dot_nt(qm, k2) - _slope(h) * distf
            sm = jnp.where(valid, s, NEG)
            mx = jnp.max(sm, axis=1, keepdims=True)
            if first:
                m_new = mx
            else:
                m_old = m_in[:, h:h + 1]
                m_new = jnp.maximum(m_old, mx)
            p = jnp.where(valid, jnp.exp(sm - m_new), 0.0)
            l_new = jnp.sum(p, axis=1, keepdims=True)
            acc = jnp.dot(p.astype(MXU), v2, preferred_element_type=F32)
            if not first:
                alpha = jnp.exp(m_old - m_new)
                l_new = alpha * l_in[:, h:h + 1] + l_new
                acc = alpha * acc_in[:, cs] + acc
            if last:
                acc = acc / l_new
            else:
                m_out[:, h:h + 1] = m_new
                l_out[:, h:h + 1] = l_new
            res.append(acc)
        acc_out[:, cs] = jnp.where(lane < 64, res[0], res[1])


def dilated_attention_prompt(proj):
    b, t, _ = proj.shape
    nk = A_NK
    state = None
    for g, (window, dil) in enumerate(A_GROUPS):
        assert window // dil == nk and t % (dil * nk) == 0
        ln = t // dil
        first, last = g == 0, g == len(A_GROUPS) - 1
        pv = proj.reshape(b, ln, dil * A_E)
        col = lambda c, prev: pl.BlockSpec(
            (None, nk, 1024),
            (lambda bi, r, i, c=c: (bi, jnp.maximum(i - 1, 0), r * (A_E // 1024) + c)) if prev else
            (lambda bi, r, i, c=c: (bi, i, r * (A_E // 1024) + c)))
        st_spec = pl.BlockSpec((None, nk, 128), lambda bi, r, i: (bi, i, r))
        acc_spec = pl.BlockSpec((None, nk, 1024), lambda bi, r, i: (bi, i, r))
        in_specs = [col(3 * g, False), col(3 * g + 1, False), col(3 * g + 1, True),
                    col(3 * g + 2, False), col(3 * g + 2, True)]
        args = [pv] * 5
        if not first:
            in_specs += [st_spec, st_spec, acc_spec]
            args += [state[0].reshape(b, ln, dil * 128), state[1].reshape(b, ln, dil * 128),
                     state[2].reshape(b, ln, dil * 1024)]
        acc_shape = jax.ShapeDtypeStruct((b, ln, dil * 1024), F32)
        st_shape = jax.ShapeDtypeStruct((b, ln, dil * 128), F32)
        out = pl.pallas_call(
            functools.partial(_dil_prompt_kernel, first=first, last=last, dil=dil),
            out_shape=[acc_shape] if last else [st_shape, st_shape, acc_shape],
            grid=(b, dil, ln // nk),
            in_specs=in_specs,
            out_specs=[acc_spec] if last else [st_spec, st_spec, acc_spec],
            compiler_params=pltpu.CompilerParams(
                dimension_semantics=("arbitrary", "arbitrary", "arbitrary"), vmem_limit_bytes=VMEM_LIMIT),
            name=f"dilated_attention_prompt_g{g}",
        )(*args)
        state = [o.reshape(b, t, -1) for o in out]
    return state[0]


def mixer_a_prompt(proj):
    b, t, _ = proj.shape
    o = dilated_attention_prompt(proj)
    new_state = []
    for g, (window, _) in enumerate(A_GROUPS):
        w = min(window, t)
        new_state.append(proj[:, t - w:, 3072 * g + 1024:3072 * g + 3072].reshape(b, w, 2, N_HEADS, HEAD_DIM))
    return o, new_state


STEP_T = 8
STEP_LANES = N_HEADS * STEP_T


def _lane_iota():
    return lax.broadcasted_iota(jnp.int32, (1, STEP_LANES), 1)


def _row2col(x):
    n = x.shape[1]
    eye = lax.broadcasted_iota(jnp.int32, (n, n), 0) == lax.broadcasted_iota(jnp.int32, (n, n), 1)
    return jnp.sum(jnp.where(eye, x, 0.0), axis=1, keepdims=True)


def _step_state_init(m_ref, l_ref, acc_ref):
    m_ref[...] = jnp.full(m_ref.shape, NEG, F32)
    l_ref[...] = jnp.zeros(l_ref.shape, F32)
    acc_ref[...] = jnp.zeros(acc_ref.shape, F32)


def _step_update(s_t, valid, v_b, m_ref, l_ref, acc_ref):
    sm = s_t if valid is None else jnp.where(valid, s_t, NEG)
    m_old = m_ref[...]
    m_new = jnp.maximum(m_old, jnp.max(sm, axis=0, keepdims=True))
    p_t = jnp.exp(sm - m_new)
    if valid is not None:
        p_t = jnp.where(valid, p_t, 0.0)
    alpha = jnp.exp(m_old - m_new)
    l_ref[...] = alpha * l_ref[...] + jnp.sum(p_t, axis=0, keepdims=True)
    acc_ref[...] = _row2col(alpha) * acc_ref[...] + _dot_tn(p_t.astype(MXU), v_b)
    m_ref[...] = m_new


def _qbd_full(q):
    lane = _lane_iota()
    sel = jnp.where(lax.broadcasted_iota(jnp.int32, (STEP_T, 1), 0) == (lane & (STEP_T - 1)), 1.0, 0.0)
    qall = _dot_tn(q.astype(MXU), sel.astype(MXU))
    row_h = lax.broadcasted_iota(jnp.int32, (D_MODEL, 1), 0) // HEAD_DIM
    return (jnp.where(row_h == (lane >> 3), qall, 0.0) * (HEAD_DIM ** -0.5)).astype(MXU)


def _extract_heads(o):
    row_h = lax.broadcasted_iota(jnp.int32, (STEP_LANES, 1), 0) >> 3
    col_h = lax.broadcasted_iota(jnp.int32, (1, D_MODEL), 1) // HEAD_DIM
    om = jnp.where(row_h == col_h, o, 0.0)
    return jnp.sum(om.reshape(N_HEADS, STEP_T, D_MODEL), axis=0)


def _step_slopes():
    return jnp.asarray(np.repeat(np.float32(2.0 ** (-8.0 * np.arange(1, N_HEADS + 1) / N_HEADS)), STEP_T)
                       .reshape(1, STEP_LANES), F32)


A_STEP_TILE = 512


def _dil_step_kernel(slope_ref, new_ref, c2_ref, c1_ref, c0_ref, o_ref, qbd_ref, m_ref, l_ref, acc_ref):
    s = pl.program_id(1)
    n2 = A_GROUPS[2][0] // A_STEP_TILE
    ilane = _lane_iota() & (STEP_T - 1)
    slope = slope_ref[...]

    @pl.when(s == 0)
    def _():
        _step_state_init(m_ref, l_ref, acc_ref)
        for g in range(3):
            qbd_ref[g] = _qbd_full(new_ref[:, 3072 * g:3072 * g + 1024])

    def tile(k, v, g, row0):
        window, dil = A_GROUPS[g]
        rows = k.shape[0]
        s_t = jnp.dot(k.astype(MXU), qbd_ref[g], preferred_element_type=F32)
        kpos = row0 + lax.broadcasted_iota(jnp.int32, (rows, 1), 0)
        dist = (window + ilane) - kpos
        valid = (dist >= 0) & (dist <= window) & ((dist & (dil - 1)) == 0)
        _step_update(s_t - slope * dist.astype(F32), valid, v.astype(MXU), m_ref, l_ref, acc_ref)

    @pl.when(s < n2)
    def _():
        tile(c2_ref[:, 0:1024], c2_ref[:, 1024:2048], 2, s * A_STEP_TILE)

    @pl.when(s == n2)
    def _():
        tile(c1_ref[:, 0:1024], c1_ref[:, 1024:2048], 1, 0)

    @pl.when(s == n2 + 1)
    def _():
        tile(c0_ref[:, 0:1024], c0_ref[:, 1024:2048], 0, 0)
        for g in range(3):
            o = 3072 * g
            tile(new_ref[:, o + 1024:o + 2048], new_ref[:, o + 2048:o + 3072], g, A_GROUPS[g][0])
        o_ref[...] = _extract_heads(acc_ref[...] / _row2col(l_ref[...]))


def mixer_a_step(proj, cache):
    b, t, _ = proj.shape
    assert t == STEP_T and A_GROUPS[1][0] == A_STEP_TILE
    c0, c1, c2 = [c.reshape(b, c.shape[1], 2 * D_MODEL) for c in cache]
    n2 = A_GROUPS[2][0] // A_STEP_TILE
    o = pl.pallas_call(
        _dil_step_kernel,
        out_shape=jax.ShapeDtypeStruct((b, t, D_MODEL), F32),
        grid=(b, n2 + 2),
        in_specs=[
            pl.BlockSpec((1, STEP_LANES), lambda i, s: (0, 0)),
            pl.BlockSpec((None, t, A_E), lambda i, s: (i, 0, 0)),
            pl.BlockSpec((None, A_STEP_TILE, 2 * D_MODEL), lambda i, s: (i, jnp.minimum(s, n2 - 1), 0)),
            pl.BlockSpec((None, A_STEP_TILE, 2 * D_MODEL), lambda i, s: (i, 0, 0)),
            pl.BlockSpec((None, A_GROUPS[0][0], 2 * D_MODEL), lambda i, s: (i, 0, 0)),
        ],
        out_specs=pl.BlockSpec((None, t, D_MODEL), lambda i, s: (i, 0, 0)),
        scratch_shapes=[pltpu.VMEM((3, D_MODEL, STEP_LANES), MXU),
                        pltpu.VMEM((1, STEP_LANES), F32), pltpu.VMEM((1, STEP_LANES), F32),
                        pltpu.VMEM((STEP_LANES, D_MODEL), F32)],
        compiler_params=pltpu.CompilerParams(
            dimension_semantics=("arbitrary", "arbitrary"), vmem_limit_bytes=VMEM_LIMIT),
        name="dilated_attention_step",
    )(_step_slopes(), proj, c2, c1, c0)
    new_state = []
    for g in range(3):
        kv_new = proj[..., 3072 * g + 1024:3072 * g + 3072].reshape(b, t, 2, N_HEADS, HEAD_DIM)
        new_state.append(jnp.concatenate([cache[g][:, t:], kv_new], axis=1))
    return o, new_state


C_STEP_PAGES = 8


def _fox_step_kernel(pt_ref, new_ref, bf_ref, *refs):
    npg = C_STEP_PAGES
    kv_refs = refs[:npg]
    lf_refs = refs[npg:2 * npg]
    o_ref, lfo_ref, qbd_ref, m_ref, l_ref, acc_ref, carry_ref = refs[2 * npg:]
    s = pl.program_id(1)
    lane = _lane_iota()

    @pl.when(s == 0)
    def _():
        _step_state_init(m_ref, l_ref, acc_ref)
        carry_ref[...] = jnp.zeros(carry_ref.shape, F32)
        qbd_ref[...] = _qbd_full(new_ref[:, 0:1024])

    expand = jnp.where(lax.broadcasted_iota(jnp.int32, (128, 1), 0) == (lane >> 3), 1.0, 0.0)
    tri = jnp.where(lax.broadcasted_iota(jnp.int32, (PAGE_SIZE, PAGE_SIZE), 1)
                    <= lax.broadcasted_iota(jnp.int32, (PAGE_SIZE, PAGE_SIZE), 0), 1.0, 0.0)
    for p in range(npg):
        lfe = jnp.dot(lf_refs[p][...], expand[0:N_HEADS, :], preferred_element_type=F32, precision=HIGHEST)
        ck = jnp.dot(tri, lfe, preferred_element_type=F32, precision=HIGHEST) + carry_ref[...]
        carry_ref[...] = ck[PAGE_SIZE - 1:PAGE_SIZE, :]
        s_t = jnp.dot(kv_refs[p][:, 0:1024].astype(MXU), qbd_ref[...], preferred_element_type=F32) - ck
        _step_update(s_t, None, kv_refs[p][:, 1024:2048].astype(MXU), m_ref, l_ref, acc_ref)

    @pl.when(s == pl.num_programs(1) - 1)
    def _():
        logf = _log_sigmoid(new_ref[:, C_F_OFF:C_F_OFF + 128] + bf_ref[...])
        lfo_ref[...] = logf
        lfe = jnp.dot(logf, expand, preferred_element_type=F32, precision=HIGHEST)
        row = lax.broadcasted_iota(jnp.int32, (STEP_T, 1), 0)
        tri8 = jnp.where(lax.broadcasted_iota(jnp.int32, (STEP_T, STEP_T), 1) <= row, 1.0, 0.0)
        ck = jnp.dot(tri8, lfe, preferred_element_type=F32, precision=HIGHEST) + carry_ref[...]
        s_t = jnp.dot(new_ref[:, 1024:2048].astype(MXU), qbd_ref[...], preferred_element_type=F32) - ck
        valid = row <= (lane & (STEP_T - 1))
        _step_update(s_t, valid, new_ref[:, 2048:3072].astype(MXU), m_ref, l_ref, acc_ref)
        o_ref[...] = _extract_heads(acc_ref[...] / _row2col(l_ref[...]))


def mixer_c_step(proj, b_forget, kv_pool, lf_pool, page_table):
    b, t, _ = proj.shape
    n_pages = page_table.shape[1]
    npg = C_STEP_PAGES
    assert t == STEP_T and n_pages % npg == 0
    kvp = kv_pool.reshape(kv_pool.shape[0], PAGE_SIZE, 2 * D_MODEL)
    bf = jnp.pad(b_forget.astype(F32), (0, 128 - N_HEADS)).reshape(1, 128)
    page = lambda shape, p: pl.BlockSpec(shape, lambda i, s, pt, p=p: (pt[i, s * npg + p], 0, 0))
    o, logf = pl.pallas_call(
        _fox_step_kernel,
        out_shape=[jax.ShapeDtypeStruct((b, t, D_MODEL), F32), jax.ShapeDtypeStruct((b, t, 128), F32)],
        grid_spec=pltpu.PrefetchScalarGridSpec(
            num_scalar_prefetch=1,
            grid=(b, n_pages // npg),
            in_specs=[pl.BlockSpec((None, t, C_E), lambda i, s, pt: (i, 0, 0)),
                      pl.BlockSpec((1, 128), lambda i, s, pt: (0, 0))]
            + [page((None, PAGE_SIZE, 2 * D_MODEL), p) for p in range(npg)]
            + [page((None, PAGE_SIZE, N_HEADS), p) for p in range(npg)],
            out_specs=[pl.BlockSpec((None, t, D_MODEL), lambda i, s, pt: (i, 0, 0)),
                       pl.BlockSpec((None, t, 128), lambda i, s, pt: (i, 0, 0))],
            scratch_shapes=[pltpu.VMEM((D_MODEL, STEP_LANES), MXU),
                            pltpu.VMEM((1, STEP_LANES), F32), pltpu.VMEM((1, STEP_LANES), F32),
                            pltpu.VMEM((STEP_LANES, D_MODEL), F32), pltpu.VMEM((1, STEP_LANES), F32)]),
        compiler_params=pltpu.CompilerParams(
            dimension_semantics=("arbitrary", "arbitrary"), vmem_limit_bytes=VMEM_LIMIT),
        name="fox_attention_step",
    )(page_table, proj, bf, *([kvp] * npg), *([lf_pool] * npg))
    kv_new = proj[..., 1024:3072].reshape(b, t, 2, N_HEADS, HEAD_DIM)
    return o, [kv_new, logf[..., :N_HEADS]]


B_CMP_PAGES = 4
B_STEP_PAGES = 8


def _nsa_cmp_pages_kernel(pt_ref, *refs):
    pages = refs[:B_CMP_PAGES]
    pe_ref, w_ref, o_ref = refs[B_CMP_PAGES:]
    nb = PAGE_SIZE // B_BLOCK
    pooled = []
    for p in range(B_CMP_PAGES):
        x = pages[p][...].reshape(nb, B_BLOCK, 512) + pe_ref[...][None]
        pooled.append(jnp.sum(x, axis=1) * (1.0 / B_BLOCK))
    o_ref[...] = jnp.dot(jnp.concatenate(pooled, axis=0), w_ref[...], preferred_element_type=F32,
                         precision=HIGHEST)


def nsa_compress_pages(kv_pool, page_table, pe, wbd):
    b, n_pages = page_table.shape
    npg = B_CMP_PAGES
    nb = PAGE_SIZE // B_BLOCK
    page = lambda p: pl.BlockSpec((None, PAGE_SIZE, 512), lambda i, s, pt, p=p: (pt[i, s * npg + p], 0, 0))
    return pl.pallas_call(
        _nsa_cmp_pages_kernel,
        out_shape=jax.ShapeDtypeStruct((b, n_pages * nb, 512), F32),
        grid_spec=pltpu.PrefetchScalarGridSpec(
            num_scalar_prefetch=1,
            grid=(b, n_pages // npg),
            in_specs=[page(p) for p in range(npg)]
            + [pl.BlockSpec((B_BLOCK, 512), lambda i, s, pt: (0, 0)),
               pl.BlockSpec((512, 512), lambda i, s, pt: (0, 0))],
            out_specs=pl.BlockSpec((None, npg * nb, 512), lambda i, s, pt: (i, s, 0))),
        compiler_params=pltpu.CompilerParams(
            dimension_semantics=("arbitrary", "arbitrary"), vmem_limit_bytes=VMEM_LIMIT),
        name="nsa_compress_pages",
    )(page_table, *([kv_pool] * npg), pe, wbd)


def _nsa_step_kernel(pt_ref, slope_ref, new_ref, cmp_ref, win_ref, *refs, past):
    npg = B_STEP_PAGES
    pages = refs[:npg]
    (o_ref, qbd_ref, sel_ref, oc_ref, ms_ref, ls_ref, accs_ref, mw_ref, lw_ref, accw_ref) = refs[npg:]
    s = pl.program_id(1)
    lane = _lane_iota()
    ilane = lane & (STEP_T - 1)
    glane = lane >> 5
    rlane = (lane >> 3) & 3
    slope = slope_ref[...]
    pos = past + ilane
    ncb = past // B_BLOCK
    nsel = ncb + 8
    kvo = B_KV_OFF

    @pl.when(s == 0)
    def _():
        irow = lax.broadcasted_iota(jnp.int32, (STEP_T, 1), 0)
        qall = jnp.zeros((256, STEP_LANES), F32)
        for r in range(4):
            sel_r = jnp.where((irow == ilane) & (rlane == r), 1.0, 0.0)
            qall = qall + _dot_tn(new_ref[:, r * 256:(r + 1) * 256].astype(MXU), sel_r.astype(MXU))
        row_g = lax.broadcasted_iota(jnp.int32, (256, 1), 0) // HEAD_DIM
        qbd_ref[...] = (jnp.where(row_g == glane, qall, 0.0) * (HEAD_DIM ** -0.5)).astype(MXU)

        brow = lax.broadcasted_iota(jnp.int32, (ncb, 1), 0)
        blk_end = (brow + 1) * B_BLOCK - 1
        s_c = jnp.dot(cmp_ref[:, 0:256].astype(MXU), qbd_ref[...], preferred_element_type=F32)
        s_c = s_c - slope * (pos - blk_end).astype(F32)
        sm = jnp.where(blk_end <= pos, s_c, -jnp.inf)
        mx = jnp.max(sm, axis=0, keepdims=True)
        mx = jnp.where(mx > -jnp.inf, mx, 0.0)
        e = jnp.exp(sm - mx)
        den = jnp.sum(e, axis=0, keepdims=True)
        p_c = e / jnp.where(den > 0, den, 1.0)
        oc_ref[...] = _dot_tn(p_c.astype(MXU), cmp_ref[:, 256:512].astype(MXU))

        lrow = lax.broadcasted_iota(jnp.int32, (STEP_LANES, 1), 0)
        same = jnp.where(((lrow >> 5) == glane) & ((lrow & (STEP_T - 1)) == ilane), 1.0, 0.0)
        sc = jnp.dot(p_c, same, preferred_element_type=F32, precision=HIGHEST)
        sc = jnp.concatenate([sc, jnp.zeros((nsel - ncb, STEP_LANES), F32)], axis=0)
        nidx = lax.broadcasted_iota(jnp.int32, (nsel, 1), 0)
        cur = pos // B_BLOCK
        forced = (nidx == 0) | (nidx >= cur - 1)
        score = jnp.where(nidx > cur, -jnp.inf, jnp.where(forced, jnp.inf, sc))
        rank = jnp.zeros((nsel, STEP_LANES), F32)
        for n in range(ncb + 1):
            row = score[n:n + 1, :]
            beats = (row > score) | ((row == score) & (nidx > n))
            rank = rank + jnp.where(beats, 1.0, 0.0)
        sel_ref[...] = jnp.where(rank < float(B_TOPK), 1.0, 0.0)
        _step_state_init(ms_ref, ls_ref, accs_ref)

    prow = lax.broadcasted_iota(jnp.int32, (PAGE_SIZE, 1), 0)
    for p in range(npg):
        pg = s * npg + p
        kpos = pg * PAGE_SIZE + prow
        s_t = jnp.dot(pages[p][:, 0:256].astype(MXU), qbd_ref[...], preferred_element_type=F32)
        s_t = s_t - slope * (pos - kpos).astype(F32)
        sel_lo = sel_ref[pl.ds(2 * pg, 1), :]
        sel_hi = sel_ref[pl.ds(2 * pg + 1, 1), :]
        valid = jnp.where(prow < B_BLOCK, sel_lo, sel_hi) > 0.5
        _step_update(s_t, valid, pages[p][:, 256:512].astype(MXU), ms_ref, ls_ref, accs_ref)

    @pl.when(s == pl.num_programs(1) - 1)
    def _():
        nrow = lax.broadcasted_iota(jnp.int32, (STEP_T, 1), 0)
        dist_n = ilane - nrow
        s_t = jnp.dot(new_ref[:, kvo + 512:kvo + 768].astype(MXU), qbd_ref[...], preferred_element_type=F32)
        s_t = s_t - slope * dist_n.astype(F32)
        valid = (dist_n >= 0) & (sel_ref[ncb:ncb + 1, :] > 0.5)
        _step_update(s_t, valid, new_ref[:, kvo + 768:kvo + 1024].astype(MXU), ms_ref, ls_ref, accs_ref)
        _step_state_init(mw_ref, lw_ref, accw_ref)
        lw = win_ref.shape[0]
        kpos = (past - lw) + lax.broadcasted_iota(jnp.int32, (lw, 1), 0)
        dist = pos - kpos
        s_t = jnp.dot(win_ref[:, 0:256].astype(MXU), qbd_ref[...], preferred_element_type=F32)
        s_t = s_t - slope * dist.astype(F32)
        _step_update(s_t, (dist >= 0) & (dist <= B_WINDOW), win_ref[:, 256:512].astype(MXU),
                     mw_ref, lw_ref, accw_ref)
        s_t = jnp.dot(new_ref[:, kvo + 1024:kvo + 1280].astype(MXU), qbd_ref[...], preferred_element_type=F32)
        s_t = s_t - slope * dist_n.astype(F32)
        _step_update(s_t, dist_n >= 0, new_ref[:, kvo + 1280:kvo + 1536].astype(MXU), mw_ref, lw_ref, accw_ref)
        lrow = lax.broadcasted_iota(jnp.int32, (STEP_LANES, 1), 0)
        pick = jnp.where((lrow & (STEP_T - 1)) == lax.broadcasted_iota(jnp.int32, (1, STEP_T), 1), 1.0, 0.0)
        gmat = jnp.dot(pick, _sigmoid(new_ref[:, B_BG_OFF:B_BG_OFF + 128]), preferred_element_type=F32,
                       precision=HIGHEST)
        gcol = ((lrow >> 3) & 3) * 4 + (lrow >> 5)
        lane128 = lax.broadcasted_iota(jnp.int32, (1, 128), 1)
        gate = lambda br: jnp.sum(jnp.where(lane128 == gcol + 16 * br, gmat, 0.0), axis=1, keepdims=True)
        o_all = (gate(0) * oc_ref[...] + gate(1) * (accs_ref[...] / _row2col(ls_ref[...]))
                 + gate(2) * (accw_ref[...] / _row2col(lw_ref[...])))
        col_g = lax.broadcasted_iota(jnp.int32, (1, 256), 1) // HEAD_DIM
        for r in range(4):
            keep = (((lrow >> 3) & 3) == r) & ((lrow >> 5) == col_g)
            om = jnp.where(keep, o_all, 0.0)
            o_ref[:, r * 256:(r + 1) * 256] = jnp.sum(om.reshape(N_HEADS, STEP_T, 256), axis=0)


def mixer_b_step(proj, pos_emb, w_cmp, kv_pool, win_cache, page_table):
    b, t, _ = proj.shape
    n_pages = page_table.shape[1]
    npg = B_STEP_PAGES
    past = n_pages * PAGE_SIZE
    assert t == STEP_T and n_pages % npg == 0 and n_pages % B_CMP_PAGES == 0
    ncb = past // B_BLOCK
    pool = kv_pool.reshape(kv_pool.shape[0], PAGE_SIZE, 1024)
    pe, wbd = _nsa_cmp_weights(pos_emb, w_cmp)
    cmp = nsa_compress_pages(pool, page_table, pe, wbd)
    lw = win_cache.shape[1]
    win = win_cache.reshape(b, lw, 512)
    page = lambda p: pl.BlockSpec((None, PAGE_SIZE, 512), lambda i, s, pt, p=p: (pt[i, s * npg + p], 0, 1))
    o = pl.pallas_call(
        functools.partial(_nsa_step_kernel, past=past),
        out_shape=jax.ShapeDtypeStruct((b, t, D_MODEL), F32),
        grid_spec=pltpu.PrefetchScalarGridSpec(
            num_scalar_prefetch=1,
            grid=(b, n_pages // npg),
            in_specs=[pl.BlockSpec((1, STEP_LANES), lambda i, s, pt: (0, 0)),
                      pl.BlockSpec((None, t, B_E), lambda i, s, pt: (i, 0, 0)),
                      pl.BlockSpec((None, ncb, 512), lambda i, s, pt: (i, 0, 0)),
                      pl.BlockSpec((None, lw, 512), lambda i, s, pt: (i, 0, 0))]
            + [page(p) for p in range(npg)],
            out_specs=pl.BlockSpec((None, t, D_MODEL), lambda i, s, pt: (i, 0, 0)),
            scratch_shapes=[pltpu.VMEM((256, STEP_LANES), MXU),
                            pltpu.VMEM((ncb + 8, STEP_LANES), F32),
                            pltpu.VMEM((STEP_LANES, 256), F32),
                            pltpu.VMEM((1, STEP_LANES), F32), pltpu.VMEM((1, STEP_LANES), F32),
                            pltpu.VMEM((STEP_LANES, 256), F32),
                            pltpu.VMEM((1, STEP_LANES), F32), pltpu.VMEM((1, STEP_LANES), F32),
                            pltpu.VMEM((STEP_LANES, 256), F32)]),
        compiler_params=pltpu.CompilerParams(
            dimension_semantics=("arbitrary", "arbitrary"), vmem_limit_bytes=VMEM_LIMIT),
        name="nsa_attention_step",
    )(page_table, _step_slopes(), proj, cmp, win, *([pool] * npg))
    kv_full_new = proj[..., B_KV_OFF:B_KV_OFF + 1024].reshape(b, t, 4, B_KV_HEADS, HEAD_DIM)
    kv_win_new = proj[..., B_KV_OFF + 1024:B_KV_OFF + 1536].reshape(b, t, 2, B_KV_HEADS, HEAD_DIM)
    new_win = jnp.concatenate([win_cache[:, t:], kv_win_new], axis=1)
    return o, [kv_full_new, new_win]


def alibi_slopes(n):
    return jnp.asarray(2.0 ** (-8.0 * np.arange(1, n + 1) / n), dtype=jnp.float32)


def masked_softmax(s, mask, axis=-1):
    s = jnp.where(mask, s, -jnp.inf)
    m = jnp.max(s, axis=axis, keepdims=True)
    m = jnp.where(jnp.isfinite(m), m, 0.0)
    e = jnp.exp(s - m)
    den = jnp.sum(e, axis=axis, keepdims=True)
    p = e / jnp.where(den > 0, den, 1.0)
    lse = jnp.squeeze(m + jnp.log(den), axis=axis)
    return p, lse


def gather_pages(pool, page_table):
    g = pool[page_table]
    return g.reshape((g.shape[0], g.shape[1] * g.shape[2]) + g.shape[3:])


def dilated_attn_prompt(q, k, v, window, dilation, slopes):
    b, t, h, hd = q.shape
    nk = window // dilation
    ln = t // dilation
    nb = -(-ln // nk)
    lp = nb * nk

    def streams(z):
        z = z.reshape(b, ln, dilation, h, hd).transpose(0, 2, 1, 3, 4)
        z = jnp.pad(z, ((0, 0), (0, 0), (0, lp - ln), (0, 0), (0, 0)))
        return z.reshape(b, dilation, nb, nk, h, hd)

    def with_prev(z):
        prev = jnp.pad(z, ((0, 0), (0, 0), (1, 0), (0, 0), (0, 0), (0, 0)))[:, :, :nb]
        return jnp.concatenate([prev, z], axis=3)

    qs = streams(q)
    kb = with_prev(streams(k))
    vb = with_prev(streams(v))
    s = jnp.einsum('brnqhd,brnkhd->brnhqk', qs, kb).astype(jnp.float32) * (hd ** -0.5)
    step = (nk + jnp.arange(nk))[:, None] - jnp.arange(2 * nk)[None, :]
    first = ((jnp.arange(nb) - 1)[:, None, None] * nk + jnp.arange(2 * nk)[None, None, :]) >= 0
    mask = (step >= 0) & (step <= nk) & first
    s = s - slopes[:, None, None] * (step * dilation).astype(jnp.float32)
    p, lse = masked_softmax(s, mask[None, None, :, None])
    o = jnp.einsum('brnhqk,brnkhd->brnqhd', p.astype(v.dtype), vb)
    o = o.reshape(b, dilation, lp, h, hd)[:, :, :ln].transpose(0, 2, 1, 3, 4).reshape(b, t, h, hd)
    lse = lse.transpose(0, 1, 2, 4, 3).reshape(b, dilation, lp, h)[:, :, :ln]
    lse = lse.transpose(0, 2, 1, 3).reshape(b, t, h)
    return o, lse


def dilated_attn_step(q, k_all, v_all, window, dilation, slopes):
    b, tq, h, hd = q.shape
    lk = k_all.shape[1]
    nk = window // dilation
    steps = jnp.arange(nk + 1)
    idx = (lk - tq + jnp.arange(tq))[:, None] - dilation * steps[None, :]
    valid = idx >= 0
    idx = jnp.maximum(idx, 0)
    kg = k_all[:, idx]
    vg = v_all[:, idx]
    s = jnp.einsum('bqhd,bqjhd->bhqj', q, kg).astype(jnp.float32) * (hd ** -0.5)
    s = s - slopes[:, None, None] * (dilation * steps).astype(jnp.float32)
    p, lse = masked_softmax(s, valid)
    o = jnp.einsum('bhqj,bqjhd->bqhd', p.astype(v_all.dtype), vg)
    return o, lse.transpose(0, 2, 1)


def mixer_a_jax(proj, cache):
    b, t, _ = proj.shape
    n_g = len(A_GROUPS)
    qkv = proj[..., :3 * n_g * A_WIDTH].reshape(b, t, n_g, 3, N_HEADS, HEAD_DIM)
    slopes = alibi_slopes(N_HEADS)
    outs, lses, new_state = [], [], []
    for g, (window, dilation) in enumerate(A_GROUPS):
        q = qkv[:, :, g, 0]
        kv_new = qkv[:, :, g, 1:]
        if cache is None:
            o, lse = dilated_attn_prompt(q, kv_new[:, :, 0], kv_new[:, :, 1], window, dilation, slopes)
            new_state.append(kv_new[:, t - min(window, t):])
        else:
            kv_all = jnp.concatenate([cache[g], kv_new], axis=1)
            o, lse = dilated_attn_step(q, kv_all[:, :, 0], kv_all[:, :, 1], window, dilation, slopes)
            new_state.append(kv_all[:, t:])
        outs.append(o)
        lses.append(lse)
    wts = jax.nn.softmax(jnp.stack(lses), axis=0)
    o = jnp.einsum('gbth,gbthd->bthd', wts, jnp.stack(outs).astype(jnp.float32))
    return o.reshape(b, t, A_WIDTH), new_state


def nsa_attention(q, kv_full, kv_win, gates, pos_emb, w_cmp):
    b, tq, hq, hd = q.shape
    l = kv_full.shape[1]
    lw = kv_win.shape[1]
    g = B_KV_HEADS
    r = hq // g
    scale = hd ** -0.5
    dt = q.dtype
    slopes = alibi_slopes(hq).reshape(g, r)
    nblk = -(-l // B_BLOCK)
    lp = nblk * B_BLOCK
    blocks = jnp.pad(kv_full, ((0, 0), (0, lp - l), (0, 0), (0, 0), (0, 0))).reshape(b, nblk, B_BLOCK, 4, g, hd)
    pooled = jnp.mean(blocks[:, :, :, :2] + pos_emb.transpose(1, 0, 2, 3), axis=2)
    cmp = jnp.einsum('bncgd,cgde->bncge', pooled, w_cmp)
    k_cmp, v_cmp = cmp[:, :, 0], cmp[:, :, 1]
    sel_blocks = blocks[:, :, :, 2:].transpose(0, 4, 1, 2, 3, 5)
    kw_p = jnp.pad(kv_win, ((0, 0), (B_WINDOW, 0), (0, 0), (0, 0), (0, 0)))
    blk_end = (jnp.arange(nblk) + 1) * B_BLOCK - 1
    bidx = jnp.arange(nblk)
    n_sel = min(B_TOPK, nblk)
    qb_size = B_QBLOCK if tq % B_QBLOCK == 0 else tq
    nq = tq // qb_size

    def one_block(args):
        qb, gb, j = args
        pos = l - tq + j * qb_size + jnp.arange(qb_size)
        qg = qb.reshape(b, qb_size, g, r, hd)
        s_c = jnp.einsum('bqgrd,bngd->bgrqn', qg, k_cmp).astype(jnp.float32) * scale
        s_c = s_c - slopes[:, :, None, None] * (pos[:, None] - blk_end[None, :]).astype(jnp.float32)
        p_c, _ = masked_softmax(s_c, blk_end[None, :] <= pos[:, None])
        o_c = jnp.einsum('bgrqn,bngd->bqgrd', p_c.astype(dt), v_cmp)
        cur = pos // B_BLOCK
        forced = (bidx[None, :] == 0) | (bidx[None, :] >= cur[:, None] - 1)
        score = jnp.where(bidx[None, :] > cur[:, None], -jnp.inf,
                          jnp.where(forced, jnp.inf, jnp.sum(p_c, axis=2)))
        _, sel = lax.top_k(score, n_sel)
        kv_sel = jax.vmap(jax.vmap(lambda blk, ix: blk[ix]))(sel_blocks, sel)
        kpos = sel[..., None] * B_BLOCK + jnp.arange(B_BLOCK)
        dist_s = pos[None, None, :, None, None] - kpos
        s_s = jnp.einsum('bqgrd,bgqnkd->bgrqnk', qg, kv_sel[..., 0, :]).astype(jnp.float32) * scale
        s_s = s_s - slopes[None, :, :, None, None, None] * dist_s[:, :, None].astype(jnp.float32)
        p_s, _ = masked_softmax(s_s, (dist_s >= 0)[:, :, None], axis=(-2, -1))
        o_s = jnp.einsum('bgrqnk,bgqnkd->bqgrd', p_s.astype(dt), kv_sel[..., 1, :])
        start = lw - tq + j * qb_size
        kvw = lax.dynamic_slice_in_dim(kw_p, start, B_WINDOW + qb_size, axis=1)
        kpos_w = l - tq + j * qb_size - B_WINDOW + jnp.arange(B_WINDOW + qb_size)
        dist_w = pos[:, None] - kpos_w[None, :]
        mask_w = (dist_w >= 0) & (dist_w <= B_WINDOW) & (kpos_w[None, :] >= l - lw)
        s_w = jnp.einsum('bqgrd,bkgd->bgrqk', qg, kvw[:, :, 0]).astype(jnp.float32) * scale
        s_w = s_w - slopes[:, :, None, None] * dist_w.astype(jnp.float32)
        p_w, _ = masked_softmax(s_w, mask_w)
        o_w = jnp.einsum('bgrqk,bkgd->bqgrd', p_w.astype(dt), kvw[:, :, 1])
        gb = gb.reshape(b, qb_size, g, r, 3)
        o = gb[..., 0:1] * o_c + gb[..., 1:2] * o_s + gb[..., 2:3] * o_w
        return o.reshape(b, qb_size, hq, hd)

    qs = q.reshape(b, nq, qb_size, hq, hd).transpose(1, 0, 2, 3, 4)
    gs = gates.reshape(b, nq, qb_size, hq, 3).transpose(1, 0, 2, 3, 4)
    out = lax.map(one_block, (qs, gs, jnp.arange(nq)))
    return out.transpose(1, 0, 2, 3, 4).reshape(b, tq, hq, hd)


def mixer_b_jax(proj, pos_emb, w_cmp, cache):
    b, t, _ = proj.shape
    q = proj[..., :D_MODEL].reshape(b, t, N_HEADS, HEAD_DIM)
    o1 = D_MODEL + 6 * B_KV_WIDTH
    kv = lax.optimization_barrier(proj[..., D_MODEL:o1].reshape(b, t, 6, B_KV_HEADS, HEAD_DIM))
    branch_gates = jax.nn.sigmoid(proj[..., o1:o1 + 3 * N_HEADS].reshape(b, t, N_HEADS, 3))
    kv_full_new, kv_win_new = kv[:, :, :4], kv[:, :, 4:]
    if cache is None:
        kv_full, kv_win = kv_full_new, kv_win_new
        new_win = kv_win_new[:, t - min(B_WINDOW, t):]
    else:
        kv_full = jnp.concatenate([cache[0], kv_full_new], axis=1)
        kv_win = jnp.concatenate([cache[1], kv_win_new], axis=1)
        new_win = kv_win[:, t:]
    o = nsa_attention(q, kv_full, kv_win, branch_gates, pos_emb, w_cmp)
    return o.reshape(b, t, D_MODEL), [kv_full_new, new_win]


def fox_attention(q, k, v, cum):
    b, tq, h, hd = q.shape
    l = k.shape[1]
    scale = hd ** -0.5
    qb_size = C_QBLOCK if tq % C_QBLOCK == 0 else tq
    nq = tq // qb_size
    cum_k = cum.transpose(0, 2, 1)
    cum_q = cum[:, l - tq:].reshape(b, nq, qb_size, h).transpose(1, 0, 3, 2)
    kpos = jnp.arange(l)

    def one_block(args):
        qb, cq, j = args
        pos = l - tq + j * qb_size + jnp.arange(qb_size)
        s = jnp.einsum('bqhd,bkhd->bhqk', qb, k).astype(jnp.float32) * scale
        s = s + cq[..., None] - cum_k[:, :, None, :]
        p, _ = masked_softmax(s, kpos[None, :] <= pos[:, None])
        return jnp.einsum('bhqk,bkhd->bqhd', p.astype(v.dtype), v)

    qs = q.reshape(b, nq, qb_size, h, hd).transpose(1, 0, 2, 3, 4)
    out = lax.map(one_block, (qs, cum_q, jnp.arange(nq)))
    return out.transpose(1, 0, 2, 3, 4).reshape(b, tq, h, hd)


def mixer_c_jax(proj, b_forget, cache):
    b, t, _ = proj.shape
    qkv = proj[..., :3 * D_MODEL].reshape(b, t, 3, N_HEADS, HEAD_DIM)
    logf = jax.nn.log_sigmoid(proj[..., 3 * D_MODEL:3 * D_MODEL + N_HEADS].astype(jnp.float32)
                              + b_forget.astype(jnp.float32))
    kv_new = qkv[:, :, 1:]
    if cache is None:
        kv_all, logf_all = kv_new, logf
    else:
        kv_all = jnp.concatenate([cache[0], kv_new], axis=1)
        logf_all = jnp.concatenate([cache[1].astype(jnp.float32), logf], axis=1)
    cum = jnp.cumsum(logf_all, axis=1)
    o = fox_attention(qkv[:, :, 0], kv_all[:, :, 0], kv_all[:, :, 1], cum)
    return o.reshape(b, t, D_MODEL), [kv_new, logf]


def _pad_cols(w, e_pad):
    return jnp.pad(w, ((0, 0), (0, e_pad - w.shape[1])))


def run_trunk(x, mods, layer_cache, weights, *, bt, tt):
    (norm_pre, norm_post, a_w_in, a_w_out, b_w_in, b_pos_emb, b_w_cmp, b_w_out,
     c_w_in, c_b_forget, c_w_out) = weights
    states = []
    for i in range(DEPTH):
        kind, j = i % N_MIXERS, i // N_MIXERS
        mod = mods[i][:, None, :]
        cache = layer_cache(i)
        if kind == 0:
            w_in = a_w_in[j].astype(BF16)
            proj = in_projection(x, mod, norm_pre[i], w_in, bt=bt, tt=tt, tn=1024)
            o, st = mixer_a_prompt(proj) if cache is None else mixer_a_step(proj, cache)
            gate_col, w_out = A_E // 1024 - 1, a_w_out[j]
        elif kind == 1:
            w_in = _pad_cols(b_w_in[j][:, _b_in_perm()], B_E).astype(BF16)
            proj = in_projection(x, mod, norm_pre[i], w_in, bt=bt, tt=tt, tn=1024)
            if cache is None:
                o, st = mixer_b_prompt(proj, b_pos_emb[j], b_w_cmp[j])
            else:
                o, st = mixer_b_step(proj, b_pos_emb[j], b_w_cmp[j], *cache)
            gate_col, w_out = 1, b_w_out[j][_b_out_perm(), :]
        else:
            w = c_w_in[j]
            w_in = jnp.concatenate([w[:, :3072], w[:, 3088:4112], _pad_cols(w[:, 3072:3088], 128)],
                                   axis=1).astype(BF16)
            proj = in_projection(x, mod, norm_pre[i], w_in, bt=bt, tt=tt, tn=C_E // 3)
            if cache is None:
                o, st = mixer_c_prompt(proj, c_b_forget[j])
            else:
                o, st = mixer_c_step(proj, c_b_forget[j], *cache)
            gate_col, w_out = 3, c_w_out[j]
        x = out_projection(o, proj, gate_col, x, mod, norm_post[i], w_out.astype(BF16), bt=bt, tt=tt)
        states.append(st)
    return x, states


def stack_state(states, kind, k):
    return jnp.stack([states[i][k] for i in range(DEPTH) if i % N_MIXERS == kind])


def kernel(x_prompt, x_sample, c_prompt, c_sample, cache_a_w128, cache_a_w512, cache_a_w2048,
           cache_b_kv, cache_b_win, cache_c_kv, cache_c_logf, page_table,
           ada_w, ada_b, norm_pre, norm_post, a_w_in, a_w_out, b_w_in, b_pos_emb, b_w_cmp, b_w_out,
           c_w_in, c_b_forget, c_w_out):
    weights = (norm_pre, norm_post, a_w_in, a_w_out, b_w_in, b_pos_emb, b_w_cmp, b_w_out,
               c_w_in, c_b_forget, c_w_out)
    nbp, nbs = x_prompt.shape[0], x_sample.shape[0]
    nb_pad = -(-(nbp + nbs) // 8) * 8
    c_all = jnp.concatenate([c_prompt, c_sample,
                             jnp.zeros((nb_pad - nbp - nbs, D_MODEL), F32)], axis=0)
    mods = ada_modulation(c_all, ada_w, ada_b)
    mods_p = mods[:, :nbp]
    mods_s = mods[:, nbp:nbp + nbs]

    def prompt_cache(i):
        return None

    def sample_cache(i):
        kind, j = i % N_MIXERS, i // N_MIXERS
        if kind == 0:
            return [cache_a_w128[j], cache_a_w512[j], cache_a_w2048[j]]
        if kind == 1:
            return [cache_b_kv[j], cache_b_win[j], page_table]
        return [cache_c_kv[j], cache_c_logf[j], page_table]

    y_prompt, st_p = run_trunk(x_prompt, mods_p, prompt_cache, weights, bt=1, tt=1024)
    y_sample, st_s = run_trunk(x_sample, mods_s, sample_cache, weights, bt=nbs, tt=x_sample.shape[1])

    outs = [y_prompt, y_sample]
    for st in (st_p, st_s):
        outs += [stack_state(st, 0, 0), stack_state(st, 0, 1), stack_state(st, 0, 2),
                 stack_state(st, 1, 0), stack_state(st, 1, 1),
                 stack_state(st, 2, 0), stack_state(st, 2, 1)]
    return tuple(outs)
```

```python
import functools

import jax
import jax.numpy as jnp
import numpy as np
from jax import lax
from jax.experimental import pallas as pl
from jax.experimental.pallas import tpu as pltpu

D_MODEL = 1024
DEPTH = 4
N_MIXERS = 3
HEAD_DIM = 64
RMS_EPS = 1e-6
N_HEADS = D_MODEL // HEAD_DIM
A_GROUPS = ((128, 1), (512, 4), (2048, 16))
A_WIDTH = D_MODEL
B_KV_HEADS = N_HEADS // 4
B_KV_WIDTH = B_KV_HEADS * HEAD_DIM
B_BLOCK = 64
B_TOPK = 16
B_WINDOW = 512
B_QBLOCK = 64
C_QBLOCK = 128
PAGE_SIZE = 128

VMEM_LIMIT = 56 * 1024 * 1024

F32 = jnp.float32
BF16 = jnp.bfloat16


def _sigmoid(x):
    return 1.0 / (1.0 + jnp.exp(-x))


def _ada_kernel(c_ref, w_ref, b_ref, o_ref):
    c = c_ref[...]
    s = c * _sigmoid(c)
    o_ref[...] = jnp.dot(s, w_ref[...], preferred_element_type=F32,
                         precision=lax.Precision.HIGHEST) + b_ref[...]


def ada_modulation(c_all, ada_w, ada_b):
    nb = c_all.shape[0]
    tn = 1024
    return pl.pallas_call(
        _ada_kernel,
        out_shape=jax.ShapeDtypeStruct((DEPTH, nb, 3 * D_MODEL), F32),
        grid=(DEPTH, 3 * D_MODEL // tn),
        in_specs=[
            pl.BlockSpec((nb, D_MODEL), lambda i, n: (0, 0)),
            pl.BlockSpec((None, D_MODEL, tn), lambda i, n: (i, 0, n)),
            pl.BlockSpec((None, 1, tn), lambda i, n: (i, 0, n)),
        ],
        out_specs=pl.BlockSpec((None, nb, tn), lambda i, n: (i, 0, n)),
        compiler_params=pltpu.CompilerParams(vmem_limit_bytes=VMEM_LIMIT),
        name="ada_modulation",
    )(c_all, ada_w, ada_b.reshape(DEPTH, 1, 3 * D_MODEL))


def _inproj_kernel(x_ref, shift_ref, scale_ref, g_ref, w_ref, o_ref, h_ref):
    bt, tt, d = x_ref.shape

    @pl.when(pl.program_id(1) == 0)
    def _():
        x = x_ref[...]
        ms = jnp.mean(x * x, axis=-1, keepdims=True)
        y = x * lax.rsqrt(ms + RMS_EPS) * g_ref[...]
        h = y * (1.0 + scale_ref[...]) + shift_ref[...]
        h_ref[...] = h.reshape(bt * tt, d).astype(BF16)

    o_ref[...] = jnp.dot(h_ref[...], w_ref[...],
                         preferred_element_type=F32).reshape(o_ref.shape)


def in_projection(x, mod, g, w_bf16, *, bt, tt, tn):
    b, t, d = x.shape
    e = w_bf16.shape[1]
    assert b % bt == 0 and t % tt == 0 and e % tn == 0
    nt = t // tt
    return pl.pallas_call(
        _inproj_kernel,
        out_shape=jax.ShapeDtypeStruct((b, t, e), F32),
        grid=(b // bt * nt, e // tn),
        in_specs=[
            pl.BlockSpec((bt, tt, d), lambda m, n: (m // nt, m % nt, 0)),
            pl.BlockSpec((bt, 1, d), lambda m, n: (m // nt, 0, 0)),
            pl.BlockSpec((bt, 1, d), lambda m, n: (m // nt, 0, 1)),
            pl.BlockSpec((1, d), lambda m, n: (0, 0)),
            pl.BlockSpec((d, tn), lambda m, n: (0, n)),
        ],
        out_specs=pl.BlockSpec((bt, tt, tn), lambda m, n: (m // nt, m % nt, n)),
        scratch_shapes=[pltpu.VMEM((bt * tt, d), BF16)],
        compiler_params=pltpu.CompilerParams(
            dimension_semantics=("arbitrary", "arbitrary"),
            vmem_limit_bytes=VMEM_LIMIT),
        name="in_projection",
    )(x, mod, mod, g.reshape(1, d), w_bf16)


def _outproj_kernel(o_ref, gate_ref, x_ref, mg_ref, g_ref, w_ref, out_ref):
    bt, tt, d = x_ref.shape
    gt = gate_ref[...]
    og = (o_ref[...] * (gt * _sigmoid(gt))).reshape(bt * tt, d).astype(BF16)
    y = jnp.dot(og, w_ref[...], preferred_element_type=F32)
    ms = jnp.mean(y * y, axis=-1, keepdims=True)
    yn = (y * lax.rsqrt(ms + RMS_EPS) * g_ref[...]).reshape(bt, tt, d)
    out_ref[...] = x_ref[...] + mg_ref[...] * yn


def out_projection(o, proj, gate_col, x, mod, g, w_bf16, *, bt, tt):
    b, t, d = x.shape
    nt = t // tt
    return pl.pallas_call(
        _outproj_kernel,
        out_shape=jax.ShapeDtypeStruct((b, t, d), F32),
        grid=(b // bt * nt,),
        in_specs=[
            pl.BlockSpec((bt, tt, d), lambda m: (m // nt, m % nt, 0)),
            pl.BlockSpec((bt, tt, d), lambda m: (m // nt, m % nt, gate_col)),
            pl.BlockSpec((bt, tt, d), lambda m: (m // nt, m % nt, 0)),
            pl.BlockSpec((bt, 1, d), lambda m: (m // nt, 0, 2)),
            pl.BlockSpec((1, d), lambda m: (0, 0)),
            pl.BlockSpec((d, d), lambda m: (0, 0)),
        ],
        out_specs=pl.BlockSpec((bt, tt, d), lambda m: (m // nt, m % nt, 0)),
        compiler_params=pltpu.CompilerParams(
            dimension_semantics=("arbitrary",),
            vmem_limit_bytes=VMEM_LIMIT),
        name="out_projection",
    )(o, proj, x, mod, g.reshape(1, d), w_bf16)


MXU = BF16
NEG = -1e30
HIGHEST = lax.Precision.HIGHEST


def _slope(h):
    return float(np.float32(2.0 ** (-8.0 * (h + 1) / N_HEADS)))


def _dot_nt(a, b):
    return lax.dot_general(a, b, (((1,), (1,)), ((), ())), preferred_element_type=F32)


def _dot_tn(a, b):
    return lax.dot_general(a, b, (((0,), (0,)), ((), ())), preferred_element_type=F32)


def _rowmax(s):
    m = s[:, 0:128]
    for c in range(1, s.shape[1] // 128):
        m = jnp.maximum(m, s[:, c * 128:(c + 1) * 128])
    return jnp.max(m, axis=1, keepdims=True)


def _flash_rows(q4, slope4, pos4, k_b, v_b, j_lo, j_hi, mask_fn, m_ref, l_ref, acc_ref, tk=128):
    m_ref[...] = jnp.full(m_ref.shape, NEG, F32)
    l_ref[...] = jnp.zeros(l_ref.shape, F32)
    acc_ref[...] = jnp.zeros(acc_ref.shape, F32)

    def body(j, c):
        off = pl.multiple_of(j * tk, tk)
        kt = k_b[pl.ds(off, tk), :]
        vt = v_b[pl.ds(off, tk), :]
        s = _dot_nt(q4, kt)
        kpos = off + lax.broadcasted_iota(jnp.int32, (1, tk), 1)
        dist = pos4 - kpos
        s = s - slope4 * dist.astype(F32)
        valid = mask_fn(dist, off)
        sm = jnp.where(valid, s, NEG)
        m_old = m_ref[...]
        m_new = jnp.maximum(m_old, jnp.max(sm, axis=1, keepdims=True))
        p = jnp.where(valid, jnp.exp(sm - m_new), 0.0)
        alpha = jnp.exp(m_old - m_new)
        l_ref[...] = alpha * l_ref[...] + jnp.sum(p, axis=1, keepdims=True)
        acc_ref[...] = alpha * acc_ref[...] + jnp.dot(p.astype(MXU), vt, preferred_element_type=F32)
        m_ref[...] = m_new
        return c

    lax.fori_loop(j_lo, j_hi, body, 0)
    return acc_ref[...] / l_ref[...]


B_E = 4096
B_KV_OFF = 2048
B_BG_OFF = 3584


def _b_in_perm():
    perm = np.zeros(3632, np.int64)
    for r in range(4):
        for g in range(4):
            h = 4 * g + r
            for d in range(64):
                perm[r * 256 + g * 64 + d] = h * 64 + d
                perm[1024 + r * 256 + g * 64 + d] = 2608 + h * 64 + d
    perm[2048:3584] = 1024 + np.arange(1536)
    for br in range(3):
        for r in range(4):
            for g in range(4):
                perm[3584 + br * 16 + r * 4 + g] = 2560 + (4 * g + r) * 3 + br
    return perm


def _b_out_perm():
    perm = np.zeros(1024, np.int64)
    for r in range(4):
        for g in range(4):
            for d in range(64):
                perm[r * 256 + g * 64 + d] = (4 * g + r) * 64 + d
    return perm


def _nsa_cmp_kernel(x_ref, pe_ref, w_ref, o_ref):
    tt = x_ref.shape[0]
    nb = tt // B_BLOCK
    x = x_ref[...].reshape(nb, B_BLOCK, 512) + pe_ref[...][None]
    pooled = jnp.sum(x, axis=1) * (1.0 / B_BLOCK)
    o_ref[...] = jnp.dot(pooled, w_ref[...], preferred_element_type=F32, precision=HIGHEST)


def _nsa_cmp_weights(pos_emb, w_cmp):
    pe = pos_emb.transpose(1, 0, 2, 3).reshape(B_BLOCK, 512)
    wbd = jnp.zeros((512, 512), F32)
    for c in range(2):
        for g in range(4):
            o = c * 256 + g * 64
            wbd = wbd.at[o:o + 64, o:o + 64].set(w_cmp[c, g])
    return pe, wbd


def nsa_compress_prompt(proj, pe, wbd, *, tt=512):
    b, t, _ = proj.shape
    return pl.pallas_call(
        _nsa_cmp_kernel,
        out_shape=jax.ShapeDtypeStruct((b, t // B_BLOCK, 512), F32),
        grid=(b, t // tt),
        in_specs=[
            pl.BlockSpec((None, tt, 512), lambda i, j: (i, j, B_KV_OFF // 512)),
            pl.BlockSpec((B_BLOCK, 512), lambda i, j: (0, 0)),
            pl.BlockSpec((512, 512), lambda i, j: (0, 0)),
        ],
        out_specs=pl.BlockSpec((None, tt // B_BLOCK, 512), lambda i, j: (i, j, 0)),
        compiler_params=pltpu.CompilerParams(vmem_limit_bytes=VMEM_LIMIT),
        name="nsa_compress_prompt",
    )(proj, pe, wbd)


NSA_TK = 512


def _nsa_prompt_kernel(q_ref, ks_ref, vs_ref, kw_ref, vw_ref, cmp_ref, bg_ref, ex_ref, o_ref,
                       ksb, kwb, vsg, vwg, q4_ref, sel_ref, wmask_ref, oc_ref, m_ref, acc_ref, *, tq, t):
    qi = pl.program_id(1)
    nblk = t // B_BLOCK
    tk = NSA_TK
    q0 = qi * tq
    lane = lax.broadcasted_iota(jnp.int32, (1, 256), 1)
    gmasks = [(lane >= 64 * g) & (lane < 64 * (g + 1)) for g in range(4)]

    @pl.when(qi == 0)
    def _():
        ksb[...] = ks_ref[...].astype(MXU)
        kwb[...] = kw_ref[...].astype(MXU)
        ones = jnp.ones((t, 64), F32)
        for g in range(4):
            vsg[g] = jnp.concatenate([vs_ref[:, 64 * g:64 * (g + 1)], ones], axis=1).astype(MXU)
            vwg[g] = jnp.concatenate([vw_ref[:, 64 * g:64 * (g + 1)], ones], axis=1).astype(MXU)

    posq = q0 + lax.broadcasted_iota(jnp.int32, (1, tq), 1)
    posc = q0 + lax.broadcasted_iota(jnp.int32, (tq, 1), 0)
    bidx = lax.broadcasted_iota(jnp.int32, (nblk, 1), 0)
    blk_end = (bidx + 1) * B_BLOCK - 1
    cur = posq // B_BLOCK
    forced = (bidx == 0) | (bidx >= cur - 1)
    kc = cmp_ref[:, 0:256].astype(MXU)
    vc = cmp_ref[:, 256:512].astype(MXU)
    distc = (posq - blk_end).astype(F32)
    validc = blk_end <= posq

    for g in range(4):
        q4 = jnp.concatenate(
            [jnp.where(gmasks[g], q_ref[:, r * 256:(r + 1) * 256], 0.0) for r in range(4)], axis=0)
        q4_ref[g] = (q4 * (HEAD_DIM ** -0.5)).astype(MXU)

        s_ct = _dot_nt(kc, q4_ref[g])
        p_parts = []
        for r in range(4):
            s_r = s_ct[:, r * tq:(r + 1) * tq] - _slope(4 * g + r) * distc
            sm = jnp.where(validc, s_r, -jnp.inf)
            mx = jnp.max(sm, axis=0, keepdims=True)
            mx = jnp.where(mx > -jnp.inf, mx, 0.0)
            e = jnp.exp(sm - mx)
            den = jnp.sum(e, axis=0, keepdims=True)
            p_parts.append(e / jnp.where(den > 0, den, 1.0))
        oc_ref[g] = _dot_tn(jnp.concatenate(p_parts, axis=1).astype(MXU), vc)

        sc = p_parts[0] + p_parts[1] + p_parts[2] + p_parts[3]
        score = jnp.where(bidx > cur, -jnp.inf, jnp.where(forced, jnp.inf, sc))
        rank = jnp.zeros((nblk, tq), F32)
        for n in range(nblk):
            row = score[n:n + 1, :]
            beats = (row > score) | ((row == score) & (bidx > n))
            rank = rank + jnp.where(beats, 1.0, 0.0)
        sel_ref[g] = jnp.where(rank < float(min(B_TOPK, nblk)), 1.0, 0.0).T.astype(MXU)

    def tile(off, k_b, vg_ref, mask_fn):
        kt = k_b[pl.ds(off, tk), :]
        kq = (off - q0 + lax.broadcasted_iota(jnp.int32, (1, tk), 1)).astype(F32)
        for g in range(4):
            s = _dot_nt(q4_ref[g], kt)
            madd = mask_fn(g)
            s = jnp.concatenate(
                [s[r * tq:(r + 1) * tq] + _slope(4 * g + r) * kq + madd for r in range(4)], axis=0)
            m_old = m_ref[g]
            m_new = jnp.maximum(m_old, _rowmax(s))
            p = jnp.exp(s - m_new)
            acc_ref[g] = jnp.exp(m_old - m_new) * acc_ref[g] + jnp.dot(
                p.astype(MXU), vg_ref[g, pl.ds(off, tk), :], preferred_element_type=F32)
            m_ref[g] = m_new

    def branch(k_b, vg_ref, j_lo, j_hi, mask_fn, last_mask_fn):
        m_ref[...] = jnp.full(m_ref.shape, NEG, F32)
        acc_ref[...] = jnp.zeros(acc_ref.shape, F32)

        def body(j, c):
            off = pl.multiple_of(j * tk, tk)
            tile(off, k_b, vg_ref, lambda g: mask_fn(g, off, j))
            return c

        lax.fori_loop(j_lo, j_hi, body, 0)
        off = pl.multiple_of(j_hi * tk, tk)
        tile(off, k_b, vg_ref, lambda g: last_mask_fn(g, off, j_hi))
        return [acc_ref[g][:, 0:64] / acc_ref[g][:, 64:65] for g in range(4)]

    j_hi = (q0 + tq - 1) // tk
    kcol = lax.broadcasted_iota(jnp.int32, (1, tk), 1)

    def sel_madd(g, off, j):
        sel_keys = jnp.dot(sel_ref[g], ex_ref[:, pl.ds(off, tk)], preferred_element_type=F32)
        return (sel_keys - 1.0) * (-NEG)

    def sel_madd_last(g, off, j):
        return sel_madd(g, off, j) + jnp.where(off + kcol <= posc, 0.0, NEG)

    o_s = branch(ksb, vsg, 0, j_hi, sel_madd, sel_madd_last)

    j_lo = jnp.maximum(q0 - B_WINDOW, 0) // tk
    for c in range(wmask_ref.shape[1] // tk):
        dist = posc - ((j_lo + c) * tk + kcol)
        wmask_ref[:, c * tk:(c + 1) * tk] = jnp.where((dist >= 0) & (dist <= B_WINDOW), 0.0, NEG)

    def win_madd(g, off, j):
        return wmask_ref[:, pl.ds(pl.multiple_of((j - j_lo) * tk, tk), tk)]

    o_w = branch(kwb, vwg, j_lo, j_hi, win_madd, win_madd)

    bgate = _sigmoid(bg_ref[...])
    for r in range(4):
        rs = slice(r * tq, (r + 1) * tq)
        parts = []
        for g in range(4):
            c0 = r * 4 + g
            parts.append(bgate[:, c0:c0 + 1] * oc_ref[g][rs, 64 * g:64 * (g + 1)]
                         + bgate[:, 16 + c0:17 + c0] * o_s[g][rs] + bgate[:, 32 + c0:33 + c0] * o_w[g][rs])
        o_ref[:, r * 256:(r + 1) * 256] = jnp.concatenate(parts, axis=1)


def _nsa_expand(t):
    nblk = t // B_BLOCK
    ex = (np.arange(t)[None, :] // B_BLOCK == np.arange(nblk)[:, None])
    return jnp.asarray(ex, MXU)


def nsa_attention_prompt(proj, cmp, *, tq=128):
    b, t, _ = proj.shape
    assert tq == 128 and t % tq == 0
    kv = lambda c: pl.BlockSpec((None, t, 256), lambda i, j, c=c: (i, 0, B_KV_OFF // 256 + c),
                                pipeline_mode=pl.Buffered(1))
    return pl.pallas_call(
        functools.partial(_nsa_prompt_kernel, tq=tq, t=t),
        out_shape=jax.ShapeDtypeStruct((b, t, D_MODEL), F32),
        grid=(b, t // tq),
        in_specs=[
            pl.BlockSpec((None, tq, 1024), lambda i, j: (i, j, 0)),
            kv(2), kv(3), kv(4), kv(5),
            pl.BlockSpec((None, t // B_BLOCK, 512), lambda i, j: (i, 0, 0)),
            pl.BlockSpec((None, tq, 128), lambda i, j: (i, j, B_BG_OFF // 128)),
            pl.BlockSpec((t // B_BLOCK, t), lambda i, j: (0, 0)),
        ],
        out_specs=pl.BlockSpec((None, tq, D_MODEL), lambda i, j: (i, j, 0)),
        scratch_shapes=[
            pltpu.VMEM((t, 256), MXU), pltpu.VMEM((t, 256), MXU),
            pltpu.VMEM((4, t, 128), MXU), pltpu.VMEM((4, t, 128), MXU),
            pltpu.VMEM((4, 4 * tq, 256), MXU),
            pltpu.VMEM((4, tq, t // B_BLOCK), MXU),
            pltpu.VMEM((tq, -(-(B_WINDOW + tq + NSA_TK - 128) // NSA_TK) * NSA_TK), F32),
            pltpu.VMEM((4, 4 * tq, 256), F32),
            pltpu.VMEM((4, 4 * tq, 1), F32),
            pltpu.VMEM((4, 4 * tq, 128), F32)],
        compiler_params=pltpu.CompilerParams(
            dimension_semantics=("arbitrary", "arbitrary"),
            vmem_limit_bytes=VMEM_LIMIT),
        name="nsa_attention_prompt",
    )(proj, proj, proj, proj, proj, cmp, proj, _nsa_expand(t))


def mixer_b_prompt(proj, pos_emb, w_cmp):
    b, t, _ = proj.shape
    pe, wbd = _nsa_cmp_weights(pos_emb, w_cmp)
    cmp = nsa_compress_prompt(proj, pe, wbd)
    o = nsa_attention_prompt(proj, cmp)
    kv_full_new = proj[..., B_KV_OFF:B_KV_OFF + 1024].reshape(b, t, 4, B_KV_HEADS, HEAD_DIM)
    new_win = proj[:, t - min(B_WINDOW, t):, B_KV_OFF + 1024:B_KV_OFF + 1536].reshape(
        b, min(B_WINDOW, t), 2, B_KV_HEADS, HEAD_DIM)
    return o, [kv_full_new, new_win]


C_E = 4224
C_F_OFF = 4096


def _log_sigmoid(x):
    return jnp.minimum(x, 0.0) - jnp.log(1.0 + jnp.exp(-jnp.abs(x)))


def _fox_prep_kernel(x_ref, bf_ref, logft_ref, cumt_ref, carry_ref):
    @pl.when(pl.program_id(1) == 0)
    def _():
        carry_ref[...] = jnp.zeros(carry_ref.shape, F32)

    n = x_ref.shape[0]
    logf = _log_sigmoid(x_ref[...] + bf_ref[...])
    tri = jnp.where(lax.broadcasted_iota(jnp.int32, (n, n), 1) <= lax.broadcasted_iota(jnp.int32, (n, n), 0),
                    1.0, 0.0)
    cum = jnp.dot(tri, logf, preferred_element_type=F32, precision=HIGHEST) + carry_ref[...]
    logft_ref[...] = logf.T[0:N_HEADS, :]
    cumt_ref[...] = cum.T[0:N_HEADS, :]
    carry_ref[...] = cum[n - 1:n, :]


def fox_prep(proj, b_forget, *, tt=128):
    b, t, _ = proj.shape
    bf = jnp.pad(b_forget.astype(F32), (0, 128 - N_HEADS)).reshape(1, 128)
    return pl.pallas_call(
        _fox_prep_kernel,
        out_shape=[jax.ShapeDtypeStruct((b, N_HEADS, t), F32), jax.ShapeDtypeStruct((b, N_HEADS, t), F32)],
        grid=(b, t // tt),
        in_specs=[pl.BlockSpec((None, tt, 128), lambda i, j: (i, j, C_F_OFF // 128)),
                  pl.BlockSpec((1, 128), lambda i, j: (0, 0))],
        out_specs=[pl.BlockSpec((None, N_HEADS, tt), lambda i, j: (i, 0, j)),
                   pl.BlockSpec((None, N_HEADS, tt), lambda i, j: (i, 0, j))],
        scratch_shapes=[pltpu.VMEM((1, 128), F32)],
        compiler_params=pltpu.CompilerParams(
            dimension_semantics=("arbitrary", "arbitrary"), vmem_limit_bytes=VMEM_LIMIT),
        name="fox_prep",
    )(proj, bf)


def _fox_prompt_kernel(q_ref, k_ref, v_ref, ck_ref, o_ref, kb, vb, qm_ref, m_ref, acc_ref, *, tq):
    hp = pl.program_id(1)
    qi = pl.program_id(2)

    lane = lax.broadcasted_iota(jnp.int32, (1, 128), 1)
    hmasks = [(lane >= 64 * hh) & (lane < 64 * (hh + 1)) for hh in range(2)]

    @pl.when(qi == 0)
    def _():
        kb[...] = k_ref[...].astype(MXU)
        for hh in range(2):
            vb[hh] = jnp.where(hmasks[hh], v_ref[...], 1.0).astype(MXU)

    for hh in range(2):
        qm_ref[hh] = (jnp.where(hmasks[hh], q_ref[...], 0.0) * (HEAD_DIM ** -0.5)).astype(MXU)
    m_ref[...] = jnp.full(m_ref.shape, NEG, F32)
    acc_ref[...] = jnp.zeros(acc_ref.shape, F32)

    def tile(j, diag):
        off = pl.multiple_of(j * tq, tq)
        kt = kb[pl.ds(off, tq), :]
        for hh in range(2):
            ck = ck_ref[pl.ds(2 * hp + hh, 1), pl.ds(off, tq)]
            s = _dot_nt(qm_ref[hh], kt) - ck
            if diag:
                causal = (lax.broadcasted_iota(jnp.int32, (1, tq), 1)
                          <= lax.broadcasted_iota(jnp.int32, (tq, 1), 0))
                s = jnp.where(causal, s, NEG)
            m_old = m_ref[hh]
            m_new = jnp.maximum(m_old, _rowmax(s))
            p = jnp.exp(s - m_new)
            alpha = jnp.exp(m_old - m_new)
            acc_ref[hh] = alpha * acc_ref[hh] + jnp.dot(p.astype(MXU), vb[hh, pl.ds(off, tq), :],
                                                        preferred_element_type=F32)
            m_ref[hh] = m_new

    def body(j, c):
        tile(j, False)
        return c

    lax.fori_loop(0, qi, body, 0)
    tile(qi, True)
    a0, a1 = acc_ref[0], acc_ref[1]
    o_ref[...] = jnp.where(lane < 64, a0 / a0[:, 64:65], a1 / a1[:, 0:1])


def fox_attention_prompt(proj, cumt, *, tq=512):
    b, t, _ = proj.shape
    tq = min(tq, t)
    return pl.pallas_call(
        functools.partial(_fox_prompt_kernel, tq=tq),
        out_shape=jax.ShapeDtypeStruct((b, t, D_MODEL), F32),
        grid=(b, N_HEADS // 2, t // tq),
        in_specs=[
            pl.BlockSpec((None, tq, 128), lambda i, p, j: (i, j, p)),
            pl.BlockSpec((None, t, 128), lambda i, p, j: (i, 0, 8 + p)),
            pl.BlockSpec((None, t, 128), lambda i, p, j: (i, 0, 16 + p)),
            pl.BlockSpec((None, N_HEADS, t), lambda i, p, j: (i, 0, 0)),
        ],
        out_specs=pl.BlockSpec((None, tq, 128), lambda i, p, j: (i, j, p)),
        scratch_shapes=[pltpu.VMEM((t, 128), MXU), pltpu.VMEM((2, t, 128), MXU),
                        pltpu.VMEM((2, tq, 128), MXU),
                        pltpu.VMEM((2, tq, 1), F32), pltpu.VMEM((2, tq, 128), F32)],
        compiler_params=pltpu.CompilerParams(
            dimension_semantics=("arbitrary", "arbitrary", "arbitrary"), vmem_limit_bytes=VMEM_LIMIT),
        name="fox_attention_prompt",
    )(proj, proj, proj, cumt)


def mixer_c_prompt(proj, b_forget):
    b, t, _ = proj.shape
    logft, cumt = fox_prep(proj, b_forget)
    o = fox_attention_prompt(proj, cumt)
    kv_new = proj[..., 1024:3072].reshape(b, t, 2, N_HEADS, HEAD_DIM)
    return o, [kv_new, logft.transpose(0, 2, 1)]


A_E = 10240
A_NK = 128


def _dil_prompt_kernel(*refs, first, last, dil):
    if first:
        q_ref, kc_ref, kp_ref, vc_ref, vp_ref = refs[:5]
        outs = refs[5:]
    else:
        q_ref, kc_ref, kp_ref, vc_ref, vp_ref, m_in, l_in, acc_in = refs[:8]
        outs = refs[8:]
    if last:
        (acc_out,) = outs
    else:
        m_out, l_out, acc_out = outs
    i = pl.program_id(2)
    nk = A_NK
    step = (nk + lax.broadcasted_iota(jnp.int32, (nk, 1), 0)) - lax.broadcasted_iota(jnp.int32, (1, 2 * nk), 1)
    kcol = lax.broadcasted_iota(jnp.int32, (1, 2 * nk), 1)
    valid = (step >= 0) & (step <= nk) & ((kcol >= nk) | (i > 0))
    distf = (step * dil).astype(F32)
    lane = lax.broadcasted_iota(jnp.int32, (1, 128), 1)
    if not last:
        m_out[...] = jnp.zeros(m_out.shape, F32)
        l_out[...] = jnp.zeros(l_out.shape, F32)
    for hp in range(N_HEADS // 2):
        cs = slice(128 * hp, 128 * (hp + 1))
        q2 = q_ref[:, cs]
        k2 = jnp.concatenate([kp_ref[:, cs], kc_ref[:, cs]], axis=0).astype(MXU)
        v2 = jnp.concatenate([vp_ref[:, cs], vc_ref[:, cs]], axis=0).astype(MXU)
        res = []
        for hh in range(2):
            h = 2 * hp + hh
            hmask = (lane >= 64 * hh) & (lane < 64 * (hh + 1))
            qm = (jnp.where(hmask, q2, 0.0) * (HEAD_DIM ** -0.5)).astype(MXU)
            s = _dot_nt(qm, k2) - _slope(h) * distf
            sm = jnp.where(valid, s, NEG)
            mx = jnp.max(sm, axis=1, keepdims=True)
            if first:
                m_new = mx
            else:
                m_old = m_in[:, h:h + 1]
                m_new = jnp.maximum(m_old, mx)
            p = jnp.where(valid, jnp.exp(sm - m_new), 0.0)
            l_new = jnp.sum(p, axis=1, keepdims=True)
            acc = jnp.dot(p.astype(MXU), v2, preferred_element_type=F32)
            if not first:
                alpha = jnp.exp(m_old - m_new)
                l_new = alpha * l_in[:, h:h + 1] + l_new
                acc = alpha * acc_in[:, cs] + acc
            if last:
                acc = acc / l_new
            else:
                m_out[:, h:h + 1] = m_new
                l_out[:, h:h + 1] = l_new
            res.append(acc)
        acc_out[:, cs] = jnp.where(lane < 64, res[0], res[1])


def dilated_attention_prompt(proj):
    b, t, _ = proj.shape
    nk = A_NK
    state = None
    for g, (window, dil) in enumerate(A_GROUPS):
        assert window // dil == nk and t % (dil * nk) == 0
        ln = t // dil
        first, last = g == 0, g == len(A_GROUPS) - 1
        pv = proj.reshape(b, ln, dil * A_E)
        col = lambda c, prev: pl.BlockSpec(
            (None, nk, 1024),
            (lambda bi, r, i, c=c: (bi, jnp.maximum(i - 1, 0), r * (A_E // 1024) + c)) if prev else
            (lambda bi, r, i, c=c: (bi, i, r * (A_E // 1024) + c)))
        st_spec = pl.BlockSpec((None, nk, 128), lambda bi, r, i: (bi, i, r))
        acc_spec = pl.BlockSpec((None, nk, 1024), lambda bi, r, i: (bi, i, r))
        in_specs = [col(3 * g, False), col(3 * g + 1, False), col(3 * g + 1, True),
                    col(3 * g + 2, False), col(3 * g + 2, True)]
        args = [pv] * 5
        if not first:
            in_specs += [st_spec, st_spec, acc_spec]
            args += [state[0].reshape(b, ln, dil * 128), state[1].reshape(b, ln, dil * 128),
                     state[2].reshape(b, ln, dil * 1024)]
        acc_shape = jax.ShapeDtypeStruct((b, ln, dil * 1024), F32)
        st_shape = jax.ShapeDtypeStruct((b, ln, dil * 128), F32)
        out = pl.pallas_call(
            functools.partial(_dil_prompt_kernel, first=first, last=last, dil=dil),
            out_shape=[acc_shape] if last else [st_shape, st_shape, acc_shape],
            grid=(b, dil, ln // nk),
            in_specs=in_specs,
            out_specs=[acc_spec] if last else [st_spec, st_spec, acc_spec],
            compiler_params=pltpu.CompilerParams(
                dimension_semantics=("arbitrary", "arbitrary", "arbitrary"), vmem_limit_bytes=VMEM_LIMIT),
            name=f"dilated_attention_prompt_g{g}",
        )(*args)
        state = [o.reshape(b, t, -1) for o in out]
    return state[0]


def mixer_a_prompt(proj):
    b, t, _ = proj.shape
    o = dilated_attention_prompt(proj)
    new_state = []
    for g, (window, _) in enumerate(A_GROUPS):
        w = min(window, t)
        new_state.append(proj[:, t - w:, 3072 * g + 1024:3072 * g + 3072].reshape(b, w, 2, N_HEADS, HEAD_DIM))
    return o, new_state


STEP_T = 8
STEP_LANES = N_HEADS * STEP_T


def _lane_iota():
    return lax.broadcasted_iota(jnp.int32, (1, STEP_LANES), 1)


def _row2col(x):
    n = x.shape[1]
    eye = lax.broadcasted_iota(jnp.int32, (n, n), 0) == lax.broadcasted_iota(jnp.int32, (n, n), 1)
    return jnp.sum(jnp.where(eye, x, 0.0), axis=1, keepdims=True)


def _step_state_init(m_ref, l_ref, acc_ref):
    m_ref[...] = jnp.full(m_ref.shape, NEG, F32)
    l_ref[...] = jnp.zeros(l_ref.shape, F32)
    acc_ref[...] = jnp.zeros(acc_ref.shape, F32)


def _step_update(s_t, valid, v_b, m_ref, l_ref, acc_ref):
    sm = s_t if valid is None else jnp.where(valid, s_t, NEG)
    m_old = m_ref[...]
    m_new = jnp.maximum(m_old, jnp.max(sm, axis=0, keepdims=True))
    p_t = jnp.exp(sm - m_new)
    if valid is not None:
        p_t = jnp.where(valid, p_t, 0.0)
    alpha = jnp.exp(m_old - m_new)
    l_ref[...] = alpha * l_ref[...] + jnp.sum(p_t, axis=0, keepdims=True)
    acc_ref[...] = _row2col(alpha) * acc_ref[...] + _dot_tn(p_t.astype(MXU), v_b)
    m_ref[...] = m_new


def _qbd_full(q):
    lane = _lane_iota()
    sel = jnp.where(lax.broadcasted_iota(jnp.int32, (STEP_T, 1), 0) == (lane & (STEP_T - 1)), 1.0, 0.0)
    qall = _dot_tn(q.astype(MXU), sel.astype(MXU))
    row_h = lax.broadcasted_iota(jnp.int32, (D_MODEL, 1), 0) // HEAD_DIM
    return (jnp.where(row_h == (lane >> 3), qall, 0.0) * (HEAD_DIM ** -0.5)).astype(MXU)


def _extract_heads(o):
    row_h = lax.broadcasted_iota(jnp.int32, (STEP_LANES, 1), 0) >> 3
    col_h = lax.broadcasted_iota(jnp.int32, (1, D_MODEL), 1) // HEAD_DIM
    om = jnp.where(row_h == col_h, o, 0.0)
    return jnp.sum(om.reshape(N_HEADS, STEP_T, D_MODEL), axis=0)


def _step_slopes():
    return jnp.asarray(np.repeat(np.float32(2.0 ** (-8.0 * np.arange(1, N_HEADS + 1) / N_HEADS)), STEP_T)
                       .reshape(1, STEP_LANES), F32)


A_STEP_TILE = 512


def _dil_step_kernel(slope_ref, new_ref, c2_ref, c1_ref, c0_ref, o_ref, qbd_ref, m_ref, l_ref, acc_ref):
    s = pl.program_id(1)
    n2 = A_GROUPS[2][0] // A_STEP_TILE
    ilane = _lane_iota() & (STEP_T - 1)
    slope = slope_ref[...]

    @pl.when(s == 0)
    def _():
        _step_state_init(m_ref, l_ref, acc_ref)
        for g in range(3):
            qbd_ref[g] = _qbd_full(new_ref[:, 3072 * g:3072 * g + 1024])

    def tile(k, v, g, row0):
        window, dil = A_GROUPS[g]
        rows = k.shape[0]
        s_t = jnp.dot(k.astype(MXU), qbd_ref[g], preferred_element_type=F32)
        kpos = row0 + lax.broadcasted_iota(jnp.int32, (rows, 1), 0)
        dist = (window + ilane) - kpos
        valid = (dist >= 0) & (dist <= window) & ((dist & (dil - 1)) == 0)
        _step_update(s_t - slope * dist.astype(F32), valid, v.astype(MXU), m_ref, l_ref, acc_ref)

    @pl.when(s < n2)
    def _():
        tile(c2_ref[:, 0:1024], c2_ref[:, 1024:2048], 2, s * A_STEP_TILE)

    @pl.when(s == n2)
    def _():
        tile(c1_ref[:, 0:1024], c1_ref[:, 1024:2048], 1, 0)

    @pl.when(s == n2 + 1)
    def _():
        tile(c0_ref[:, 0:1024], c0_ref[:, 1024:2048], 0, 0)
        for g in range(3):
            o = 3072 * g
            tile(new_ref[:, o + 1024:o + 2048], new_ref[:, o + 2048:o + 3072], g, A_GROUPS[g][0])
        o_ref[...] = _extract_heads(acc_ref[...] / _row2col(l_ref[...]))


def mixer_a_step(proj, cache):
    b, t, _ = proj.shape
    assert t == STEP_T and A_GROUPS[1][0] == A_STEP_TILE
    c0, c1, c2 = [c.reshape(b, c.shape[1], 2 * D_MODEL) for c in cache]
    n2 = A_GROUPS[2][0] // A_STEP_TILE
    o = pl.pallas_call(
        _dil_step_kernel,
        out_shape=jax.ShapeDtypeStruct((b, t, D_MODEL), F32),
        grid=(b, n2 + 2),
        in_specs=[
            pl.BlockSpec((1, STEP_LANES), lambda i, s: (0, 0)),
            pl.BlockSpec((None, t, A_E), lambda i, s: (i, 0, 0)),
            pl.BlockSpec((None, A_STEP_TILE, 2 * D_MODEL), lambda i, s: (i, jnp.minimum(s, n2 - 1), 0)),
            pl.BlockSpec((None, A_STEP_TILE, 2 * D_MODEL), lambda i, s: (i, 0, 0)),
            pl.BlockSpec((None, A_GROUPS[0][0], 2 * D_MODEL), lambda i, s: (i, 0, 0)),
        ],
        out_specs=pl.BlockSpec((None, t, D_MODEL), lambda i, s: (i, 0, 0)),
        scratch_shapes=[pltpu.VMEM((3, D_MODEL, STEP_LANES), MXU),
                        pltpu.VMEM((1, STEP_LANES), F32), pltpu.VMEM((1, STEP_LANES), F32),
                        pltpu.VMEM((STEP_LANES, D_MODEL), F32)],
        compiler_params=pltpu.CompilerParams(
            dimension_semantics=("arbitrary", "arbitrary"), vmem_limit_bytes=VMEM_LIMIT),
        name="dilated_attention_step",
    )(_step_slopes(), proj, c2, c1, c0)
    new_state = []
    for g in range(3):
        kv_new = proj[..., 3072 * g + 1024:3072 * g + 3072].reshape(b, t, 2, N_HEADS, HEAD_DIM)
        new_state.append(jnp.concatenate([cache[g][:, t:], kv_new], axis=1))
    return o, new_state


C_STEP_PAGES = 8


def _fox_step_kernel(pt_ref, new_ref, bf_ref, *refs):
    npg = C_STEP_PAGES
    kv_refs = refs[:npg]
    lf_refs = refs[npg:2 * npg]
    o_ref, lfo_ref, qbd_ref, m_ref, l_ref, acc_ref, carry_ref = refs[2 * npg:]
    s = pl.program_id(1)
    lane = _lane_iota()

    @pl.when(s == 0)
    def _():
        _step_state_init(m_ref, l_ref, acc_ref)
        carry_ref[...] = jnp.zeros(carry_ref.shape, F32)
        qbd_ref[...] = _qbd_full(new_ref[:, 0:1024])

    expand = jnp.where(lax.broadcasted_iota(jnp.int32, (128, 1), 0) == (lane >> 3), 1.0, 0.0)
    tri = jnp.where(lax.broadcasted_iota(jnp.int32, (PAGE_SIZE, PAGE_SIZE), 1)
                    <= lax.broadcasted_iota(jnp.int32, (PAGE_SIZE, PAGE_SIZE), 0), 1.0, 0.0)
    for p in range(npg):
        lfe = jnp.dot(lf_refs[p][...], expand[0:N_HEADS, :], preferred_element_type=F32, precision=HIGHEST)
        ck = jnp.dot(tri, lfe, preferred_element_type=F32, precision=HIGHEST) + carry_ref[...]
        carry_ref[...] = ck[PAGE_SIZE - 1:PAGE_SIZE, :]
        s_t = jnp.dot(kv_refs[p][:, 0:1024].astype(MXU), qbd_ref[...], preferred_element_type=F32) - ck
        _step_update(s_t, None, kv_refs[p][:, 1024:2048].astype(MXU), m_ref, l_ref, acc_ref)

    @pl.when(s == pl.num_programs(1) - 1)
    def _():
        logf = _log_sigmoid(new_ref[:, C_F_OFF:C_F_OFF + 128] + bf_ref[...])
        lfo_ref[...] = logf
        lfe = jnp.dot(logf, expand, preferred_element_type=F32, precision=HIGHEST)
        row = lax.broadcasted_iota(jnp.int32, (STEP_T, 1), 0)
        tri8 = jnp.where(lax.broadcasted_iota(jnp.int32, (STEP_T, STEP_T), 1) <= row, 1.0, 0.0)
        ck = jnp.dot(tri8, lfe, preferred_element_type=F32, precision=HIGHEST) + carry_ref[...]
        s_t = jnp.dot(new_ref[:, 1024:2048].astype(MXU), qbd_ref[...], preferred_element_type=F32) - ck
        valid = row <= (lane & (STEP_T - 1))
        _step_update(s_t, valid, new_ref[:, 2048:3072].astype(MXU), m_ref, l_ref, acc_ref)
        o_ref[...] = _extract_heads(acc_ref[...] / _row2col(l_ref[...]))


def mixer_c_step(proj, b_forget, kv_pool, lf_pool, page_table):
    b, t, _ = proj.shape
    n_pages = page_table.shape[1]
    npg = C_STEP_PAGES
    assert t == STEP_T and n_pages % npg == 0
    kvp = kv_pool.reshape(kv_pool.shape[0], PAGE_SIZE, 2 * D_MODEL)
    bf = jnp.pad(b_forget.astype(F32), (0, 128 - N_HEADS)).reshape(1, 128)
    page = lambda shape, p: pl.BlockSpec(shape, lambda i, s, pt, p=p: (pt[i, s * npg + p], 0, 0))
    o, logf = pl.pallas_call(
        _fox_step_kernel,
        out_shape=[jax.ShapeDtypeStruct((b, t, D_MODEL), F32), jax.ShapeDtypeStruct((b, t, 128), F32)],
        grid_spec=pltpu.PrefetchScalarGridSpec(
            num_scalar_prefetch=1,
            grid=(b, n_pages // npg),
            in_specs=[pl.BlockSpec((None, t, C_E), lambda i, s, pt: (i, 0, 0)),
                      pl.BlockSpec((1, 128), lambda i, s, pt: (0, 0))]
            + [page((None, PAGE_SIZE, 2 * D_MODEL), p) for p in range(npg)]
            + [page((None, PAGE_SIZE, N_HEADS), p) for p in range(npg)],
            out_specs=[pl.BlockSpec((None, t, D_MODEL), lambda i, s, pt: (i, 0, 0)),
                       pl.BlockSpec((None, t, 128), lambda i, s, pt: (i, 0, 0))],
            scratch_shapes=[pltpu.VMEM((D_MODEL, STEP_LANES), MXU),
                            pltpu.VMEM((1, STEP_LANES), F32), pltpu.VMEM((1, STEP_LANES), F32),
                            pltpu.VMEM((STEP_LANES, D_MODEL), F32), pltpu.VMEM((1, STEP_LANES), F32)]),
        compiler_params=pltpu.CompilerParams(
            dimension_semantics=("arbitrary", "arbitrary"), vmem_limit_bytes=VMEM_LIMIT),
        name="fox_attention_step",
    )(page_table, proj, bf, *([kvp] * npg), *([lf_pool] * npg))
    kv_new = proj[..., 1024:3072].reshape(b, t, 2, N_HEADS, HEAD_DIM)
    return o, [kv_new, logf[..., :N_HEADS]]


B_CMP_PAGES = 4
B_STEP_PAGES = 8


def _nsa_cmp_pages_kernel(pt_ref, *refs):
    pages = refs[:B_CMP_PAGES]
    pe_ref, w_ref, o_ref = refs[B_CMP_PAGES:]
    nb = PAGE_SIZE // B_BLOCK
    pooled = []
    for p in range(B_CMP_PAGES):
        x = pages[p][...].reshape(nb, B_BLOCK, 512) + pe_ref[...][None]
        pooled.append(jnp.sum(x, axis=1) * (1.0 / B_BLOCK))
    o_ref[...] = jnp.dot(jnp.concatenate(pooled, axis=0), w_ref[...], preferred_element_type=F32,
                         precision=HIGHEST)


def nsa_compress_pages(kv_pool, page_table, pe, wbd):
    b, n_pages = page_table.shape
    npg = B_CMP_PAGES
    nb = PAGE_SIZE // B_BLOCK
    page = lambda p: pl.BlockSpec((None, PAGE_SIZE, 512), lambda i, s, pt, p=p: (pt[i, s * npg + p], 0, 0))
    return pl.pallas_call(
        _nsa_cmp_pages_kernel,
        out_shape=jax.ShapeDtypeStruct((b, n_pages * nb, 512), F32),
        grid_spec=pltpu.PrefetchScalarGridSpec(
            num_scalar_prefetch=1,
            grid=(b, n_pages // npg),
            in_specs=[page(p) for p in range(npg)]
            + [pl.BlockSpec((B_BLOCK, 512), lambda i, s, pt: (0, 0)),
               pl.BlockSpec((512, 512), lambda i, s, pt: (0, 0))],
            out_specs=pl.BlockSpec((None, npg * nb, 512), lambda i, s, pt: (i, s, 0))),
        compiler_params=pltpu.CompilerParams(
            dimension_semantics=("arbitrary", "arbitrary"), vmem_limit_bytes=VMEM_LIMIT),
        name="nsa_compress_pages",
    )(page_table, *([kv_pool] * npg), pe, wbd)


def _nsa_step_kernel(pt_ref, slope_ref, new_ref, cmp_ref, win_ref, *refs, past):
    npg = B_STEP_PAGES
    pages = refs[:npg]
    (o_ref, qbd_ref, sel_ref, oc_ref, ms_ref, ls_ref, accs_ref, mw_ref, lw_ref, accw_ref) = refs[npg:]
    s = pl.program_id(1)
    lane = _lane_iota()
    ilane = lane & (STEP_T - 1)
    glane = lane >> 5
    rlane = (lane >> 3) & 3
    slope = slope_ref[...]
    pos = past + ilane
    ncb = past // B_BLOCK
    nsel = ncb + 8
    kvo = B_KV_OFF

    @pl.when(s == 0)
    def _():
        irow = lax.broadcasted_iota(jnp.int32, (STEP_T, 1), 0)
        qall = jnp.zeros((256, STEP_LANES), F32)
        for r in range(4):
            sel_r = jnp.where((irow == ilane) & (rlane == r), 1.0, 0.0)
            qall = qall + _dot_tn(new_ref[:, r * 256:(r + 1) * 256].astype(MXU), sel_r.astype(MXU))
        row_g = lax.broadcasted_iota(jnp.int32, (256, 1), 0) // HEAD_DIM
        qbd_ref[...] = (jnp.where(row_g == glane, qall, 0.0) * (HEAD_DIM ** -0.5)).astype(MXU)

        brow = lax.broadcasted_iota(jnp.int32, (ncb, 1), 0)
        blk_end = (brow + 1) * B_BLOCK - 1
        s_c = jnp.dot(cmp_ref[:, 0:256].astype(MXU), qbd_ref[...], preferred_element_type=F32)
        s_c = s_c - slope * (pos - blk_end).astype(F32)
        sm = jnp.where(blk_end <= pos, s_c, -jnp.inf)
        mx = jnp.max(sm, axis=0, keepdims=True)
        mx = jnp.where(mx > -jnp.inf, mx, 0.0)
        e = jnp.exp(sm - mx)
        den = jnp.sum(e, axis=0, keepdims=True)
        p_c = e / jnp.where(den > 0, den, 1.0)
        oc_ref[...] = _dot_tn(p_c.astype(MXU), cmp_ref[:, 256:512].astype(MXU))

        lrow = lax.broadcasted_iota(jnp.int32, (STEP_LANES, 1), 0)
        same = jnp.where(((lrow >> 5) == glane) & ((lrow & (STEP_T - 1)) == ilane), 1.0, 0.0)
        sc = jnp.dot(p_c, same, preferred_element_type=F32, precision=HIGHEST)
        sc = jnp.concatenate([sc, jnp.zeros((nsel - ncb, STEP_LANES), F32)], axis=0)
        nidx = lax.broadcasted_iota(jnp.int32, (nsel, 1), 0)
        cur = pos // B_BLOCK
        forced = (nidx == 0) | (nidx >= cur - 1)
        score = jnp.where(nidx > cur, -jnp.inf, jnp.where(forced, jnp.inf, sc))
        rank = jnp.zeros((nsel, STEP_LANES), F32)
        for n in range(ncb + 1):
            row = score[n:n + 1, :]
            beats = (row > score) | ((row == score) & (nidx > n))
            rank = rank + jnp.where(beats, 1.0, 0.0)
        sel_ref[...] = jnp.where(rank < float(B_TOPK), 1.0, 0.0)
        _step_state_init(ms_ref, ls_ref, accs_ref)

    prow = lax.broadcasted_iota(jnp.int32, (PAGE_SIZE, 1), 0)
    for p in range(npg):
        pg = s * npg + p
        kpos = pg * PAGE_SIZE + prow
        s_t = jnp.dot(pages[p][:, 0:256].astype(MXU), qbd_ref[...], preferred_element_type=F32)
        s_t = s_t - slope * (pos - kpos).astype(F32)
        sel_lo = sel_ref[pl.ds(2 * pg, 1), :]
        sel_hi = sel_ref[pl.ds(2 * pg + 1, 1), :]
        valid = jnp.where(prow < B_BLOCK, sel_lo, sel_hi) > 0.5
        _step_update(s_t, valid, pages[p][:, 256:512].astype(MXU), ms_ref, ls_ref, accs_ref)

    @pl.when(s == pl.num_programs(1) - 1)
    def _():
        nrow = lax.broadcasted_iota(jnp.int32, (STEP_T, 1), 0)
        dist_n = ilane - nrow
        s_t = jnp.dot(new_ref[:, kvo + 512:kvo + 768].astype(MXU), qbd_ref[...], preferred_element_type=F32)
        s_t = s_t - slope * dist_n.astype(F32)
        valid = (dist_n >= 0) & (sel_ref[ncb:ncb + 1, :] > 0.5)
        _step_update(s_t, valid, new_ref[:, kvo + 768:kvo + 1024].astype(MXU), ms_ref, ls_ref, accs_ref)
        _step_state_init(mw_ref, lw_ref, accw_ref)
        lw = win_ref.shape[0]
        kpos = (past - lw) + lax.broadcasted_iota(jnp.int32, (lw, 1), 0)
        dist = pos - kpos
        s_t = jnp.dot(win_ref[:, 0:256].astype(MXU), qbd_ref[...], preferred_element_type=F32)
        s_t = s_t - slope * dist.astype(F32)
        _step_update(s_t, (dist >= 0) & (dist <= B_WINDOW), win_ref[:, 256:512].astype(MXU),
                     mw_ref, lw_ref, accw_ref)
        s_t = jnp.dot(new_ref[:, kvo + 1024:kvo + 1280].astype(MXU), qbd_ref[...], preferred_element_type=F32)
        s_t = s_t - slope * dist_n.astype(F32)
        _step_update(s_t, dist_n >= 0, new_ref[:, kvo + 1280:kvo + 1536].astype(MXU), mw_ref, lw_ref, accw_ref)
        lrow = lax.broadcasted_iota(jnp.int32, (STEP_LANES, 1), 0)
        pick = jnp.where((lrow & (STEP_T - 1)) == lax.broadcasted_iota(jnp.int32, (1, STEP_T), 1), 1.0, 0.0)
        gmat = jnp.dot(pick, _sigmoid(new_ref[:, B_BG_OFF:B_BG_OFF + 128]), preferred_element_type=F32,
                       precision=HIGHEST)
        gcol = ((lrow >> 3) & 3) * 4 + (lrow >> 5)
        lane128 = lax.broadcasted_iota(jnp.int32, (1, 128), 1)
        gate = lambda br: jnp.sum(jnp.where(lane128 == gcol + 16 * br, gmat, 0.0), axis=1, keepdims=True)
        o_all = (gate(0) * oc_ref[...] + gate(1) * (accs_ref[...] / _row2col(ls_ref[...]))
                 + gate(2) * (accw_ref[...] / _row2col(lw_ref[...])))
        col_g = lax.broadcasted_iota(jnp.int32, (1, 256), 1) // HEAD_DIM
        for r in range(4):
            keep = (((lrow >> 3) & 3) == r) & ((lrow >> 5) == col_g)
            om = jnp.where(keep, o_all, 0.0)
            o_ref[:, r * 256:(r + 1) * 256] = jnp.sum(om.reshape(N_HEADS, STEP_T, 256), axis=0)


def mixer_b_step(proj, pos_emb, w_cmp, kv_pool, win_cache, page_table):
    b, t, _ = proj.shape
    n_pages = page_table.shape[1]
    npg = B_STEP_PAGES
    past = n_pages * PAGE_SIZE
    assert t == STEP_T and n_pages % npg == 0 and n_pages % B_CMP_PAGES == 0
    ncb = past // B_BLOCK
    pool = kv_pool.reshape(kv_pool.shape[0], PAGE_SIZE, 1024)
    pe, wbd = _nsa_cmp_weights(pos_emb, w_cmp)
    cmp = nsa_compress_pages(pool, page_table, pe, wbd)
    lw = win_cache.shape[1]
    win = win_cache.reshape(b, lw, 512)
    page = lambda p: pl.BlockSpec((None, PAGE_SIZE, 512), lambda i, s, pt, p=p: (pt[i, s * npg + p], 0, 1))
    o = pl.pallas_call(
        functools.partial(_nsa_step_kernel, past=past),
        out_shape=jax.ShapeDtypeStruct((b, t, D_MODEL), F32),
        grid_spec=pltpu.PrefetchScalarGridSpec(
            num_scalar_prefetch=1,
            grid=(b, n_pages // npg),
            in_specs=[pl.BlockSpec((1, STEP_LANES), lambda i, s, pt: (0, 0)),
                      pl.BlockSpec((None, t, B_E), lambda i, s, pt: (i, 0, 0)),
                      pl.BlockSpec((None, ncb, 512), lambda i, s, pt: (i, 0, 0)),
                      pl.BlockSpec((None, lw, 512), lambda i, s, pt: (i, 0, 0))]
            + [page(p) for p in range(npg)],
            out_specs=pl.BlockSpec((None, t, D_MODEL), lambda i, s, pt: (i, 0, 0)),
            scratch_shapes=[pltpu.VMEM((256, STEP_LANES), MXU),
                            pltpu.VMEM((ncb + 8, STEP_LANES), F32),
                            pltpu.VMEM((STEP_LANES, 256), F32),
                            pltpu.VMEM((1, STEP_LANES), F32), pltpu.VMEM((1, STEP_LANES), F32),
                            pltpu.VMEM((STEP_LANES, 256), F32),
                            pltpu.VMEM((1, STEP_LANES), F32), pltpu.VMEM((1, STEP_LANES), F32),
                            pltpu.VMEM((STEP_LANES, 256), F32)]),
        compiler_params=pltpu.CompilerParams(
            dimension_semantics=("arbitrary", "arbitrary"), vmem_limit_bytes=VMEM_LIMIT),
        name="nsa_attention_step",
    )(page_table, _step_slopes(), proj, cmp, win, *([pool] * npg))
    kv_full_new = proj[..., B_KV_OFF:B_KV_OFF + 1024].reshape(b, t, 4, B_KV_HEADS, HEAD_DIM)
    kv_win_new = proj[..., B_KV_OFF + 1024:B_KV_OFF + 1536].reshape(b, t, 2, B_KV_HEADS, HEAD_DIM)
    new_win = jnp.concatenate([win_cache[:, t:], kv_win_new], axis=1)
    return o, [kv_full_new, new_win]


def _tile_rows(x, n):
    return jnp.concatenate([x] * n, axis=0)


def _dot_hi(a, b):
    return jnp.dot(a, b, preferred_element_type=F32, precision=HIGHEST)


def _roll_kernel(c_ref, n_ref, o_ref):
    w = c_ref.shape[1]
    lane = lax.broadcasted_iota(jnp.int32, (1, 128), 1)
    place = jnp.where(lax.broadcasted_iota(jnp.int32, (STEP_T, 1), 0) == lane - (128 - STEP_T), 1.0, 0.0)
    new_t = lax.dot_general(n_ref[...], place, (((0,), (0,)), ((), ())), preferred_element_type=F32,
                            precision=HIGHEST)
    rolled = pltpu.roll(c_ref[...], w - STEP_T, axis=1)
    if w > 128:
        o_ref[:, 0:w - 128] = rolled[:, 0:w - 128]
    o_ref[:, w - 128:w] = jnp.where(lane >= 128 - STEP_T, new_t, rolled[:, w - 128:w])


def cache_roll(cache_t, new_rows, *, rb=512):
    nl, b, r, w = cache_t.shape
    return pl.pallas_call(
        _roll_kernel,
        out_shape=jax.ShapeDtypeStruct(cache_t.shape, F32),
        grid=(nl, b, r // rb),
        in_specs=[pl.BlockSpec((None, None, rb, w), lambda l, i, k: (l, i, k, 0)),
                  pl.BlockSpec((None, None, STEP_T, rb), lambda l, i, k: (l, i, 0, k))],
        out_specs=pl.BlockSpec((None, None, rb, w), lambda l, i, k: (l, i, k, 0)),
        compiler_params=pltpu.CompilerParams(
            dimension_semantics=("arbitrary", "arbitrary", "arbitrary"), vmem_limit_bytes=VMEM_LIMIT),
        name="cache_roll",
    )(cache_t, new_rows)


def _token_minor(x, token_axis):
    perm = [a for a in range(x.ndim) if a != token_axis] + [token_axis]
    return jnp.transpose(x, perm)


def _token_major(x, token_axis):
    perm = list(range(x.ndim - 1))
    perm.insert(token_axis, x.ndim - 1)
    return jnp.transpose(x, perm)


def _dil_step_kernel(slope_ref, new_ref, c0_ref, c1_ref, c2_ref, o_ref):
    hg = pl.program_id(1)
    rows = 4 * STEP_T
    rowi = lax.broadcasted_iota(jnp.int32, (rows, 1), 0) & (STEP_T - 1)
    bd = (lax.broadcasted_iota(jnp.int32, (rows, 1), 0) >> 3) == (lax.broadcasted_iota(jnp.int32, (1, 256), 1) >> 6)
    slope = slope_ref[...]
    pieces = []
    for g, c_ref in enumerate((c0_ref, c1_ref, c2_ref)):
        window, dil = A_GROUPS[g]
        col = lambda part: pl.ds(pl.multiple_of(3072 * g + 1024 * part + hg * 256, 256), 256)
        qbd = (jnp.where(bd, _tile_rows(new_ref[:, col(0)], 4), 0.0) * (HEAD_DIM ** -0.5)).astype(MXU)
        s = jnp.dot(qbd, c_ref[0].astype(MXU), preferred_element_type=F32)
        dist = (window + rowi) - lax.broadcasted_iota(jnp.int32, (1, window), 1)
        valid = (dist <= window) & ((dist & (dil - 1)) == 0)
        pieces.append((jnp.where(valid, s - slope * dist.astype(F32), NEG), c_ref[1].astype(MXU), True))
        sn = _dot_nt(qbd, new_ref[:, col(1)].astype(MXU))
        distn = rowi - lax.broadcasted_iota(jnp.int32, (1, STEP_T), 1)
        validn = (distn >= 0) & ((distn & (dil - 1)) == 0)
        pieces.append((jnp.where(validn, sn - slope * distn.astype(F32), NEG),
                       new_ref[:, col(2)].astype(MXU), False))
    m = pieces[0][0][:, 0:1]
    for s, _, _ in pieces:
        m = jnp.maximum(m, jnp.max(s, axis=1, keepdims=True))
    den = jnp.zeros((rows, 1), F32)
    acc = jnp.zeros((rows, 256), F32)
    for s, v, transposed in pieces:
        p = jnp.exp(s - m)
        den = den + jnp.sum(p, axis=1, keepdims=True)
        pb = p.astype(MXU)
        acc = acc + (_dot_nt(pb, v) if transposed else jnp.dot(pb, v, preferred_element_type=F32))
    o = jnp.where(bd, acc / den, 0.0)
    o_ref[...] = jnp.sum(o.reshape(4, STEP_T, 256), axis=0)


def mixer_a_step(proj, cache):
    b, t, _ = proj.shape
    assert t == STEP_T
    views = [_token_minor(c, 1).reshape(b, 2, D_MODEL, c.shape[1]) for c in cache]
    slopes = np.float32(2.0 ** (-8.0 * np.arange(1, N_HEADS + 1) / N_HEADS))
    slope_rows = jnp.asarray(np.repeat(slopes, STEP_T).reshape(4, 4 * STEP_T, 1), F32)
    cspec = lambda w: pl.BlockSpec((None, 2, 256, w), lambda i, h: (i, 0, h, 0))
    o = pl.pallas_call(
        _dil_step_kernel,
        out_shape=jax.ShapeDtypeStruct((b, t, D_MODEL), F32),
        grid=(b, 4),
        in_specs=[pl.BlockSpec((None, 4 * STEP_T, 1), lambda i, h: (h, 0, 0)),
                  pl.BlockSpec((None, t, A_E), lambda i, h: (i, 0, 0))]
        + [cspec(v.shape[-1]) for v in views],
        out_specs=pl.BlockSpec((None, t, 256), lambda i, h: (i, 0, h)),
        compiler_params=pltpu.CompilerParams(
            dimension_semantics=("arbitrary", "arbitrary"), vmem_limit_bytes=VMEM_LIMIT),
        name="dilated_attention_step",
    )(slope_rows, proj, *views)
    return o, [proj[..., 3072 * g + 1024:3072 * g + 3072] for g in range(3)]


def roll_a_caches(caches, new_rows):
    nl, b, w = caches.shape[:3]
    ct = _token_minor(caches, 2).reshape(nl, b, 2 * D_MODEL, w)
    out = cache_roll(ct, jnp.stack(new_rows))
    return _token_major(out.reshape(nl, b, 2, N_HEADS, HEAD_DIM, w), 2)


C_STEP_PAGES = 8


def _fox_step_kernel(pt_ref, new_ref, bf_ref, *refs):
    npg = C_STEP_PAGES
    kv_refs = refs[:npg]
    lf_refs = refs[npg:2 * npg]
    o_ref, lfo_ref, qbd_ref, m_ref, l_ref, acc_ref, carry_ref = refs[2 * npg:]
    s_id = pl.program_id(1)
    rows = 4 * STEP_T
    rowi = lax.broadcasted_iota(jnp.int32, (STEP_LANES, 1), 0) & (STEP_T - 1)
    bd = (lax.broadcasted_iota(jnp.int32, (rows, 1), 0) >> 3) == (lax.broadcasted_iota(jnp.int32, (1, 256), 1) >> 6)
    expand = jnp.where((lax.broadcasted_iota(jnp.int32, (STEP_LANES, 1), 0) >> 3)
                       == lax.broadcasted_iota(jnp.int32, (1, N_HEADS), 1), 1.0, 0.0)

    @pl.when(s_id == 0)
    def _():
        m_ref[...] = jnp.full(m_ref.shape, NEG, F32)
        l_ref[...] = jnp.zeros(l_ref.shape, F32)
        acc_ref[...] = jnp.zeros(acc_ref.shape, F32)
        carry_ref[...] = jnp.zeros(carry_ref.shape, F32)
        for hg in range(4):
            q = new_ref[:, 256 * hg:256 * (hg + 1)]
            qbd_ref[hg] = (jnp.where(bd, _tile_rows(q, 4), 0.0) * (HEAD_DIM ** -0.5)).astype(MXU)

    def update(s, valid, pv_fn):
        if valid is not None:
            s = jnp.where(valid, s, NEG)
        m_old = m_ref[...]
        m_new = jnp.maximum(m_old, _rowmax(s) if s.shape[1] % 128 == 0 else jnp.max(s, axis=1, keepdims=True))
        p = jnp.exp(s - m_new)
        alpha = jnp.exp(m_old - m_new)
        l_ref[...] = alpha * l_ref[...] + jnp.sum(p, axis=1, keepdims=True)
        pb = p.astype(MXU)
        pv = jnp.concatenate([pv_fn(pb[rows * hg:rows * (hg + 1)], hg) for hg in range(4)], axis=0)
        acc_ref[...] = alpha * acc_ref[...] + pv
        m_ref[...] = m_new

    triu = jnp.where(lax.broadcasted_iota(jnp.int32, (PAGE_SIZE, 1), 0)
                     <= lax.broadcasted_iota(jnp.int32, (1, PAGE_SIZE), 1), 1.0, 0.0)
    carry = carry_ref[...]
    cks = []
    for p in range(npg):
        cum = _dot_hi(lf_refs[p][...], triu) + carry
        carry = cum[:, PAGE_SIZE - 1:PAGE_SIZE]
        cks.append(cum)
    carry_ref[...] = carry
    ck = _dot_hi(expand, jnp.concatenate(cks, axis=1))
    s = jnp.concatenate(
        [jnp.dot(qbd_ref[hg],
                 jnp.concatenate([kv_refs[p][0, 256 * hg:256 * (hg + 1), :] for p in range(npg)], axis=1).astype(MXU),
                 preferred_element_type=F32) for hg in range(4)], axis=0) - ck
    update(s, None, lambda pb, hg: _dot_nt(
        pb, jnp.concatenate([kv_refs[p][1, 256 * hg:256 * (hg + 1), :] for p in range(npg)], axis=1).astype(MXU)))

    @pl.when(s_id == pl.num_programs(1) - 1)
    def _():
        logf = _log_sigmoid(new_ref[:, C_F_OFF:C_F_OFF + 128] + bf_ref[...])
        lfo_ref[...] = logf
        tri8 = jnp.where(lax.broadcasted_iota(jnp.int32, (STEP_T, 1), 0)
                         <= lax.broadcasted_iota(jnp.int32, (1, 128), 1), 1.0, 0.0)
        cum_n = lax.dot_general(logf, tri8, (((0,), (0,)), ((), ())), preferred_element_type=F32,
                                precision=HIGHEST)[0:N_HEADS, :] + carry_ref[...]
        ck_n = _dot_hi(expand, cum_n)[:, 0:STEP_T]
        s_n = jnp.concatenate(
            [_dot_nt(qbd_ref[hg], new_ref[:, 1024 + 256 * hg:1024 + 256 * (hg + 1)].astype(MXU))
             for hg in range(4)], axis=0) - ck_n
        valid = lax.broadcasted_iota(jnp.int32, (1, STEP_T), 1) <= rowi
        update(s_n, valid, lambda pb, hg: jnp.dot(
            pb, new_ref[:, 2048 + 256 * hg:2048 + 256 * (hg + 1)].astype(MXU), preferred_element_type=F32))
        o = acc_ref[...] / l_ref[...]
        for hg in range(4):
            om = jnp.where(bd, o[rows * hg:rows * (hg + 1)], 0.0)
            o_ref[:, 256 * hg:256 * (hg + 1)] = jnp.sum(om.reshape(4, STEP_T, 256), axis=0)


def mixer_c_step(proj, b_forget, kv_pool, lf_pool, page_table):
    b, t, _ = proj.shape
    n_pages = page_table.shape[1]
    npg = C_STEP_PAGES
    assert t == STEP_T and n_pages % npg == 0
    kvp = _token_minor(kv_pool, 1).reshape(kv_pool.shape[0], 2, D_MODEL, PAGE_SIZE)
    lfp = _token_minor(lf_pool, 1)
    bf = jnp.pad(b_forget.astype(F32), (0, 128 - N_HEADS)).reshape(1, 128)
    page = lambda shape, p: pl.BlockSpec(shape, lambda i, s, pt, p=p: (pt[i, s * npg + p],) + (0,) * (len(shape) - 1))
    o, logf = pl.pallas_call(
        _fox_step_kernel,
        out_shape=[jax.ShapeDtypeStruct((b, t, D_MODEL), F32), jax.ShapeDtypeStruct((b, t, 128), F32)],
        grid_spec=pltpu.PrefetchScalarGridSpec(
            num_scalar_prefetch=1,
            grid=(b, n_pages // npg),
            in_specs=[pl.BlockSpec((None, t, C_E), lambda i, s, pt: (i, 0, 0)),
                      pl.BlockSpec((1, 128), lambda i, s, pt: (0, 0))]
            + [page((None, 2, D_MODEL, PAGE_SIZE), p) for p in range(npg)]
            + [page((None, N_HEADS, PAGE_SIZE), p) for p in range(npg)],
            out_specs=[pl.BlockSpec((None, t, D_MODEL), lambda i, s, pt: (i, 0, 0)),
                       pl.BlockSpec((None, t, 128), lambda i, s, pt: (i, 0, 0))],
            scratch_shapes=[pltpu.VMEM((4, 4 * STEP_T, 256), MXU),
                            pltpu.VMEM((STEP_LANES, 1), F32), pltpu.VMEM((STEP_LANES, 1), F32),
                            pltpu.VMEM((STEP_LANES, 256), F32), pltpu.VMEM((N_HEADS, 1), F32)]),
        compiler_params=pltpu.CompilerParams(
            dimension_semantics=("arbitrary", "arbitrary"), vmem_limit_bytes=VMEM_LIMIT),
        name="fox_attention_step",
    )(page_table, proj, bf, *([kvp] * npg), *([lfp] * npg))
    kv_new = proj[..., 1024:3072].reshape(b, t, 2, N_HEADS, HEAD_DIM)
    return o, [kv_new, logf[..., :N_HEADS]]


B_STEP_PAGES = 8


def _nsa_cmp_pages_kernel(pt_ref, *refs):
    npg = B_STEP_PAGES
    pages = refs[:npg]
    pet_ref, w_ref, o_ref = refs[npg:]
    nb = npg * PAGE_SIZE // B_BLOCK
    pool = jnp.where(lax.broadcasted_iota(jnp.int32, (npg * PAGE_SIZE, 1), 0) // B_BLOCK
                     == lax.broadcasted_iota(jnp.int32, (1, 128), 1), 1.0 / B_BLOCK, 0.0).astype(MXU)
    outs = []
    for c in range(2):
        x = jnp.concatenate([pages[p][c] + pet_ref[c] for p in range(npg)], axis=1)
        hi = x.astype(MXU)
        lo = (x - hi.astype(F32)).astype(MXU)
        pooled = (jnp.dot(hi, pool, preferred_element_type=F32) + jnp.dot(lo, pool, preferred_element_type=F32))
        cmp_c = lax.dot_general(pooled, w_ref[c], (((0,), (0,)), ((), ())), preferred_element_type=F32,
                                precision=HIGHEST)
        outs.append(cmp_c[0:nb, :])
    o_ref[...] = jnp.concatenate(outs, axis=1)


def nsa_compress_pages(pool_t, page_table, pos_emb, w_cmp):
    b, n_pages = page_table.shape
    npg = B_STEP_PAGES
    nb = npg * PAGE_SIZE // B_BLOCK
    pet = pos_emb.transpose(0, 2, 3, 1).reshape(2, 256, B_BLOCK)
    pet = jnp.concatenate([pet] * (PAGE_SIZE // B_BLOCK), axis=2)
    _, wbd = _nsa_cmp_weights(pos_emb, w_cmp)
    wc = jnp.stack([wbd[0:256, 0:256], wbd[256:512, 256:512]])
    page = lambda p: pl.BlockSpec((None, 2, 256, PAGE_SIZE), lambda i, s, pt, p=p: (pt[i, s * npg + p], 0, 0, 0))
    return pl.pallas_call(
        _nsa_cmp_pages_kernel,
        out_shape=jax.ShapeDtypeStruct((b, n_pages * PAGE_SIZE // B_BLOCK, 512), F32),
        grid_spec=pltpu.PrefetchScalarGridSpec(
            num_scalar_prefetch=1,
            grid=(b, n_pages // npg),
            in_specs=[page(p) for p in range(npg)]
            + [pl.BlockSpec((2, 256, PAGE_SIZE), lambda i, s, pt: (0, 0, 0)),
               pl.BlockSpec((2, 256, 256), lambda i, s, pt: (0, 0, 0))],
            out_specs=pl.BlockSpec((None, nb, 512), lambda i, s, pt: (i, s, 0))),
        compiler_params=pltpu.CompilerParams(
            dimension_semantics=("arbitrary", "arbitrary"), vmem_limit_bytes=VMEM_LIMIT),
        name="nsa_compress_pages",
    )(page_table, *([pool_t] * npg), pet, wc)


def _nsa_step_kernel(pt_ref, slope_ref, new_ref, cmp_ref, win_ref, *refs, past):
    npg = B_STEP_PAGES
    pages = refs[:npg]
    o_ref, qbd_ref, sel_ref, oc_ref, m_ref, l_ref, acc_ref = refs[npg:]
    s_id = pl.program_id(1)
    rows = STEP_LANES
    ridx = lax.broadcasted_iota(jnp.int32, (rows, 1), 0)
    rowi = ridx & (STEP_T - 1)
    rowg = (ridx >> 3) & 3
    colg = lax.broadcasted_iota(jnp.int32, (1, 256), 1) >> 6
    bd = rowg == colg
    slope = slope_ref[...]
    pos = past + rowi
    ncb = past // B_BLOCK
    nsel = sel_ref.shape[1]
    kvo = B_KV_OFF

    def update(s, valid, pv):
        s = jnp.where(valid, s, NEG)
        m_old = m_ref[...]
        m_new = jnp.maximum(m_old, jnp.max(s, axis=1, keepdims=True))
        p = jnp.exp(s - m_new)
        alpha = jnp.exp(m_old - m_new)
        l_ref[...] = alpha * l_ref[...] + jnp.sum(p, axis=1, keepdims=True)
        acc_ref[...] = alpha * acc_ref[...] + pv(p.astype(MXU))
        m_ref[...] = m_new

    @pl.when(s_id == 0)
    def _():
        qbd = jnp.concatenate(
            [jnp.where(bd[0:32], _tile_rows(new_ref[:, 256 * r:256 * (r + 1)], 4), 0.0) for r in range(4)], axis=0)
        qbd_ref[...] = (qbd * (HEAD_DIM ** -0.5)).astype(MXU)
        blk_end = (lax.broadcasted_iota(jnp.int32, (1, ncb), 1) + 1) * B_BLOCK - 1
        s_c = _dot_nt(qbd_ref[...], cmp_ref[:, 0:256].astype(MXU)) - slope * (pos - blk_end).astype(F32)
        sm = jnp.where(blk_end <= pos, s_c, -jnp.inf)
        mx = jnp.max(sm, axis=1, keepdims=True)
        mx = jnp.where(mx > -jnp.inf, mx, 0.0)
        e = jnp.exp(sm - mx)
        den = jnp.sum(e, axis=1, keepdims=True)
        p_c = e / jnp.where(den > 0, den, 1.0)
        oc_ref[...] = jnp.dot(p_c.astype(MXU), cmp_ref[:, 256:512].astype(MXU), preferred_element_type=F32)
        cidx = lax.broadcasted_iota(jnp.int32, (1, rows), 1)
        same = jnp.where((((cidx >> 3) & 3) == rowg) & ((cidx & (STEP_T - 1)) == rowi), 1.0, 0.0)
        sc = _dot_hi(same, p_c)
        sc = jnp.concatenate([sc, jnp.zeros((rows, nsel - ncb), F32)], axis=1)
        nidx = lax.broadcasted_iota(jnp.int32, (1, nsel), 1)
        cur = pos // B_BLOCK
        forced = (nidx == 0) | (nidx >= cur - 1)
        score = jnp.where(nidx > cur, -jnp.inf, jnp.where(forced, jnp.inf, sc))
        rank = jnp.zeros((rows, nsel), F32)
        for n in range(ncb + 1):
            colv = score[:, n:n + 1]
            beats = (colv > score) | ((colv == score) & (nidx > n))
            rank = rank + jnp.where(beats, 1.0, 0.0)
        sel_ref[...] = jnp.where(rank < float(B_TOPK), 1.0, 0.0).astype(MXU)
        m_ref[...] = jnp.full(m_ref.shape, NEG, F32)
        l_ref[...] = jnp.zeros(l_ref.shape, F32)
        acc_ref[...] = jnp.zeros(acc_ref.shape, F32)

    nk = npg * PAGE_SIZE
    kbase = s_id * nk
    kt = jnp.concatenate([pages[p][0] for p in range(npg)], axis=1).astype(MXU)
    vt = jnp.concatenate([pages[p][1] for p in range(npg)], axis=1).astype(MXU)
    kpos = kbase + lax.broadcasted_iota(jnp.int32, (1, nk), 1)
    s = jnp.dot(qbd_ref[...], kt, preferred_element_type=F32) - slope * (pos - kpos).astype(F32)
    ex = jnp.where(lax.broadcasted_iota(jnp.int32, (nsel, 1), 0) == (kpos // B_BLOCK), 1.0, 0.0).astype(MXU)
    selk = jnp.dot(sel_ref[...], ex, preferred_element_type=F32)
    update(s, selk > 0.5, lambda pb: _dot_nt(pb, vt))

    @pl.when(s_id == pl.num_programs(1) - 1)
    def _():
        distn = rowi - lax.broadcasted_iota(jnp.int32, (1, STEP_T), 1)
        sn = _dot_nt(qbd_ref[...], new_ref[:, kvo + 512:kvo + 768].astype(MXU)) - slope * distn.astype(F32)
        vn = new_ref[:, kvo + 768:kvo + 1024].astype(MXU)
        seln = sel_ref[:, ncb:ncb + 1].astype(F32) > 0.5
        update(sn, (distn >= 0) & seln, lambda pb: jnp.dot(pb, vn, preferred_element_type=F32))
        o_s = acc_ref[...] / l_ref[...]
        lw = win_ref.shape[2]
        dist = pos - ((past - lw) + lax.broadcasted_iota(jnp.int32, (1, lw), 1))
        sw = jnp.dot(qbd_ref[...], win_ref[0].astype(MXU), preferred_element_type=F32) - slope * dist.astype(F32)
        sw = jnp.where((dist >= 0) & (dist <= B_WINDOW), sw, NEG)
        swn = _dot_nt(qbd_ref[...], new_ref[:, kvo + 1024:kvo + 1280].astype(MXU)) - slope * distn.astype(F32)
        swn = jnp.where(distn >= 0, swn, NEG)
        mw = jnp.maximum(_rowmax(sw), jnp.max(swn, axis=1, keepdims=True))
        pw = jnp.exp(sw - mw)
        pwn = jnp.exp(swn - mw)
        o_w = (_dot_nt(pw.astype(MXU), win_ref[1].astype(MXU))
               + jnp.dot(pwn.astype(MXU), new_ref[:, kvo + 1280:kvo + 1536].astype(MXU), preferred_element_type=F32))
        o_w = o_w / (jnp.sum(pw, axis=1, keepdims=True) + jnp.sum(pwn, axis=1, keepdims=True))
        pick = jnp.where(rowi == lax.broadcasted_iota(jnp.int32, (1, STEP_T), 1), 1.0, 0.0)
        gmat = _dot_hi(pick, _sigmoid(new_ref[:, B_BG_OFF:B_BG_OFF + 128]))
        gcol = (ridx >> 5) * 4 + rowg
        lane128 = lax.broadcasted_iota(jnp.int32, (1, 128), 1)
        gate = lambda br: jnp.sum(jnp.where(lane128 == gcol + 16 * br, gmat, 0.0), axis=1, keepdims=True)
        o_all = jnp.where(bd, gate(0) * oc_ref[...] + gate(1) * o_s + gate(2) * o_w, 0.0)
        for r in range(4):
            o_ref[:, 256 * r:256 * (r + 1)] = jnp.sum(o_all[32 * r:32 * (r + 1)].reshape(4, STEP_T, 256), axis=0)


def mixer_b_step(proj, pos_emb, w_cmp, kv_pool, win_cache, page_table):
    b, t, _ = proj.shape
    n_pages = page_table.shape[1]
    npg = B_STEP_PAGES
    past = n_pages * PAGE_SIZE
    assert t == STEP_T and n_pages % npg == 0
    ncb = past // B_BLOCK
    nsel = -(-(ncb + 1) // 128) * 128
    pool_t = _token_minor(kv_pool, 1).reshape(kv_pool.shape[0], 4, 256, PAGE_SIZE)
    cmp = nsa_compress_pages(pool_t, page_table, pos_emb, w_cmp)
    lw = win_cache.shape[1]
    win_t = _token_minor(win_cache, 1).reshape(b, 2, 256, lw)
    heads = np.array([4 * g + r for r in range(4) for g in range(4)])
    slopes = np.float32(2.0 ** (-8.0 * (heads + 1) / N_HEADS))
    slope_rows = jnp.asarray(np.repeat(slopes, STEP_T).reshape(STEP_LANES, 1), F32)
    page = lambda p: pl.BlockSpec((None, 2, 256, PAGE_SIZE), lambda i, s, pt, p=p: (pt[i, s * npg + p], 1, 0, 0))
    o = pl.pallas_call(
        functools.partial(_nsa_step_kernel, past=past),
        out_shape=jax.ShapeDtypeStruct((b, t, D_MODEL), F32),
        grid_spec=pltpu.PrefetchScalarGridSpec(
            num_scalar_prefetch=1,
            grid=(b, n_pages // npg),
            in_specs=[pl.BlockSpec((STEP_LANES, 1), lambda i, s, pt: (0, 0)),
                      pl.BlockSpec((None, t, B_E), lambda i, s, pt: (i, 0, 0)),
                      pl.BlockSpec((None, ncb, 512), lambda i, s, pt: (i, 0, 0)),
                      pl.BlockSpec((None, 2, 256, lw), lambda i, s, pt: (i, 0, 0, 0))]
            + [page(p) for p in range(npg)],
            out_specs=pl.BlockSpec((None, t, D_MODEL), lambda i, s, pt: (i, 0, 0)),
            scratch_shapes=[pltpu.VMEM((STEP_LANES, 256), MXU),
                            pltpu.VMEM((STEP_LANES, nsel), MXU),
                            pltpu.VMEM((STEP_LANES, 256), F32),
                            pltpu.VMEM((STEP_LANES, 1), F32), pltpu.VMEM((STEP_LANES, 1), F32),
                            pltpu.VMEM((STEP_LANES, 256), F32)]),
        compiler_params=pltpu.CompilerParams(
            dimension_semantics=("arbitrary", "arbitrary"), vmem_limit_bytes=VMEM_LIMIT),
        name="nsa_attention_step",
    )(page_table, slope_rows, proj, cmp, win_t, *([pool_t] * npg))
    kv_full_new = proj[..., B_KV_OFF:B_KV_OFF + 1024].reshape(b, t, 4, B_KV_HEADS, HEAD_DIM)
    return o, [kv_full_new, proj[..., B_KV_OFF + 1024:B_KV_OFF + 1536]]


def roll_b_window(win_caches, new_rows):
    nl, b, w = win_caches.shape[:3]
    ct = _token_minor(win_caches, 2).reshape(nl, b, 2 * B_KV_WIDTH, w)
    out = cache_roll(ct, jnp.stack(new_rows))
    return _token_major(out.reshape(nl, b, 2, B_KV_HEADS, HEAD_DIM, w), 2)


def alibi_slopes(n):
    return jnp.asarray(2.0 ** (-8.0 * np.arange(1, n + 1) / n), dtype=jnp.float32)


def masked_softmax(s, mask, axis=-1):
    s = jnp.where(mask, s, -jnp.inf)
    m = jnp.max(s, axis=axis, keepdims=True)
    m = jnp.where(jnp.isfinite(m), m, 0.0)
    e = jnp.exp(s - m)
    den = jnp.sum(e, axis=axis, keepdims=True)
    p = e / jnp.where(den > 0, den, 1.0)
    lse = jnp.squeeze(m + jnp.log(den), axis=axis)
    return p, lse


def gather_pages(pool, page_table):
    g = pool[page_table]
    return g.reshape((g.shape[0], g.shape[1] * g.shape[2]) + g.shape[3:])


def dilated_attn_prompt(q, k, v, window, dilation, slopes):
    b, t, h, hd = q.shape
    nk = window // dilation
    ln = t // dilation
    nb = -(-ln // nk)
    lp = nb * nk

    def streams(z):
        z = z.reshape(b, ln, dilation, h, hd).transpose(0, 2, 1, 3, 4)
        z = jnp.pad(z, ((0, 0), (0, 0), (0, lp - ln), (0, 0), (0, 0)))
        return z.reshape(b, dilation, nb, nk, h, hd)

    def with_prev(z):
        prev = jnp.pad(z, ((0, 0), (0, 0), (1, 0), (0, 0), (0, 0), (0, 0)))[:, :, :nb]
        return jnp.concatenate([prev, z], axis=3)

    qs = streams(q)
    kb = with_prev(streams(k))
    vb = with_prev(streams(v))
    s = jnp.einsum('brnqhd,brnkhd->brnhqk', qs, kb).astype(jnp.float32) * (hd ** -0.5)
    step = (nk + jnp.arange(nk))[:, None] - jnp.arange(2 * nk)[None, :]
    first = ((jnp.arange(nb) - 1)[:, None, None] * nk + jnp.arange(2 * nk)[None, None, :]) >= 0
    mask = (step >= 0) & (step <= nk) & first
    s = s - slopes[:, None, None] * (step * dilation).astype(jnp.float32)
    p, lse = masked_softmax(s, mask[None, None, :, None])
    o = jnp.einsum('brnhqk,brnkhd->brnqhd', p.astype(v.dtype), vb)
    o = o.reshape(b, dilation, lp, h, hd)[:, :, :ln].transpose(0, 2, 1, 3, 4).reshape(b, t, h, hd)
    lse = lse.transpose(0, 1, 2, 4, 3).reshape(b, dilation, lp, h)[:, :, :ln]
    lse = lse.transpose(0, 2, 1, 3).reshape(b, t, h)
    return o, lse


def dilated_attn_step(q, k_all, v_all, window, dilation, slopes):
    b, tq, h, hd = q.shape
    lk = k_all.shape[1]
    nk = window // dilation
    steps = jnp.arange(nk + 1)
    idx = (lk - tq + jnp.arange(tq))[:, None] - dilation * steps[None, :]
    valid = idx >= 0
    idx = jnp.maximum(idx, 0)
    kg = k_all[:, idx]
    vg = v_all[:, idx]
    s = jnp.einsum('bqhd,bqjhd->bhqj', q, kg).astype(jnp.float32) * (hd ** -0.5)
    s = s - slopes[:, None, None] * (dilation * steps).astype(jnp.float32)
    p, lse = masked_softmax(s, valid)
    o = jnp.einsum('bhqj,bqjhd->bqhd', p.astype(v_all.dtype), vg)
    return o, lse.transpose(0, 2, 1)


def mixer_a_jax(proj, cache):
    b, t, _ = proj.shape
    n_g = len(A_GROUPS)
    qkv = proj[..., :3 * n_g * A_WIDTH].reshape(b, t, n_g, 3, N_HEADS, HEAD_DIM)
    slopes = alibi_slopes(N_HEADS)
    outs, lses, new_state = [], [], []
    for g, (window, dilation) in enumerate(A_GROUPS):
        q = qkv[:, :, g, 0]
        kv_new = qkv[:, :, g, 1:]
        if cache is None:
            o, lse = dilated_attn_prompt(q, kv_new[:, :, 0], kv_new[:, :, 1], window, dilation, slopes)
            new_state.append(kv_new[:, t - min(window, t):])
        else:
            kv_all = jnp.concatenate([cache[g], kv_new], axis=1)
            o, lse = dilated_attn_step(q, kv_all[:, :, 0], kv_all[:, :, 1], window, dilation, slopes)
            new_state.append(kv_all[:, t:])
        outs.append(o)
        lses.append(lse)
    wts = jax.nn.softmax(jnp.stack(lses), axis=0)
    o = jnp.einsum('gbth,gbthd->bthd', wts, jnp.stack(outs).astype(jnp.float32))
    return o.reshape(b, t, A_WIDTH), new_state


def nsa_attention(q, kv_full, kv_win, gates, pos_emb, w_cmp):
    b, tq, hq, hd = q.shape
    l = kv_full.shape[1]
    lw = kv_win.shape[1]
    g = B_KV_HEADS
    r = hq // g
    scale = hd ** -0.5
    dt = q.dtype
    slopes = alibi_slopes(hq).reshape(g, r)
    nblk = -(-l // B_BLOCK)
    lp = nblk * B_BLOCK
    blocks = jnp.pad(kv_full, ((0, 0), (0, lp - l), (0, 0), (0, 0), (0, 0))).reshape(b, nblk, B_BLOCK, 4, g, hd)
    pooled = jnp.mean(blocks[:, :, :, :2] + pos_emb.transpose(1, 0, 2, 3), axis=2)
    cmp = jnp.einsum('bncgd,cgde->bncge', pooled, w_cmp)
    k_cmp, v_cmp = cmp[:, :, 0], cmp[:, :, 1]
    sel_blocks = blocks[:, :, :, 2:].transpose(0, 4, 1, 2, 3, 5)
    kw_p = jnp.pad(kv_win, ((0, 0), (B_WINDOW, 0), (0, 0), (0, 0), (0, 0)))
    blk_end = (jnp.arange(nblk) + 1) * B_BLOCK - 1
    bidx = jnp.arange(nblk)
    n_sel = min(B_TOPK, nblk)
    qb_size = B_QBLOCK if tq % B_QBLOCK == 0 else tq
    nq = tq // qb_size

    def one_block(args):
        qb, gb, j = args
        pos = l - tq + j * qb_size + jnp.arange(qb_size)
        qg = qb.reshape(b, qb_size, g, r, hd)
        s_c = jnp.einsum('bqgrd,bngd->bgrqn', qg, k_cmp).astype(jnp.float32) * scale
        s_c = s_c - slopes[:, :, None, None] * (pos[:, None] - blk_end[None, :]).astype(jnp.float32)
        p_c, _ = masked_softmax(s_c, blk_end[None, :] <= pos[:, None])
        o_c = jnp.einsum('bgrqn,bngd->bqgrd', p_c.astype(dt), v_cmp)
        cur = pos // B_BLOCK
        forced = (bidx[None, :] == 0) | (bidx[None, :] >= cur[:, None] - 1)
        score = jnp.where(bidx[None, :] > cur[:, None], -jnp.inf,
                          jnp.where(forced, jnp.inf, jnp.sum(p_c, axis=2)))
        _, sel = lax.top_k(score, n_sel)
        kv_sel = jax.vmap(jax.vmap(lambda blk, ix: blk[ix]))(sel_blocks, sel)
        kpos = sel[..., None] * B_BLOCK + jnp.arange(B_BLOCK)
        dist_s = pos[None, None, :, None, None] - kpos
        s_s = jnp.einsum('bqgrd,bgqnkd->bgrqnk', qg, kv_sel[..., 0, :]).astype(jnp.float32) * scale
        s_s = s_s - slopes[None, :, :, None, None, None] * dist_s[:, :, None].astype(jnp.float32)
        p_s, _ = masked_softmax(s_s, (dist_s >= 0)[:, :, None], axis=(-2, -1))
        o_s = jnp.einsum('bgrqnk,bgqnkd->bqgrd', p_s.astype(dt), kv_sel[..., 1, :])
        start = lw - tq + j * qb_size
        kvw = lax.dynamic_slice_in_dim(kw_p, start, B_WINDOW + qb_size, axis=1)
        kpos_w = l - tq + j * qb_size - B_WINDOW + jnp.arange(B_WINDOW + qb_size)
        dist_w = pos[:, None] - kpos_w[None, :]
        mask_w = (dist_w >= 0) & (dist_w <= B_WINDOW) & (kpos_w[None, :] >= l - lw)
        s_w = jnp.einsum('bqgrd,bkgd->bgrqk', qg, kvw[:, :, 0]).astype(jnp.float32) * scale
        s_w = s_w - slopes[:, :, None, None] * dist_w.astype(jnp.float32)
        p_w, _ = masked_softmax(s_w, mask_w)
        o_w = jnp.einsum('bgrqk,bkgd->bqgrd', p_w.astype(dt), kvw[:, :, 1])
        gb = gb.reshape(b, qb_size, g, r, 3)
        o = gb[..., 0:1] * o_c + gb[..., 1:2] * o_s + gb[..., 2:3] * o_w
        return o.reshape(b, qb_size, hq, hd)

    qs = q.reshape(b, nq, qb_size, hq, hd).transpose(1, 0, 2, 3, 4)
    gs = gates.reshape(b, nq, qb_size, hq, 3).transpose(1, 0, 2, 3, 4)
    out = lax.map(one_block, (qs, gs, jnp.arange(nq)))
    return out.transpose(1, 0, 2, 3, 4).reshape(b, tq, hq, hd)


def mixer_b_jax(proj, pos_emb, w_cmp, cache):
    b, t, _ = proj.shape
    q = proj[..., :D_MODEL].reshape(b, t, N_HEADS, HEAD_DIM)
    o1 = D_MODEL + 6 * B_KV_WIDTH
    kv = lax.optimization_barrier(proj[..., D_MODEL:o1].reshape(b, t, 6, B_KV_HEADS, HEAD_DIM))
    branch_gates = jax.nn.sigmoid(proj[..., o1:o1 + 3 * N_HEADS].reshape(b, t, N_HEADS, 3))
    kv_full_new, kv_win_new = kv[:, :, :4], kv[:, :, 4:]
    if cache is None:
        kv_full, kv_win = kv_full_new, kv_win_new
        new_win = kv_win_new[:, t - min(B_WINDOW, t):]
    else:
        kv_full = jnp.concatenate([cache[0], kv_full_new], axis=1)
        kv_win = jnp.concatenate([cache[1], kv_win_new], axis=1)
        new_win = kv_win[:, t:]
    o = nsa_attention(q, kv_full, kv_win, branch_gates, pos_emb, w_cmp)
    return o.reshape(b, t, D_MODEL), [kv_full_new, new_win]


def fox_attention(q, k, v, cum):
    b, tq, h, hd = q.shape
    l = k.shape[1]
    scale = hd ** -0.5
    qb_size = C_QBLOCK if tq % C_QBLOCK == 0 else tq
    nq = tq // qb_size
    cum_k = cum.transpose(0, 2, 1)
    cum_q = cum[:, l - tq:].reshape(b, nq, qb_size, h).transpose(1, 0, 3, 2)
    kpos = jnp.arange(l)

    def one_block(args):
        qb, cq, j = args
        pos = l - tq + j * qb_size + jnp.arange(qb_size)
        s = jnp.einsum('bqhd,bkhd->bhqk', qb, k).astype(jnp.float32) * scale
        s = s + cq[..., None] - cum_k[:, :, None, :]
        p, _ = masked_softmax(s, kpos[None, :] <= pos[:, None])
        return jnp.einsum('bhqk,bkhd->bqhd', p.astype(v.dtype), v)

    qs = q.reshape(b, nq, qb_size, h, hd).transpose(1, 0, 2, 3, 4)
    out = lax.map(one_block, (qs, cum_q, jnp.arange(nq)))
    return out.transpose(1, 0, 2, 3, 4).reshape(b, tq, h, hd)


def mixer_c_jax(proj, b_forget, cache):
    b, t, _ = proj.shape
    qkv = proj[..., :3 * D_MODEL].reshape(b, t, 3, N_HEADS, HEAD_DIM)
    logf = jax.nn.log_sigmoid(proj[..., 3 * D_MODEL:3 * D_MODEL + N_HEADS].astype(jnp.float32)
                              + b_forget.astype(jnp.float32))
    kv_new = qkv[:, :, 1:]
    if cache is None:
        kv_all, logf_all = kv_new, logf
    else:
        kv_all = jnp.concatenate([cache[0], kv_new], axis=1)
        logf_all = jnp.concatenate([cache[1].astype(jnp.float32), logf], axis=1)
    cum = jnp.cumsum(logf_all, axis=1)
    o = fox_attention(qkv[:, :, 0], kv_all[:, :, 0], kv_all[:, :, 1], cum)
    return o.reshape(b, t, D_MODEL), [kv_new, logf]


def _pad_cols(w, e_pad):
    return jnp.pad(w, ((0, 0), (0, e_pad - w.shape[1])))


def run_trunk(x, mods, layer_cache, weights, *, bt, tt):
    (norm_pre, norm_post, a_w_in, a_w_out, b_w_in, b_pos_emb, b_w_cmp, b_w_out,
     c_w_in, c_b_forget, c_w_out) = weights
    states = []
    for i in range(DEPTH):
        kind, j = i % N_MIXERS, i // N_MIXERS
        mod = mods[i][:, None, :]
        cache = layer_cache(i)
        if kind == 0:
            w_in = a_w_in[j].astype(BF16)
            proj = in_projection(x, mod, norm_pre[i], w_in, bt=bt, tt=tt, tn=1024)
            o, st = mixer_a_prompt(proj) if cache is None else mixer_a_step(proj, cache)
            gate_col, w_out = A_E // 1024 - 1, a_w_out[j]
        elif kind == 1:
            w_in = _pad_cols(b_w_in[j][:, _b_in_perm()], B_E).astype(BF16)
            proj = in_projection(x, mod, norm_pre[i], w_in, bt=bt, tt=tt, tn=1024)
            if cache is None:
                o, st = mixer_b_prompt(proj, b_pos_emb[j], b_w_cmp[j])
            else:
                o, st = mixer_b_step(proj, b_pos_emb[j], b_w_cmp[j], *cache)
            gate_col, w_out = 1, b_w_out[j][_b_out_perm(), :]
        else:
            w = c_w_in[j]
            w_in = jnp.concatenate([w[:, :3072], w[:, 3088:4112], _pad_cols(w[:, 3072:3088], 128)],
                                   axis=1).astype(BF16)
            proj = in_projection(x, mod, norm_pre[i], w_in, bt=bt, tt=tt, tn=C_E // 3)
            if cache is None:
                o, st = mixer_c_prompt(proj, c_b_forget[j])
            else:
                o, st = mixer_c_step(proj, c_b_forget[j], *cache)
            gate_col, w_out = 3, c_w_out[j]
        x = out_projection(o, proj, gate_col, x, mod, norm_post[i], w_out.astype(BF16), bt=bt, tt=tt)
        states.append(st)
    return x, states


def stack_state(states, kind, k):
    return jnp.stack([states[i][k] for i in range(DEPTH) if i % N_MIXERS == kind])


def kernel(x_prompt, x_sample, c_prompt, c_sample, cache_a_w128, cache_a_w512, cache_a_w2048,
           cache_b_kv, cache_b_win, cache_c_kv, cache_c_logf, page_table,
           ada_w, ada_b, norm_pre, norm_post, a_w_in, a_w_out, b_w_in, b_pos_emb, b_w_cmp, b_w_out,
           c_w_in, c_b_forget, c_w_out):
    weights = (norm_pre, norm_post, a_w_in, a_w_out, b_w_in, b_pos_emb, b_w_cmp, b_w_out,
               c_w_in, c_b_forget, c_w_out)
    nbp, nbs = x_prompt.shape[0], x_sample.shape[0]
    nb_pad = -(-(nbp + nbs) // 8) * 8
    c_all = jnp.concatenate([c_prompt, c_sample,
                             jnp.zeros((nb_pad - nbp - nbs, D_MODEL), F32)], axis=0)
    mods = ada_modulation(c_all, ada_w, ada_b)
    mods_p = mods[:, :nbp]
    mods_s = mods[:, nbp:nbp + nbs]

    def prompt_cache(i):
        return None

    def sample_cache(i):
        kind, j = i % N_MIXERS, i // N_MIXERS
        if kind == 0:
            return [cache_a_w128[j], cache_a_w512[j], cache_a_w2048[j]]
        if kind == 1:
            return [cache_b_kv[j], cache_b_win[j], page_table]
        return [cache_c_kv[j], cache_c_logf[j], page_table]

    y_prompt, st_p = run_trunk(x_prompt, mods_p, prompt_cache, weights, bt=1, tt=1024)
    y_sample, st_s = run_trunk(x_sample, mods_s, sample_cache, weights, bt=nbs, tt=x_sample.shape[1])

    outs = [y_prompt, y_sample]
    outs += [stack_state(st_p, 0, 0), stack_state(st_p, 0, 1), stack_state(st_p, 0, 2),
             stack_state(st_p, 1, 0), stack_state(st_p, 1, 1),
             stack_state(st_p, 2, 0), stack_state(st_p, 2, 1)]
    layers = lambda kind: [i for i in range(DEPTH) if i % N_MIXERS == kind]
    outs += [roll_a_caches(c, [st_s[i][g] for i in layers(0)])
             for g, c in enumerate((cache_a_w128, cache_a_w512, cache_a_w2048))]
    outs += [stack_state(st_s, 1, 0), roll_b_window(cache_b_win, [st_s[i][1] for i in layers(1)]),
             stack_state(st_s, 2, 0), stack_state(st_s, 2, 1)]
    return tuple(outs)
```

```python
import functools

import jax
import jax.numpy as jnp
import numpy as np
from jax import lax
from jax.experimental import pallas as pl
from jax.experimental.pallas import tpu as pltpu

D_MODEL = 1024
DEPTH = 4
N_MIXERS = 3
HEAD_DIM = 64
RMS_EPS = 1e-6
N_HEADS = D_MODEL // HEAD_DIM
A_GROUPS = ((128, 1), (512, 4), (2048, 16))
A_WIDTH = D_MODEL
B_KV_HEADS = N_HEADS // 4
B_KV_WIDTH = B_KV_HEADS * HEAD_DIM
B_BLOCK = 64
B_TOPK = 16
B_WINDOW = 512
B_QBLOCK = 64
C_QBLOCK = 128
PAGE_SIZE = 128

VMEM_LIMIT = 56 * 1024 * 1024

F32 = jnp.float32
BF16 = jnp.bfloat16


def _sigmoid(x):
    return 1.0 / (1.0 + jnp.exp(-x))


def _ada_kernel(c_ref, w_ref, b_ref, o_ref):
    c = c_ref[...]
    s = c * _sigmoid(c)
    o_ref[...] = jnp.dot(s, w_ref[...], preferred_element_type=F32,
                         precision=lax.Precision.HIGHEST) + b_ref[...]


def ada_modulation(c_all, ada_w, ada_b):
    nb = c_all.shape[0]
    tn = 1024
    return pl.pallas_call(
        _ada_kernel,
        out_shape=jax.ShapeDtypeStruct((DEPTH, nb, 3 * D_MODEL), F32),
        grid=(DEPTH, 3 * D_MODEL // tn),
        in_specs=[
            pl.BlockSpec((nb, D_MODEL), lambda i, n: (0, 0)),
            pl.BlockSpec((None, D_MODEL, tn), lambda i, n: (i, 0, n)),
            pl.BlockSpec((None, 1, tn), lambda i, n: (i, 0, n)),
        ],
        out_specs=pl.BlockSpec((None, nb, tn), lambda i, n: (i, 0, n)),
        compiler_params=pltpu.CompilerParams(vmem_limit_bytes=VMEM_LIMIT),
        name="ada_modulation",
    )(c_all, ada_w, ada_b.reshape(DEPTH, 1, 3 * D_MODEL))


def _inproj_kernel(x_ref, shift_ref, scale_ref, g_ref, w_ref, o_ref, h_ref):
    bt, tt, d = x_ref.shape

    @pl.when(pl.program_id(1) == 0)
    def _():
        x = x_ref[...]
        ms = jnp.mean(x * x, axis=-1, keepdims=True)
        y = x * lax.rsqrt(ms + RMS_EPS) * g_ref[...]
        h = y * (1.0 + scale_ref[...]) + shift_ref[...]
        h_ref[...] = h.reshape(bt * tt, d).astype(BF16)

    o_ref[...] = jnp.dot(h_ref[...], w_ref[...],
                         preferred_element_type=F32).reshape(o_ref.shape)


def in_projection(x, mod, g, w_bf16, *, bt, tt, tn):
    b, t, d = x.shape
    e = w_bf16.shape[1]
    assert b % bt == 0 and t % tt == 0 and e % tn == 0
    nt = t // tt
    return pl.pallas_call(
        _inproj_kernel,
        out_shape=jax.ShapeDtypeStruct((b, t, e), F32),
        grid=(b // bt * nt, e // tn),
        in_specs=[
            pl.BlockSpec((bt, tt, d), lambda m, n: (m // nt, m % nt, 0)),
            pl.BlockSpec((bt, 1, d), lambda m, n: (m // nt, 0, 0)),
            pl.BlockSpec((bt, 1, d), lambda m, n: (m // nt, 0, 1)),
            pl.BlockSpec((1, d), lambda m, n: (0, 0)),
            pl.BlockSpec((d, tn), lambda m, n: (0, n)),
        ],
        out_specs=pl.BlockSpec((bt, tt, tn), lambda m, n: (m // nt, m % nt, n)),
        scratch_shapes=[pltpu.VMEM((bt * tt, d), BF16)],
        compiler_params=pltpu.CompilerParams(
            dimension_semantics=("arbitrary", "arbitrary"),
            vmem_limit_bytes=VMEM_LIMIT),
        name="in_projection",
    )(x, mod, mod, g.reshape(1, d), w_bf16)


def _outproj_kernel(o_ref, gate_ref, x_ref, mg_ref, g_ref, w_ref, out_ref):
    bt, tt, d = x_ref.shape
    gt = gate_ref[...]
    og = (o_ref[...] * (gt * _sigmoid(gt))).reshape(bt * tt, d).astype(BF16)
    y = jnp.dot(og, w_ref[...], preferred_element_type=F32)
    ms = jnp.mean(y * y, axis=-1, keepdims=True)
    yn = (y * lax.rsqrt(ms + RMS_EPS) * g_ref[...]).reshape(bt, tt, d)
    out_ref[...] = x_ref[...] + mg_ref[...] * yn


def out_projection(o, proj, gate_col, x, mod, g, w_bf16, *, bt, tt):
    b, t, d = x.shape
    nt = t // tt
    return pl.pallas_call(
        _outproj_kernel,
        out_shape=jax.ShapeDtypeStruct((b, t, d), F32),
        grid=(b // bt * nt,),
        in_specs=[
            pl.BlockSpec((bt, tt, d), lambda m: (m // nt, m % nt, 0)),
            pl.BlockSpec((bt, tt, d), lambda m: (m // nt, m % nt, gate_col)),
            pl.BlockSpec((bt, tt, d), lambda m: (m // nt, m % nt, 0)),
            pl.BlockSpec((bt, 1, d), lambda m: (m // nt, 0, 2)),
            pl.BlockSpec((1, d), lambda m: (0, 0)),
            pl.BlockSpec((d, d), lambda m: (0, 0)),
        ],
        out_specs=pl.BlockSpec((bt, tt, d), lambda m: (m // nt, m % nt, 0)),
        compiler_params=pltpu.CompilerParams(
            dimension_semantics=("arbitrary",),
            vmem_limit_bytes=VMEM_LIMIT),
        name="out_projection",
    )(o, proj, x, mod, g.reshape(1, d), w_bf16)


MXU = BF16
NEG = -1e30
HIGHEST = lax.Precision.HIGHEST


def _slope(h):
    return float(np.float32(2.0 ** (-8.0 * (h + 1) / N_HEADS)))


def _dot_nt(a, b):
    return lax.dot_general(a, b, (((1,), (1,)), ((), ())), preferred_element_type=F32)


def _dot_tn(a, b):
    return lax.dot_general(a, b, (((0,), (0,)), ((), ())), preferred_element_type=F32)


def _rowmax(s):
    m = s[:, 0:128]
    for c in range(1, s.shape[1] // 128):
        m = jnp.maximum(m, s[:, c * 128:(c + 1) * 128])
    return jnp.max(m, axis=1, keepdims=True)


def _flash_rows(q4, slope4, pos4, k_b, v_b, j_lo, j_hi, mask_fn, m_ref, l_ref, acc_ref, tk=128):
    m_ref[...] = jnp.full(m_ref.shape, NEG, F32)
    l_ref[...] = jnp.zeros(l_ref.shape, F32)
    acc_ref[...] = jnp.zeros(acc_ref.shape, F32)

    def body(j, c):
        off = pl.multiple_of(j * tk, tk)
        kt = k_b[pl.ds(off, tk), :]
        vt = v_b[pl.ds(off, tk), :]
        s = _dot_nt(q4, kt)
        kpos = off + lax.broadcasted_iota(jnp.int32, (1, tk), 1)
        dist = pos4 - kpos
        s = s - slope4 * dist.astype(F32)
        valid = mask_fn(dist, off)
        sm = jnp.where(valid, s, NEG)
        m_old = m_ref[...]
        m_new = jnp.maximum(m_old, jnp.max(sm, axis=1, keepdims=True))
        p = jnp.where(valid, jnp.exp(sm - m_new), 0.0)
        alpha = jnp.exp(m_old - m_new)
        l_ref[...] = alpha * l_ref[...] + jnp.sum(p, axis=1, keepdims=True)
        acc_ref[...] = alpha * acc_ref[...] + jnp.dot(p.astype(MXU), vt, preferred_element_type=F32)
        m_ref[...] = m_new
        return c

    lax.fori_loop(j_lo, j_hi, body, 0)
    return acc_ref[...] / l_ref[...]


B_E = 4096
B_KV_OFF = 2048
B_BG_OFF = 3584


def _b_in_perm():
    perm = np.zeros(3632, np.int64)
    for r in range(4):
        for g in range(4):
            h = 4 * g + r
            for d in range(64):
                perm[r * 256 + g * 64 + d] = h * 64 + d
                perm[1024 + r * 256 + g * 64 + d] = 2608 + h * 64 + d
    perm[2048:3584] = 1024 + np.arange(1536)
    for br in range(3):
        for r in range(4):
            for g in range(4):
                perm[3584 + br * 16 + r * 4 + g] = 2560 + (4 * g + r) * 3 + br
    return perm


def _b_out_perm():
    perm = np.zeros(1024, np.int64)
    for r in range(4):
        for g in range(4):
            for d in range(64):
                perm[r * 256 + g * 64 + d] = (4 * g + r) * 64 + d
    return perm


def _nsa_cmp_kernel(x_ref, pe_ref, w_ref, o_ref):
    tt = x_ref.shape[0]
    nb = tt // B_BLOCK
    x = x_ref[...].reshape(nb, B_BLOCK, 512) + pe_ref[...][None]
    pooled = jnp.sum(x, axis=1) * (1.0 / B_BLOCK)
    o_ref[...] = jnp.dot(pooled, w_ref[...], preferred_element_type=F32, precision=HIGHEST)


def _nsa_cmp_weights(pos_emb, w_cmp):
    pe = pos_emb.transpose(1, 0, 2, 3).reshape(B_BLOCK, 512)
    wbd = jnp.zeros((512, 512), F32)
    for c in range(2):
        for g in range(4):
            o = c * 256 + g * 64
            wbd = wbd.at[o:o + 64, o:o + 64].set(w_cmp[c, g])
    return pe, wbd


def nsa_compress_prompt(proj, pe, wbd, *, tt=512):
    b, t, _ = proj.shape
    return pl.pallas_call(
        _nsa_cmp_kernel,
        out_shape=jax.ShapeDtypeStruct((b, t // B_BLOCK, 512), F32),
        grid=(b, t // tt),
        in_specs=[
            pl.BlockSpec((None, tt, 512), lambda i, j: (i, j, B_KV_OFF // 512)),
            pl.BlockSpec((B_BLOCK, 512), lambda i, j: (0, 0)),
            pl.BlockSpec((512, 512), lambda i, j: (0, 0)),
        ],
        out_specs=pl.BlockSpec((None, tt // B_BLOCK, 512), lambda i, j: (i, j, 0)),
        compiler_params=pltpu.CompilerParams(vmem_limit_bytes=VMEM_LIMIT),
        name="nsa_compress_prompt",
    )(proj, pe, wbd)


NSA_TK = 512


def _nsa_prompt_kernel(q_ref, ks_ref, vs_ref, kw_ref, vw_ref, cmp_ref, bg_ref, ex_ref, o_ref,
                       ksb, kwb, vsg, vwg, q4_ref, sel_ref, wmask_ref, oc_ref, m_ref, acc_ref, *, tq, t):
    qi = pl.program_id(1)
    nblk = t // B_BLOCK
    tk = NSA_TK
    q0 = qi * tq
    lane = lax.broadcasted_iota(jnp.int32, (1, 256), 1)
    gmasks = [(lane >= 64 * g) & (lane < 64 * (g + 1)) for g in range(4)]

    @pl.when(qi == 0)
    def _():
        ksb[...] = ks_ref[...].astype(MXU)
        kwb[...] = kw_ref[...].astype(MXU)
        ones = jnp.ones((t, 64), F32)
        for g in range(4):
            vsg[g] = jnp.concatenate([vs_ref[:, 64 * g:64 * (g + 1)], ones], axis=1).astype(MXU)
            vwg[g] = jnp.concatenate([vw_ref[:, 64 * g:64 * (g + 1)], ones], axis=1).astype(MXU)

    posq = q0 + lax.broadcasted_iota(jnp.int32, (1, tq), 1)
    posc = q0 + lax.broadcasted_iota(jnp.int32, (tq, 1), 0)
    bidx = lax.broadcasted_iota(jnp.int32, (nblk, 1), 0)
    blk_end = (bidx + 1) * B_BLOCK - 1
    cur = posq // B_BLOCK
    forced = (bidx == 0) | (bidx >= cur - 1)
    kc = cmp_ref[:, 0:256].astype(MXU)
    vc = cmp_ref[:, 256:512].astype(MXU)
    distc = (posq - blk_end).astype(F32)
    validc = blk_end <= posq

    for g in range(4):
        q4 = jnp.concatenate(
            [jnp.where(gmasks[g], q_ref[:, r * 256:(r + 1) * 256], 0.0) for r in range(4)], axis=0)
        q4_ref[g] = (q4 * (HEAD_DIM ** -0.5)).astype(MXU)

        s_ct = _dot_nt(kc, q4_ref[g])
        p_parts = []
        for r in range(4):
            s_r = s_ct[:, r * tq:(r + 1) * tq] - _slope(4 * g + r) * distc
            sm = jnp.where(validc, s_r, -jnp.inf)
            mx = jnp.max(sm, axis=0, keepdims=True)
            mx = jnp.where(mx > -jnp.inf, mx, 0.0)
            e = jnp.exp(sm - mx)
            den = jnp.sum(e, axis=0, keepdims=True)
            p_parts.append(e / jnp.where(den > 0, den, 1.0))
        oc_ref[g] = _dot_tn(jnp.concatenate(p_parts, axis=1).astype(MXU), vc)

        sc = p_parts[0] + p_parts[1] + p_parts[2] + p_parts[3]
        score = jnp.where(bidx > cur, -jnp.inf, jnp.where(forced, jnp.inf, sc))
        rank = jnp.zeros((nblk, tq), F32)
        for n in range(nblk):
            row = score[n:n + 1, :]
            beats = (row > score) | ((row == score) & (bidx > n))
            rank = rank + jnp.where(beats, 1.0, 0.0)
        sel_ref[g] = jnp.where(rank < float(min(B_TOPK, nblk)), 1.0, 0.0).T.astype(MXU)

    def tile(off, k_b, vg_ref, mask_fn):
        kt = k_b[pl.ds(off, tk), :]
        kq = (off - q0 + lax.broadcasted_iota(jnp.int32, (1, tk), 1)).astype(F32)
        ss = []
        for g in range(4):
            s = _dot_nt(q4_ref[g], kt)
            madd = mask_fn(g)
            ss.append(jnp.concatenate(
                [s[r * tq:(r + 1) * tq] + _slope(4 * g + r) * kq + madd for r in range(4)], axis=0))
        m_olds = [m_ref[g] for g in range(4)]
        m_news = [jnp.maximum(m_olds[g], _rowmax(ss[g])) for g in range(4)]
        ps = [jnp.exp(ss[g] - m_news[g]).astype(MXU) for g in range(4)]
        pvs = [jnp.dot(ps[g], vg_ref[g, pl.ds(off, tk), :], preferred_element_type=F32) for g in range(4)]
        for g in range(4):
            acc_ref[g] = jnp.exp(m_olds[g] - m_news[g]) * acc_ref[g] + pvs[g]
            m_ref[g] = m_news[g]

    def branch(k_b, vg_ref, j_lo, j_hi, mask_fn, last_mask_fn):
        m_ref[...] = jnp.full(m_ref.shape, NEG, F32)
        acc_ref[...] = jnp.zeros(acc_ref.shape, F32)

        def body(j, c):
            off = pl.multiple_of(j * tk, tk)
            tile(off, k_b, vg_ref, lambda g: mask_fn(g, off, j))
            return c

        lax.fori_loop(j_lo, j_hi, body, 0)
        off = pl.multiple_of(j_hi * tk, tk)
        tile(off, k_b, vg_ref, lambda g: last_mask_fn(g, off, j_hi))
        return [acc_ref[g][:, 0:64] / acc_ref[g][:, 64:65] for g in range(4)]

    j_hi = (q0 + tq - 1) // tk
    kcol = lax.broadcasted_iota(jnp.int32, (1, tk), 1)

    def sel_madd(g, off, j):
        sel_keys = jnp.dot(sel_ref[g], ex_ref[:, pl.ds(off, tk)], preferred_element_type=F32)
        return (sel_keys - 1.0) * (-NEG)

    def sel_madd_last(g, off, j):
        return sel_madd(g, off, j) + jnp.where(off + kcol <= posc, 0.0, NEG)

    o_s = branch(ksb, vsg, 0, j_hi, sel_madd, sel_madd_last)

    j_lo = jnp.maximum(q0 - B_WINDOW, 0) // tk
    for c in range(wmask_ref.shape[1] // tk):
        dist = posc - ((j_lo + c) * tk + kcol)
        wmask_ref[:, c * tk:(c + 1) * tk] = jnp.where((dist >= 0) & (dist <= B_WINDOW), 0.0, NEG)

    def win_madd(g, off, j):
        return wmask_ref[:, pl.ds(pl.multiple_of((j - j_lo) * tk, tk), tk)]

    o_w = branch(kwb, vwg, j_lo, j_hi, win_madd, win_madd)

    bgate = _sigmoid(bg_ref[...])
    for r in range(4):
        rs = slice(r * tq, (r + 1) * tq)
        parts = []
        for g in range(4):
            c0 = r * 4 + g
            parts.append(bgate[:, c0:c0 + 1] * oc_ref[g][rs, 64 * g:64 * (g + 1)]
                         + bgate[:, 16 + c0:17 + c0] * o_s[g][rs] + bgate[:, 32 + c0:33 + c0] * o_w[g][rs])
        o_ref[:, r * 256:(r + 1) * 256] = jnp.concatenate(parts, axis=1)


def _nsa_expand(t):
    nblk = t // B_BLOCK
    ex = (np.arange(t)[None, :] // B_BLOCK == np.arange(nblk)[:, None])
    return jnp.asarray(ex, MXU)


def nsa_attention_prompt(proj, cmp, *, tq=128):
    b, t, _ = proj.shape
    assert tq == 128 and t % tq == 0
    kv = lambda c: pl.BlockSpec((None, t, 256), lambda i, j, c=c: (i, 0, B_KV_OFF // 256 + c),
                                pipeline_mode=pl.Buffered(1))
    return pl.pallas_call(
        functools.partial(_nsa_prompt_kernel, tq=tq, t=t),
        out_shape=jax.ShapeDtypeStruct((b, t, D_MODEL), F32),
        grid=(b, t // tq),
        in_specs=[
            pl.BlockSpec((None, tq, 1024), lambda i, j: (i, j, 0)),
            kv(2), kv(3), kv(4), kv(5),
            pl.BlockSpec((None, t // B_BLOCK, 512), lambda i, j: (i, 0, 0)),
            pl.BlockSpec((None, tq, 128), lambda i, j: (i, j, B_BG_OFF // 128)),
            pl.BlockSpec((t // B_BLOCK, t), lambda i, j: (0, 0)),
        ],
        out_specs=pl.BlockSpec((None, tq, D_MODEL), lambda i, j: (i, j, 0)),
        scratch_shapes=[
            pltpu.VMEM((t, 256), MXU), pltpu.VMEM((t, 256), MXU),
            pltpu.VMEM((4, t, 128), MXU), pltpu.VMEM((4, t, 128), MXU),
            pltpu.VMEM((4, 4 * tq, 256), MXU),
            pltpu.VMEM((4, tq, t // B_BLOCK), MXU),
            pltpu.VMEM((tq, -(-(B_WINDOW + tq + NSA_TK - 128) // NSA_TK) * NSA_TK), F32),
            pltpu.VMEM((4, 4 * tq, 256), F32),
            pltpu.VMEM((4, 4 * tq, 1), F32),
            pltpu.VMEM((4, 4 * tq, 128), F32)],
        compiler_params=pltpu.CompilerParams(
            dimension_semantics=("arbitrary", "arbitrary"),
            vmem_limit_bytes=VMEM_LIMIT),
        name="nsa_attention_prompt",
    )(proj, proj, proj, proj, proj, cmp, proj, _nsa_expand(t))


def mixer_b_prompt(proj, pos_emb, w_cmp):
    b, t, _ = proj.shape
    pe, wbd = _nsa_cmp_weights(pos_emb, w_cmp)
    cmp = nsa_compress_prompt(proj, pe, wbd)
    o = nsa_attention_prompt(proj, cmp)
    kv_full_new = proj[..., B_KV_OFF:B_KV_OFF + 1024].reshape(b, t, 4, B_KV_HEADS, HEAD_DIM)
    new_win = proj[:, t - min(B_WINDOW, t):, B_KV_OFF + 1024:B_KV_OFF + 1536].reshape(
        b, min(B_WINDOW, t), 2, B_KV_HEADS, HEAD_DIM)
    return o, [kv_full_new, new_win]


C_E = 4224
C_F_OFF = 4096


def _log_sigmoid(x):
    return jnp.minimum(x, 0.0) - jnp.log(1.0 + jnp.exp(-jnp.abs(x)))


def _fox_prep_kernel(x_ref, bf_ref, logft_ref, cumt_ref, carry_ref):
    @pl.when(pl.program_id(1) == 0)
    def _():
        carry_ref[...] = jnp.zeros(carry_ref.shape, F32)

    n = x_ref.shape[0]
    logf = _log_sigmoid(x_ref[...] + bf_ref[...])
    tri = jnp.where(lax.broadcasted_iota(jnp.int32, (n, n), 1) <= lax.broadcasted_iota(jnp.int32, (n, n), 0),
                    1.0, 0.0)
    cum = jnp.dot(tri, logf, preferred_element_type=F32, precision=HIGHEST) + carry_ref[...]
    logft_ref[...] = logf.T[0:N_HEADS, :]
    cumt_ref[...] = cum.T[0:N_HEADS, :]
    carry_ref[...] = cum[n - 1:n, :]


def fox_prep(proj, b_forget, *, tt=128):
    b, t, _ = proj.shape
    bf = jnp.pad(b_forget.astype(F32), (0, 128 - N_HEADS)).reshape(1, 128)
    return pl.pallas_call(
        _fox_prep_kernel,
        out_shape=[jax.ShapeDtypeStruct((b, N_HEADS, t), F32), jax.ShapeDtypeStruct((b, N_HEADS, t), F32)],
        grid=(b, t // tt),
        in_specs=[pl.BlockSpec((None, tt, 128), lambda i, j: (i, j, C_F_OFF // 128)),
                  pl.BlockSpec((1, 128), lambda i, j: (0, 0))],
        out_specs=[pl.BlockSpec((None, N_HEADS, tt), lambda i, j: (i, 0, j)),
                   pl.BlockSpec((None, N_HEADS, tt), lambda i, j: (i, 0, j))],
        scratch_shapes=[pltpu.VMEM((1, 128), F32)],
        compiler_params=pltpu.CompilerParams(
            dimension_semantics=("arbitrary", "arbitrary"), vmem_limit_bytes=VMEM_LIMIT),
        name="fox_prep",
    )(proj, bf)


def _fox_prompt_kernel(q_ref, k_ref, v_ref, ck_ref, o_ref, kb, vb, qm_ref, m_ref, acc_ref, *, tq):
    hp = pl.program_id(1)
    qi = pl.program_id(2)

    lane = lax.broadcasted_iota(jnp.int32, (1, 128), 1)
    hmasks = [(lane >= 64 * hh) & (lane < 64 * (hh + 1)) for hh in range(2)]

    @pl.when(qi == 0)
    def _():
        kb[...] = k_ref[...].astype(MXU)
        for hh in range(2):
            vb[hh] = jnp.where(hmasks[hh], v_ref[...], 1.0).astype(MXU)

    for hh in range(2):
        qm_ref[hh] = (jnp.where(hmasks[hh], q_ref[...], 0.0) * (HEAD_DIM ** -0.5)).astype(MXU)
    m_ref[...] = jnp.full(m_ref.shape, NEG, F32)
    acc_ref[...] = jnp.zeros(acc_ref.shape, F32)

    def tile(j, diag):
        off = pl.multiple_of(j * tq, tq)
        kt = kb[pl.ds(off, tq), :]
        for hh in range(2):
            ck = ck_ref[pl.ds(2 * hp + hh, 1), pl.ds(off, tq)]
            s = _dot_nt(qm_ref[hh], kt) - ck
            if diag:
                causal = (lax.broadcasted_iota(jnp.int32, (1, tq), 1)
                          <= lax.broadcasted_iota(jnp.int32, (tq, 1), 0))
                s = jnp.where(causal, s, NEG)
            m_old = m_ref[hh]
            m_new = jnp.maximum(m_old, _rowmax(s))
            p = jnp.exp(s - m_new)
            acc_ref[hh] = jnp.exp(m_old - m_new) * acc_ref[hh] + jnp.dot(
                p.astype(MXU), vb[hh, pl.ds(off, tq), :], preferred_element_type=F32)
            m_ref[hh] = m_new

    def body(j, c):
        tile(j, False)
        return c

    lax.fori_loop(0, qi, body, 0)
    tile(qi, True)
    a0, a1 = acc_ref[0], acc_ref[1]
    o_ref[...] = jnp.where(lane < 64, a0 / a0[:, 64:65], a1 / a1[:, 0:1])


def fox_attention_prompt(proj, cumt, *, tq=512):
    b, t, _ = proj.shape
    tq = min(tq, t)
    return pl.pallas_call(
        functools.partial(_fox_prompt_kernel, tq=tq),
        out_shape=jax.ShapeDtypeStruct((b, t, D_MODEL), F32),
        grid=(b, N_HEADS // 2, t // tq),
        in_specs=[
            pl.BlockSpec((None, tq, 128), lambda i, p, j: (i, j, p)),
            pl.BlockSpec((None, t, 128), lambda i, p, j: (i, 0, 8 + p)),
            pl.BlockSpec((None, t, 128), lambda i, p, j: (i, 0, 16 + p)),
            pl.BlockSpec((None, N_HEADS, t), lambda i, p, j: (i, 0, 0)),
        ],
        out_specs=pl.BlockSpec((None, tq, 128), lambda i, p, j: (i, j, p)),
        scratch_shapes=[pltpu.VMEM((t, 128), MXU), pltpu.VMEM((2, t, 128), MXU),
                        pltpu.VMEM((2, tq, 128), MXU),
                        pltpu.VMEM((2, tq, 1), F32), pltpu.VMEM((2, tq, 128), F32)],
        compiler_params=pltpu.CompilerParams(
            dimension_semantics=("arbitrary", "arbitrary", "arbitrary"), vmem_limit_bytes=VMEM_LIMIT),
        name="fox_attention_prompt",
    )(proj, proj, proj, cumt)


def mixer_c_prompt(proj, b_forget):
    b, t, _ = proj.shape
    logft, cumt = fox_prep(proj, b_forget)
    o = fox_attention_prompt(proj, cumt)
    kv_new = proj[..., 1024:3072].reshape(b, t, 2, N_HEADS, HEAD_DIM)
    return o, [kv_new, logft.transpose(0, 2, 1)]


A_E = 10240
A_NK = 128


def _dil_prompt_kernel(*refs, first, last, dil):
    if first:
        q_ref, kc_ref, kp_ref, vc_ref, vp_ref = refs[:5]
        outs = refs[5:]
    else:
        q_ref, kc_ref, kp_ref, vc_ref, vp_ref, m_in, l_in, acc_in = refs[:8]
        outs = refs[8:]
    if last:
        (acc_out,) = outs
    else:
        m_out, l_out, acc_out = outs
    i = pl.program_id(2)
    nk = A_NK
    step = (nk + lax.broadcasted_iota(jnp.int32, (nk, 1), 0)) - lax.broadcasted_iota(jnp.int32, (1, 2 * nk), 1)
    kcol = lax.broadcasted_iota(jnp.int32, (1, 2 * nk), 1)
    valid = (step >= 0) & (step <= nk) & ((kcol >= nk) | (i > 0))
    distf = (step * dil).astype(F32)
    lane = lax.broadcasted_iota(jnp.int32, (1, 128), 1)
    if not last:
        m_out[...] = jnp.zeros(m_out.shape, F32)
        l_out[...] = jnp.zeros(l_out.shape, F32)
    for hp in range(N_HEADS // 2):
        cs = slice(128 * hp, 128 * (hp + 1))
        q2 = q_ref[:, cs]
        k2 = jnp.concatenate([kp_ref[:, cs], kc_ref[:, cs]], axis=0).astype(MXU)
        v2 = jnp.concatenate([vp_ref[:, cs], vc_ref[:, cs]], axis=0).astype(MXU)
        res = []
        for hh in range(2):
            h = 2 * hp + hh
            hmask = (lane >= 64 * hh) & (lane < 64 * (hh + 1))
            qm = (jnp.where(hmask, q2, 0.0) * (HEAD_DIM ** -0.5)).astype(MXU)
            s = _dot_nt(qm, k2) - _slope(h) * distf
            sm = jnp.where(valid, s, NEG)
            mx = jnp.max(sm, axis=1, keepdims=True)
            if first:
                m_new = mx
            else:
                m_old = m_in[:, h:h + 1]
                m_new = jnp.maximum(m_old, mx)
            p = jnp.where(valid, jnp.exp(sm - m_new), 0.0)
            l_new = jnp.sum(p, axis=1, keepdims=True)
            acc = jnp.dot(p.astype(MXU), v2, preferred_element_type=F32)
            if not first:
                alpha = jnp.exp(m_old - m_new)
                l_new = alpha * l_in[:, h:h + 1] + l_new
                acc = alpha * acc_in[:, cs] + acc
            if last:
                acc = acc / l_new
            else:
                m_out[:, h:h + 1] = m_new
                l_out[:, h:h + 1] = l_new
            res.append(acc)
        acc_out[:, cs] = jnp.where(lane < 64, res[0], res[1])


def dilated_attention_prompt(proj):
    b, t, _ = proj.shape
    nk = A_NK
    state = None
    for g, (window, dil) in enumerate(A_GROUPS):
        assert window // dil == nk and t % (dil * nk) == 0
        ln = t // dil
        first, last = g == 0, g == len(A_GROUPS) - 1
        pv = proj.reshape(b, ln, dil * A_E)
        col = lambda c, prev: pl.BlockSpec(
            (None, nk, 1024),
            (lambda bi, r, i, c=c: (bi, jnp.maximum(i - 1, 0), r * (A_E // 1024) + c)) if prev else
            (lambda bi, r, i, c=c: (bi, i, r * (A_E // 1024) + c)))
        st_spec = pl.BlockSpec((None, nk, 128), lambda bi, r, i: (bi, i, r))
        acc_spec = pl.BlockSpec((None, nk, 1024), lambda bi, r, i: (bi, i, r))
        in_specs = [col(3 * g, False), col(3 * g + 1, False), col(3 * g + 1, True),
                    col(3 * g + 2, False), col(3 * g + 2, True)]
        args = [pv] * 5
        if not first:
            in_specs += [st_spec, st_spec, acc_spec]
            args += [state[0].reshape(b, ln, dil * 128), state[1].reshape(b, ln, dil * 128),
                     state[2].reshape(b, ln, dil * 1024)]
        acc_shape = jax.ShapeDtypeStruct((b, ln, dil * 1024), F32)
        st_shape = jax.ShapeDtypeStruct((b, ln, dil * 128), F32)
        out = pl.pallas_call(
            functools.partial(_dil_prompt_kernel, first=first, last=last, dil=dil),
            out_shape=[acc_shape] if last else [st_shape, st_shape, acc_shape],
            grid=(b, dil, ln // nk),
            in_specs=in_specs,
            out_specs=[acc_spec] if last else [st_spec, st_spec, acc_spec],
            compiler_params=pltpu.CompilerParams(
                dimension_semantics=("arbitrary", "arbitrary", "arbitrary"), vmem_limit_bytes=VMEM_LIMIT),
            name=f"dilated_attention_prompt_g{g}",
        )(*args)
        state = [o.reshape(b, t, -1) for o in out]
    return state[0]


A_CHUNK = 2048
A_UNROLL = 2


def _dil_prompt_fused_kernel(slope_ref, *refs):
    groups = [refs[5 * g:5 * g + 5] for g in range(3)]
    o_ref, kcat, vcat, bias_ref, m0_ref, m1_ref, acc0_ref, acc1_ref = refs[15:]
    m_refs, acc_refs = (m0_ref, m1_ref), (acc0_ref, acc1_ref)
    ci = pl.program_id(2)
    nk = A_NK
    ch = o_ref.shape[0]
    lane = lax.broadcasted_iota(jnp.int32, (1, 128), 1)
    hmasks = [(lane >= 64 * hh) & (lane < 64 * (hh + 1)) for hh in range(2)]
    for hh in range(2):
        m_refs[hh][...] = jnp.full(m_refs[hh].shape, NEG, F32)
        acc_refs[hh][...] = jnp.zeros(acc_refs[hh].shape, F32)
    step = (nk + lax.broadcasted_iota(jnp.int32, (nk, 1), 0)) - lax.broadcasted_iota(jnp.int32, (1, 2 * nk), 1)
    kcol = lax.broadcasted_iota(jnp.int32, (1, 2 * nk), 1)
    in_band = (step >= 0) & (step <= nk)
    slopes = slope_ref[...]
    for g, (q_ref, kc_ref, kp_ref, vc_ref, vp_ref) in enumerate(groups):
        _, dil = A_GROUPS[g]
        pr = nk * dil
        kcat[0:pr, :] = kp_ref[...]
        kcat[pr:pr + ch, :] = kc_ref[...]
        vcat[0:pr, :] = vp_ref[...]
        vcat[pr:pr + ch, :] = vc_ref[...]
        distf = (step * dil).astype(F32)
        for hh in range(2):
            bias_ref[hh] = slopes[:, 64 * hh:64 * hh + 1] * distf

        def body(tj, c, q_ref=q_ref, dil=dil, pr=pr):
            work = []
            for u in range(A_UNROLL):
                ti = tj * A_UNROLL + u
                r = ti % dil
                n = ti // dil
                start = pr * n + r
                qs = pl.ds(start, nk, stride=dil)
                ks = pl.ds(start, 2 * nk, stride=dil)
                q2 = q_ref[qs, :]
                k2 = kcat[ks, :].astype(MXU)
                v2 = vcat[ks, :]
                valid = in_band & ((kcol >= nk) | (n > 0) | (ci > 0))
                for hh in range(2):
                    qm = (jnp.where(hmasks[hh], q2, 0.0) * (HEAD_DIM ** -0.5)).astype(MXU)
                    sm = jnp.where(valid, _dot_nt(qm, k2) - bias_ref[hh], NEG)
                    work.append((hh, qs, sm, v2))
            m_olds = [m_refs[hh][qs, :] for hh, qs, _, _ in work]
            m_news = [jnp.maximum(mo, _rowmax(sm)) for mo, (_, _, sm, _) in zip(m_olds, work)]
            ps = [jnp.exp(sm - mn).astype(MXU) for mn, (_, _, sm, _) in zip(m_news, work)]
            pvs = [jnp.dot(p, jnp.where(hmasks[hh], v2, 1.0).astype(MXU), preferred_element_type=F32)
                   for p, (hh, _, _, v2) in zip(ps, work)]
            for mo, mn, pv, (hh, qs, _, _) in zip(m_olds, m_news, pvs, work):
                acc_refs[hh][qs, :] = jnp.exp(mo - mn) * acc_refs[hh][qs, :] + pv
                m_refs[hh][qs, :] = mn
            return c

        lax.fori_loop(0, ch // nk // A_UNROLL, body, 0)
    a0, a1 = acc0_ref[...], acc1_ref[...]
    o_ref[...] = jnp.where(lane < 64, a0 / a0[:, 64:65], a1 / a1[:, 0:1])


def dilated_attention_prompt(proj):
    b, t, _ = proj.shape
    nk = A_NK
    ch = min(A_CHUNK, t)
    assert t % ch == 0 and all(w // d == nk and ch % (nk * d) == 0 for w, d in A_GROUPS)
    sl = np.float32(2.0 ** (-8.0 * np.arange(1, N_HEADS + 1) / N_HEADS))
    slope_rows = jnp.asarray(np.repeat(sl, 64).reshape(N_HEADS // 2, 1, 128), F32)
    in_specs = [pl.BlockSpec((None, 1, 128), lambda bi, hp, ci: (hp, 0, 0))]
    for g, (_, dil) in enumerate(A_GROUPS):
        pr = nk * dil
        cur = lambda part, g=g: pl.BlockSpec(
            (None, ch, 128), lambda bi, hp, ci, part=part, g=g: (bi, ci, (3072 * g + 1024 * part) // 128 + hp))
        prev = lambda part, g=g, pr=pr: pl.BlockSpec(
            (None, pr, 128), lambda bi, hp, ci, part=part, g=g, pr=pr:
            (bi, jnp.maximum(ci * (ch // pr) - 1, 0), (3072 * g + 1024 * part) // 128 + hp))
        in_specs += [cur(0), cur(1), prev(1), cur(2), prev(2)]
    return pl.pallas_call(
        _dil_prompt_fused_kernel,
        out_shape=jax.ShapeDtypeStruct((b, t, D_MODEL), F32),
        grid=(b, N_HEADS // 2, t // ch),
        in_specs=in_specs,
        out_specs=pl.BlockSpec((None, ch, 128), lambda bi, hp, ci: (bi, ci, hp)),
        scratch_shapes=[pltpu.VMEM((nk * A_GROUPS[-1][1] + ch, 128), F32),
                        pltpu.VMEM((nk * A_GROUPS[-1][1] + ch, 128), F32),
                        pltpu.VMEM((2, nk, 2 * nk), F32),
                        pltpu.VMEM((ch, 1), F32), pltpu.VMEM((ch, 1), F32),
                        pltpu.VMEM((ch, 128), F32), pltpu.VMEM((ch, 128), F32)],
        compiler_params=pltpu.CompilerParams(
            dimension_semantics=("arbitrary", "arbitrary", "arbitrary"), vmem_limit_bytes=VMEM_LIMIT),
        name="dilated_attention_prompt",
    )(slope_rows, *([proj] * 15))


def mixer_a_prompt(proj):
    b, t, _ = proj.shape
    o = dilated_attention_prompt(proj)
    new_state = []
    for g, (window, _) in enumerate(A_GROUPS):
        w = min(window, t)
        new_state.append(proj[:, t - w:, 3072 * g + 1024:3072 * g + 3072].reshape(b, w, 2, N_HEADS, HEAD_DIM))
    return o, new_state


STEP_T = 8
STEP_LANES = N_HEADS * STEP_T


def _lane_iota():
    return lax.broadcasted_iota(jnp.int32, (1, STEP_LANES), 1)


def _row2col(x):
    n = x.shape[1]
    eye = lax.broadcasted_iota(jnp.int32, (n, n), 0) == lax.broadcasted_iota(jnp.int32, (n, n), 1)
    return jnp.sum(jnp.where(eye, x, 0.0), axis=1, keepdims=True)


def _step_state_init(m_ref, l_ref, acc_ref):
    m_ref[...] = jnp.full(m_ref.shape, NEG, F32)
    l_ref[...] = jnp.zeros(l_ref.shape, F32)
    acc_ref[...] = jnp.zeros(acc_ref.shape, F32)


def _step_update(s_t, valid, v_b, m_ref, l_ref, acc_ref):
    sm = s_t if valid is None else jnp.where(valid, s_t, NEG)
    m_old = m_ref[...]
    m_new = jnp.maximum(m_old, jnp.max(sm, axis=0, keepdims=True))
    p_t = jnp.exp(sm - m_new)
    if valid is not None:
        p_t = jnp.where(valid, p_t, 0.0)
    alpha = jnp.exp(m_old - m_new)
    l_ref[...] = alpha * l_ref[...] + jnp.sum(p_t, axis=0, keepdims=True)
    acc_ref[...] = _row2col(alpha) * acc_ref[...] + _dot_tn(p_t.astype(MXU), v_b)
    m_ref[...] = m_new


def _qbd_full(q):
    lane = _lane_iota()
    sel = jnp.where(lax.broadcasted_iota(jnp.int32, (STEP_T, 1), 0) == (lane & (STEP_T - 1)), 1.0, 0.0)
    qall = _dot_tn(q.astype(MXU), sel.astype(MXU))
    row_h = lax.broadcasted_iota(jnp.int32, (D_MODEL, 1), 0) // HEAD_DIM
    return (jnp.where(row_h == (lane >> 3), qall, 0.0) * (HEAD_DIM ** -0.5)).astype(MXU)


def _extract_heads(o):
    row_h = lax.broadcasted_iota(jnp.int32, (STEP_LANES, 1), 0) >> 3
    col_h = lax.broadcasted_iota(jnp.int32, (1, D_MODEL), 1) // HEAD_DIM
    om = jnp.where(row_h == col_h, o, 0.0)
    return jnp.sum(om.reshape(N_HEADS, STEP_T, D_MODEL), axis=0)


def _step_slopes():
    return jnp.asarray(np.repeat(np.float32(2.0 ** (-8.0 * np.arange(1, N_HEADS + 1) / N_HEADS)), STEP_T)
                       .reshape(1, STEP_LANES), F32)


A_STEP_TILE = 512


def _dil_step_kernel(slope_ref, new_ref, c2_ref, c1_ref, c0_ref, o_ref, qbd_ref, m_ref, l_ref, acc_ref):
    s = pl.program_id(1)
    n2 = A_GROUPS[2][0] // A_STEP_TILE
    ilane = _lane_iota() & (STEP_T - 1)
    slope = slope_ref[...]

    @pl.when(s == 0)
    def _():
        _step_state_init(m_ref, l_ref, acc_ref)
        for g in range(3):
            qbd_ref[g] = _qbd_full(new_ref[:, 3072 * g:3072 * g + 1024])

    def tile(k, v, g, row0):
        window, dil = A_GROUPS[g]
        rows = k.shape[0]
        s_t = jnp.dot(k.astype(MXU), qbd_ref[g], preferred_element_type=F32)
        kpos = row0 + lax.broadcasted_iota(jnp.int32, (rows, 1), 0)
        dist = (window + ilane) - kpos
        valid = (dist >= 0) & (dist <= window) & ((dist & (dil - 1)) == 0)
        _step_update(s_t - slope * dist.astype(F32), valid, v.astype(MXU), m_ref, l_ref, acc_ref)

    @pl.when(s < n2)
    def _():
        tile(c2_ref[:, 0:1024], c2_ref[:, 1024:2048], 2, s * A_STEP_TILE)

    @pl.when(s == n2)
    def _():
        tile(c1_ref[:, 0:1024], c1_ref[:, 1024:2048], 1, 0)

    @pl.when(s == n2 + 1)
    def _():
        tile(c0_ref[:, 0:1024], c0_ref[:, 1024:2048], 0, 0)
        for g in range(3):
            o = 3072 * g
            tile(new_ref[:, o + 1024:o + 2048], new_ref[:, o + 2048:o + 3072], g, A_GROUPS[g][0])
        o_ref[...] = _extract_heads(acc_ref[...] / _row2col(l_ref[...]))


def mixer_a_step(proj, cache):
    b, t, _ = proj.shape
    assert t == STEP_T and A_GROUPS[1][0] == A_STEP_TILE
    c0, c1, c2 = [c.reshape(b, c.shape[1], 2 * D_MODEL) for c in cache]
    n2 = A_GROUPS[2][0] // A_STEP_TILE
    o = pl.pallas_call(
        _dil_step_kernel,
        out_shape=jax.ShapeDtypeStruct((b, t, D_MODEL), F32),
        grid=(b, n2 + 2),
        in_specs=[
            pl.BlockSpec((1, STEP_LANES), lambda i, s: (0, 0)),
            pl.BlockSpec((None, t, A_E), lambda i, s: (i, 0, 0)),
            pl.BlockSpec((None, A_STEP_TILE, 2 * D_MODEL), lambda i, s: (i, jnp.minimum(s, n2 - 1), 0)),
            pl.BlockSpec((None, A_STEP_TILE, 2 * D_MODEL), lambda i, s: (i, 0, 0)),
            pl.BlockSpec((None, A_GROUPS[0][0], 2 * D_MODEL), lambda i, s: (i, 0, 0)),
        ],
        out_specs=pl.BlockSpec((None, t, D_MODEL), lambda i, s: (i, 0, 0)),
        scratch_shapes=[pltpu.VMEM((3, D_MODEL, STEP_LANES), MXU),
                        pltpu.VMEM((1, STEP_LANES), F32), pltpu.VMEM((1, STEP_LANES), F32),
                        pltpu.VMEM((STEP_LANES, D_MODEL), F32)],
        compiler_params=pltpu.CompilerParams(
            dimension_semantics=("arbitrary", "arbitrary"), vmem_limit_bytes=VMEM_LIMIT),
        name="dilated_attention_step",
    )(_step_slopes(), proj, c2, c1, c0)
    new_state = []
    for g in range(3):
        kv_new = proj[..., 3072 * g + 1024:3072 * g + 3072].reshape(b, t, 2, N_HEADS, HEAD_DIM)
        new_state.append(jnp.concatenate([cache[g][:, t:], kv_new], axis=1))
    return o, new_state


C_STEP_PAGES = 8


def _fox_step_kernel(pt_ref, new_ref, bf_ref, *refs):
    npg = C_STEP_PAGES
    kv_refs = refs[:npg]
    lf_refs = refs[npg:2 * npg]
    o_ref, lfo_ref, qbd_ref, m_ref, l_ref, acc_ref, carry_ref = refs[2 * npg:]
    s = pl.program_id(1)
    lane = _lane_iota()

    @pl.when(s == 0)
    def _():
        _step_state_init(m_ref, l_ref, acc_ref)
        carry_ref[...] = jnp.zeros(carry_ref.shape, F32)
        qbd_ref[...] = _qbd_full(new_ref[:, 0:1024])

    expand = jnp.where(lax.broadcasted_iota(jnp.int32, (128, 1), 0) == (lane >> 3), 1.0, 0.0)
    tri = jnp.where(lax.broadcasted_iota(jnp.int32, (PAGE_SIZE, PAGE_SIZE), 1)
                    <= lax.broadcasted_iota(jnp.int32, (PAGE_SIZE, PAGE_SIZE), 0), 1.0, 0.0)
    for p in range(npg):
        lfe = jnp.dot(lf_refs[p][...], expand[0:N_HEADS, :], preferred_element_type=F32, precision=HIGHEST)
        ck = jnp.dot(tri, lfe, preferred_element_type=F32, precision=HIGHEST) + carry_ref[...]
        carry_ref[...] = ck[PAGE_SIZE - 1:PAGE_SIZE, :]
        s_t = jnp.dot(kv_refs[p][:, 0:1024].astype(MXU), qbd_ref[...], preferred_element_type=F32) - ck
        _step_update(s_t, None, kv_refs[p][:, 1024:2048].astype(MXU), m_ref, l_ref, acc_ref)

    @pl.when(s == pl.num_programs(1) - 1)
    def _():
        logf = _log_sigmoid(new_ref[:, C_F_OFF:C_F_OFF + 128] + bf_ref[...])
        lfo_ref[...] = logf
        lfe = jnp.dot(logf, expand, preferred_element_type=F32, precision=HIGHEST)
        row = lax.broadcasted_iota(jnp.int32, (STEP_T, 1), 0)
        tri8 = jnp.where(lax.broadcasted_iota(jnp.int32, (STEP_T, STEP_T), 1) <= row, 1.0, 0.0)
        ck = jnp.dot(tri8, lfe, preferred_element_type=F32, precision=HIGHEST) + carry_ref[...]
        s_t = jnp.dot(new_ref[:, 1024:2048].astype(MXU), qbd_ref[...], preferred_element_type=F32) - ck
        valid = row <= (lane & (STEP_T - 1))
        _step_update(s_t, valid, new_ref[:, 2048:3072].astype(MXU), m_ref, l_ref, acc_ref)
        o_ref[...] = _extract_heads(acc_ref[...] / _row2col(l_ref[...]))


def mixer_c_step(proj, b_forget, kv_pool, lf_pool, page_table):
    b, t, _ = proj.shape
    n_pages = page_table.shape[1]
    npg = C_STEP_PAGES
    assert t == STEP_T and n_pages % npg == 0
    kvp = kv_pool.reshape(kv_pool.shape[0], PAGE_SIZE, 2 * D_MODEL)
    bf = jnp.pad(b_forget.astype(F32), (0, 128 - N_HEADS)).reshape(1, 128)
    page = lambda shape, p: pl.BlockSpec(shape, lambda i, s, pt, p=p: (pt[i, s * npg + p], 0, 0))
    o, logf = pl.pallas_call(
        _fox_step_kernel,
        out_shape=[jax.ShapeDtypeStruct((b, t, D_MODEL), F32), jax.ShapeDtypeStruct((b, t, 128), F32)],
        grid_spec=pltpu.PrefetchScalarGridSpec(
            num_scalar_prefetch=1,
            grid=(b, n_pages // npg),
            in_specs=[pl.BlockSpec((None, t, C_E), lambda i, s, pt: (i, 0, 0)),
                      pl.BlockSpec((1, 128), lambda i, s, pt: (0, 0))]
            + [page((None, PAGE_SIZE, 2 * D_MODEL), p) for p in range(npg)]
            + [page((None, PAGE_SIZE, N_HEADS), p) for p in range(npg)],
            out_specs=[pl.BlockSpec((None, t, D_MODEL), lambda i, s, pt: (i, 0, 0)),
                       pl.BlockSpec((None, t, 128), lambda i, s, pt: (i, 0, 0))],
            scratch_shapes=[pltpu.VMEM((D_MODEL, STEP_LANES), MXU),
                            pltpu.VMEM((1, STEP_LANES), F32), pltpu.VMEM((1, STEP_LANES), F32),
                            pltpu.VMEM((STEP_LANES, D_MODEL), F32), pltpu.VMEM((1, STEP_LANES), F32)]),
        compiler_params=pltpu.CompilerParams(
            dimension_semantics=("arbitrary", "arbitrary"), vmem_limit_bytes=VMEM_LIMIT),
        name="fox_attention_step",
    )(page_table, proj, bf, *([kvp] * npg), *([lf_pool] * npg))
    kv_new = proj[..., 1024:3072].reshape(b, t, 2, N_HEADS, HEAD_DIM)
    return o, [kv_new, logf[..., :N_HEADS]]


B_CMP_PAGES = 4
B_STEP_PAGES = 8


def _nsa_cmp_pages_kernel(pt_ref, *refs):
    pages = refs[:B_CMP_PAGES]
    pe_ref, w_ref, o_ref = refs[B_CMP_PAGES:]
    nb = PAGE_SIZE // B_BLOCK
    pooled = []
    for p in range(B_CMP_PAGES):
        x = pages[p][...].reshape(nb, B_BLOCK, 512) + pe_ref[...][None]
        pooled.append(jnp.sum(x, axis=1) * (1.0 / B_BLOCK))
    o_ref[...] = jnp.dot(jnp.concatenate(pooled, axis=0), w_ref[...], preferred_element_type=F32,
                         precision=HIGHEST)


def nsa_compress_pages(kv_pool, page_table, pe, wbd):
    b, n_pages = page_table.shape
    npg = B_CMP_PAGES
    nb = PAGE_SIZE // B_BLOCK
    page = lambda p: pl.BlockSpec((None, PAGE_SIZE, 512), lambda i, s, pt, p=p: (pt[i, s * npg + p], 0, 0))
    return pl.pallas_call(
        _nsa_cmp_pages_kernel,
        out_shape=jax.ShapeDtypeStruct((b, n_pages * nb, 512), F32),
        grid_spec=pltpu.PrefetchScalarGridSpec(
            num_scalar_prefetch=1,
            grid=(b, n_pages // npg),
            in_specs=[page(p) for p in range(npg)]
            + [pl.BlockSpec((B_BLOCK, 512), lambda i, s, pt: (0, 0)),
               pl.BlockSpec((512, 512), lambda i, s, pt: (0, 0))],
            out_specs=pl.BlockSpec((None, npg * nb, 512), lambda i, s, pt: (i, s, 0))),
        compiler_params=pltpu.CompilerParams(
            dimension_semantics=("arbitrary", "arbitrary"), vmem_limit_bytes=VMEM_LIMIT),
        name="nsa_compress_pages",
    )(page_table, *([kv_pool] * npg), pe, wbd)


def _nsa_step_kernel(pt_ref, slope_ref, new_ref, cmp_ref, win_ref, *refs, past):
    npg = B_STEP_PAGES
    pages = refs[:npg]
    (o_ref, qbd_ref, sel_ref, oc_ref, ms_ref, ls_ref, accs_ref, mw_ref, lw_ref, accw_ref) = refs[npg:]
    s = pl.program_id(1)
    lane = _lane_iota()
    ilane = lane & (STEP_T - 1)
    glane = lane >> 5
    rlane = (lane >> 3) & 3
    slope = slope_ref[...]
    pos = past + ilane
    ncb = past // B_BLOCK
    nsel = ncb + 8
    kvo = B_KV_OFF

    @pl.when(s == 0)
    def _():
        irow = lax.broadcasted_iota(jnp.int32, (STEP_T, 1), 0)
        qall = jnp.zeros((256, STEP_LANES), F32)
        for r in range(4):
            sel_r = jnp.where((irow == ilane) & (rlane == r), 1.0, 0.0)
            qall = qall + _dot_tn(new_ref[:, r * 256:(r + 1) * 256].astype(MXU), sel_r.astype(MXU))
        row_g = lax.broadcasted_iota(jnp.int32, (256, 1), 0) // HEAD_DIM
        qbd_ref[...] = (jnp.where(row_g == glane, qall, 0.0) * (HEAD_DIM ** -0.5)).astype(MXU)

        brow = lax.broadcasted_iota(jnp.int32, (ncb, 1), 0)
        blk_end = (brow + 1) * B_BLOCK - 1
        s_c = jnp.dot(cmp_ref[:, 0:256].astype(MXU), qbd_ref[...], preferred_element_type=F32)
        s_c = s_c - slope * (pos - blk_end).astype(F32)
        sm = jnp.where(blk_end <= pos, s_c, -jnp.inf)
        mx = jnp.max(sm, axis=0, keepdims=True)
        mx = jnp.where(mx > -jnp.inf, mx, 0.0)
        e = jnp.exp(sm - mx)
        den = jnp.sum(e, axis=0, keepdims=True)
        p_c = e / jnp.where(den > 0, den, 1.0)
        oc_ref[...] = _dot_tn(p_c.astype(MXU), cmp_ref[:, 256:512].astype(MXU))

        lrow = lax.broadcasted_iota(jnp.int32, (STEP_LANES, 1), 0)
        same = jnp.where(((lrow >> 5) == glane) & ((lrow & (STEP_T - 1)) == ilane), 1.0, 0.0)
        sc = jnp.dot(p_c, same, preferred_element_type=F32, precision=HIGHEST)
        sc = jnp.concatenate([sc, jnp.zeros((nsel - ncb, STEP_LANES), F32)], axis=0)
        nidx = lax.broadcasted_iota(jnp.int32, (nsel, 1), 0)
        cur = pos // B_BLOCK
        forced = (nidx == 0) | (nidx >= cur - 1)
        score = jnp.where(nidx > cur, -jnp.inf, jnp.where(forced, jnp.inf, sc))
        rank = jnp.zeros((nsel, STEP_LANES), F32)
        for n in range(ncb + 1):
            row = score[n:n + 1, :]
            beats = (row > score) | ((row == score) & (nidx > n))
            rank = rank + jnp.where(beats, 1.0, 0.0)
        sel_ref[...] = jnp.where(rank < float(B_TOPK), 1.0, 0.0)
        _step_state_init(ms_ref, ls_ref, accs_ref)

    prow = lax.broadcasted_iota(jnp.int32, (PAGE_SIZE, 1), 0)
    for p in range(npg):
        pg = s * npg + p
        kpos = pg * PAGE_SIZE + prow
        s_t = jnp.dot(pages[p][:, 0:256].astype(MXU), qbd_ref[...], preferred_element_type=F32)
        s_t = s_t - slope * (pos - kpos).astype(F32)
        sel_lo = sel_ref[pl.ds(2 * pg, 1), :]
        sel_hi = sel_ref[pl.ds(2 * pg + 1, 1), :]
        valid = jnp.where(prow < B_BLOCK, sel_lo, sel_hi) > 0.5
        _step_update(s_t, valid, pages[p][:, 256:512].astype(MXU), ms_ref, ls_ref, accs_ref)

    @pl.when(s == pl.num_programs(1) - 1)
    def _():
        nrow = lax.broadcasted_iota(jnp.int32, (STEP_T, 1), 0)
        dist_n = ilane - nrow
        s_t = jnp.dot(new_ref[:, kvo + 512:kvo + 768].astype(MXU), qbd_ref[...], preferred_element_type=F32)
        s_t = s_t - slope * dist_n.astype(F32)
        valid = (dist_n >= 0) & (sel_ref[ncb:ncb + 1, :] > 0.5)
        _step_update(s_t, valid, new_ref[:, kvo + 768:kvo + 1024].astype(MXU), ms_ref, ls_ref, accs_ref)
        _step_state_init(mw_ref, lw_ref, accw_ref)
        lw = win_ref.shape[0]
        kpos = (past - lw) + lax.broadcasted_iota(jnp.int32, (lw, 1), 0)
        dist = pos - kpos
        s_t = jnp.dot(win_ref[:, 0:256].astype(MXU), qbd_ref[...], preferred_element_type=F32)
        s_t = s_t - slope * dist.astype(F32)
        _step_update(s_t, (dist >= 0) & (dist <= B_WINDOW), win_ref[:, 256:512].astype(MXU),
                     mw_ref, lw_ref, accw_ref)
        s_t = jnp.dot(new_ref[:, kvo + 1024:kvo + 1280].astype(MXU), qbd_ref[...], preferred_element_type=F32)
        s_t = s_t - slope * dist_n.astype(F32)
        _step_update(s_t, dist_n >= 0, new_ref[:, kvo + 1280:kvo + 1536].astype(MXU), mw_ref, lw_ref, accw_ref)
        lrow = lax.broadcasted_iota(jnp.int32, (STEP_LANES, 1), 0)
        pick = jnp.where((lrow & (STEP_T - 1)) == lax.broadcasted_iota(jnp.int32, (1, STEP_T), 1), 1.0, 0.0)
        gmat = jnp.dot(pick, _sigmoid(new_ref[:, B_BG_OFF:B_BG_OFF + 128]), preferred_element_type=F32,
                       precision=HIGHEST)
        gcol = ((lrow >> 3) & 3) * 4 + (lrow >> 5)
        lane128 = lax.broadcasted_iota(jnp.int32, (1, 128), 1)
        gate = lambda br: jnp.sum(jnp.where(lane128 == gcol + 16 * br, gmat, 0.0), axis=1, keepdims=True)
        o_all = (gate(0) * oc_ref[...] + gate(1) * (accs_ref[...] / _row2col(ls_ref[...]))
                 + gate(2) * (accw_ref[...] / _row2col(lw_ref[...])))
        col_g = lax.broadcasted_iota(jnp.int32, (1, 256), 1) // HEAD_DIM
        for r in range(4):
            keep = (((lrow >> 3) & 3) == r) & ((lrow >> 5) == col_g)
            om = jnp.where(keep, o_all, 0.0)
            o_ref[:, r * 256:(r + 1) * 256] = jnp.sum(om.reshape(N_HEADS, STEP_T, 256), axis=0)


def mixer_b_step(proj, pos_emb, w_cmp, kv_pool, win_cache, page_table):
    b, t, _ = proj.shape
    n_pages = page_table.shape[1]
    npg = B_STEP_PAGES
    past = n_pages * PAGE_SIZE
    assert t == STEP_T and n_pages % npg == 0 and n_pages % B_CMP_PAGES == 0
    ncb = past // B_BLOCK
    pool = kv_pool.reshape(kv_pool.shape[0], PAGE_SIZE, 1024)
    pe, wbd = _nsa_cmp_weights(pos_emb, w_cmp)
    cmp = nsa_compress_pages(pool, page_table, pe, wbd)
    lw = win_cache.shape[1]
    win = win_cache.reshape(b, lw, 512)
    page = lambda p: pl.BlockSpec((None, PAGE_SIZE, 512), lambda i, s, pt, p=p: (pt[i, s * npg + p], 0, 1))
    o = pl.pallas_call(
        functools.partial(_nsa_step_kernel, past=past),
        out_shape=jax.ShapeDtypeStruct((b, t, D_MODEL), F32),
        grid_spec=pltpu.PrefetchScalarGridSpec(
            num_scalar_prefetch=1,
            grid=(b, n_pages // npg),
            in_specs=[pl.BlockSpec((1, STEP_LANES), lambda i, s, pt: (0, 0)),
                      pl.BlockSpec((None, t, B_E), lambda i, s, pt: (i, 0, 0)),
                      pl.BlockSpec((None, ncb, 512), lambda i, s, pt: (i, 0, 0)),
                      pl.BlockSpec((None, lw, 512), lambda i, s, pt: (i, 0, 0))]
            + [page(p) for p in range(npg)],
            out_specs=pl.BlockSpec((None, t, D_MODEL), lambda i, s, pt: (i, 0, 0)),
            scratch_shapes=[pltpu.VMEM((256, STEP_LANES), MXU),
                            pltpu.VMEM((ncb + 8, STEP_LANES), F32),
                            pltpu.VMEM((STEP_LANES, 256), F32),
                            pltpu.VMEM((1, STEP_LANES), F32), pltpu.VMEM((1, STEP_LANES), F32),
                            pltpu.VMEM((STEP_LANES, 256), F32),
                            pltpu.VMEM((1, STEP_LANES), F32), pltpu.VMEM((1, STEP_LANES), F32),
                            pltpu.VMEM((STEP_LANES, 256), F32)]),
        compiler_params=pltpu.CompilerParams(
            dimension_semantics=("arbitrary", "arbitrary"), vmem_limit_bytes=VMEM_LIMIT),
        name="nsa_attention_step",
    )(page_table, _step_slopes(), proj, cmp, win, *([pool] * npg))
    kv_full_new = proj[..., B_KV_OFF:B_KV_OFF + 1024].reshape(b, t, 4, B_KV_HEADS, HEAD_DIM)
    kv_win_new = proj[..., B_KV_OFF + 1024:B_KV_OFF + 1536].reshape(b, t, 2, B_KV_HEADS, HEAD_DIM)
    new_win = jnp.concatenate([win_cache[:, t:], kv_win_new], axis=1)
    return o, [kv_full_new, new_win]


def _tile_rows(x, n):
    return jnp.concatenate([x] * n, axis=0)


def _dot_hi(a, b):
    return jnp.dot(a, b, preferred_element_type=F32, precision=HIGHEST)


def _roll_kernel(c_ref, n_ref, o_ref):
    w = c_ref.shape[1]
    lane = lax.broadcasted_iota(jnp.int32, (1, 128), 1)
    place = jnp.where(lax.broadcasted_iota(jnp.int32, (STEP_T, 1), 0) == lane - (128 - STEP_T), 1.0, 0.0)
    new_t = lax.dot_general(n_ref[...], place, (((0,), (0,)), ((), ())), preferred_element_type=F32,
                            precision=HIGHEST)
    rolled = pltpu.roll(c_ref[...], w - STEP_T, axis=1)
    if w > 128:
        o_ref[:, 0:w - 128] = rolled[:, 0:w - 128]
    o_ref[:, w - 128:w] = jnp.where(lane >= 128 - STEP_T, new_t, rolled[:, w - 128:w])


def cache_roll(cache_t, new_rows, *, rb=512):
    nl, b, r, w = cache_t.shape
    return pl.pallas_call(
        _roll_kernel,
        out_shape=jax.ShapeDtypeStruct(cache_t.shape, F32),
        grid=(nl, b, r // rb),
        in_specs=[pl.BlockSpec((None, None, rb, w), lambda l, i, k: (l, i, k, 0)),
                  pl.BlockSpec((None, None, STEP_T, rb), lambda l, i, k: (l, i, 0, k))],
        out_specs=pl.BlockSpec((None, None, rb, w), lambda l, i, k: (l, i, k, 0)),
        compiler_params=pltpu.CompilerParams(
            dimension_semantics=("arbitrary", "arbitrary", "arbitrary"), vmem_limit_bytes=VMEM_LIMIT),
        name="cache_roll",
    )(cache_t, new_rows)


def _token_minor(x, token_axis):
    perm = [a for a in range(x.ndim) if a != token_axis] + [token_axis]
    return jnp.transpose(x, perm)


def _token_major(x, token_axis):
    perm = list(range(x.ndim - 1))
    perm.insert(token_axis, x.ndim - 1)
    return jnp.transpose(x, perm)


def _dil_step_kernel(slope_ref, new_ref, c0_ref, c1_ref, c2_ref, o_ref):
    hg = pl.program_id(1)
    rows = 4 * STEP_T
    rowi = lax.broadcasted_iota(jnp.int32, (rows, 1), 0) & (STEP_T - 1)
    bd = (lax.broadcasted_iota(jnp.int32, (rows, 1), 0) >> 3) == (lax.broadcasted_iota(jnp.int32, (1, 256), 1) >> 6)
    slope = slope_ref[...]
    pieces = []
    for g, c_ref in enumerate((c0_ref, c1_ref, c2_ref)):
        window, dil = A_GROUPS[g]
        col = lambda part: pl.ds(pl.multiple_of(3072 * g + 1024 * part + hg * 256, 256), 256)
        qbd = (jnp.where(bd, _tile_rows(new_ref[:, col(0)], 4), 0.0) * (HEAD_DIM ** -0.5)).astype(MXU)
        s = jnp.dot(qbd, c_ref[0].astype(MXU), preferred_element_type=F32)
        dist = (window + rowi) - lax.broadcasted_iota(jnp.int32, (1, window), 1)
        valid = (dist <= window) & ((dist & (dil - 1)) == 0)
        pieces.append((jnp.where(valid, s - slope * dist.astype(F32), NEG), c_ref[1].astype(MXU), True))
        sn = _dot_nt(qbd, new_ref[:, col(1)].astype(MXU))
        distn = rowi - lax.broadcasted_iota(jnp.int32, (1, STEP_T), 1)
        validn = (distn >= 0) & ((distn & (dil - 1)) == 0)
        pieces.append((jnp.where(validn, sn - slope * distn.astype(F32), NEG),
                       new_ref[:, col(2)].astype(MXU), False))
    m = pieces[0][0][:, 0:1]
    for s, _, _ in pieces:
        m = jnp.maximum(m, jnp.max(s, axis=1, keepdims=True))
    den = jnp.zeros((rows, 1), F32)
    acc = jnp.zeros((rows, 256), F32)
    for s, v, transposed in pieces:
        p = jnp.exp(s - m)
        den = den + jnp.sum(p, axis=1, keepdims=True)
        pb = p.astype(MXU)
        acc = acc + (_dot_nt(pb, v) if transposed else jnp.dot(pb, v, preferred_element_type=F32))
    o = jnp.where(bd, acc / den, 0.0)
    o_ref[...] = jnp.sum(o.reshape(4, STEP_T, 256), axis=0)


def mixer_a_step(proj, caches, j):
    b, t, _ = proj.shape
    assert t == STEP_T
    views = [_token_minor(c, 2).reshape(c.shape[0], b, 2, D_MODEL, c.shape[2]) for c in caches]
    slopes = np.float32(2.0 ** (-8.0 * np.arange(1, N_HEADS + 1) / N_HEADS))
    slope_rows = jnp.asarray(np.repeat(slopes, STEP_T).reshape(4, 4 * STEP_T, 1), F32)
    cspec = lambda w: pl.BlockSpec((None, None, 2, 256, w), lambda i, h: (j, i, 0, h, 0))
    o = pl.pallas_call(
        _dil_step_kernel,
        out_shape=jax.ShapeDtypeStruct((b, t, D_MODEL), F32),
        grid=(b, 4),
        in_specs=[pl.BlockSpec((None, 4 * STEP_T, 1), lambda i, h: (h, 0, 0)),
                  pl.BlockSpec((None, t, A_E), lambda i, h: (i, 0, 0))]
        + [cspec(v.shape[-1]) for v in views],
        out_specs=pl.BlockSpec((None, t, 256), lambda i, h: (i, 0, h)),
        compiler_params=pltpu.CompilerParams(
            dimension_semantics=("arbitrary", "arbitrary"), vmem_limit_bytes=VMEM_LIMIT),
        name="dilated_attention_step",
    )(slope_rows, proj, *views)
    return o, [proj[..., 3072 * g + 1024:3072 * g + 3072] for g in range(3)]


def roll_a_caches(caches, new_rows):
    nl, b, w = caches.shape[:3]
    ct = _token_minor(caches, 2).reshape(nl, b, 2 * D_MODEL, w)
    out = cache_roll(ct, jnp.stack(new_rows))
    return _token_major(out.reshape(nl, b, 2, N_HEADS, HEAD_DIM, w), 2)


C_STEP_PAGES = 8


def _fox_step_kernel(pt_ref, new_ref, bf_ref, *refs):
    npg = C_STEP_PAGES
    kv_refs = refs[:npg]
    lf_refs = refs[npg:2 * npg]
    o_ref, lfo_ref, qbd_ref, m_ref, l_ref, acc_ref, carry_ref = refs[2 * npg:]
    s_id = pl.program_id(1)
    rows = 4 * STEP_T
    rowi = lax.broadcasted_iota(jnp.int32, (STEP_LANES, 1), 0) & (STEP_T - 1)
    bd = (lax.broadcasted_iota(jnp.int32, (rows, 1), 0) >> 3) == (lax.broadcasted_iota(jnp.int32, (1, 256), 1) >> 6)
    expand = jnp.where((lax.broadcasted_iota(jnp.int32, (STEP_LANES, 1), 0) >> 3)
                       == lax.broadcasted_iota(jnp.int32, (1, N_HEADS), 1), 1.0, 0.0)

    @pl.when(s_id == 0)
    def _():
        m_ref[...] = jnp.full(m_ref.shape, NEG, F32)
        l_ref[...] = jnp.zeros(l_ref.shape, F32)
        acc_ref[...] = jnp.zeros(acc_ref.shape, F32)
        carry_ref[...] = jnp.zeros(carry_ref.shape, F32)
        for hg in range(4):
            q = new_ref[:, 256 * hg:256 * (hg + 1)]
            qbd_ref[hg] = (jnp.where(bd, _tile_rows(q, 4), 0.0) * (HEAD_DIM ** -0.5)).astype(MXU)

    def update(s, valid, pv_fn):
        if valid is not None:
            s = jnp.where(valid, s, NEG)
        m_old = m_ref[...]
        m_new = jnp.maximum(m_old, _rowmax(s) if s.shape[1] % 128 == 0 else jnp.max(s, axis=1, keepdims=True))
        p = jnp.exp(s - m_new)
        alpha = jnp.exp(m_old - m_new)
        l_ref[...] = alpha * l_ref[...] + jnp.sum(p, axis=1, keepdims=True)
        pb = p.astype(MXU)
        pv = jnp.concatenate([pv_fn(pb[rows * hg:rows * (hg + 1)], hg) for hg in range(4)], axis=0)
        acc_ref[...] = alpha * acc_ref[...] + pv
        m_ref[...] = m_new

    triu = jnp.where(lax.broadcasted_iota(jnp.int32, (PAGE_SIZE, 1), 0)
                     <= lax.broadcasted_iota(jnp.int32, (1, PAGE_SIZE), 1), 1.0, 0.0)
    carry = carry_ref[...]
    cks = []
    for p in range(npg):
        cum = _dot_hi(lf_refs[p][...], triu) + carry
        carry = cum[:, PAGE_SIZE - 1:PAGE_SIZE]
        cks.append(cum)
    carry_ref[...] = carry
    ck = _dot_hi(expand, jnp.concatenate(cks, axis=1))
    s = jnp.concatenate(
        [jnp.dot(qbd_ref[hg],
                 jnp.concatenate([kv_refs[p][0, 256 * hg:256 * (hg + 1), :] for p in range(npg)], axis=1).astype(MXU),
                 preferred_element_type=F32) for hg in range(4)], axis=0) - ck
    update(s, None, lambda pb, hg: _dot_nt(
        pb, jnp.concatenate([kv_refs[p][1, 256 * hg:256 * (hg + 1), :] for p in range(npg)], axis=1).astype(MXU)))

    @pl.when(s_id == pl.num_programs(1) - 1)
    def _():
        logf = _log_sigmoid(new_ref[:, C_F_OFF:C_F_OFF + 128] + bf_ref[...])
        lfo_ref[...] = logf
        tri8 = jnp.where(lax.broadcasted_iota(jnp.int32, (STEP_T, 1), 0)
                         <= lax.broadcasted_iota(jnp.int32, (1, 128), 1), 1.0, 0.0)
        cum_n = lax.dot_general(logf, tri8, (((0,), (0,)), ((), ())), preferred_element_type=F32,
                                precision=HIGHEST)[0:N_HEADS, :] + carry_ref[...]
        ck_n = _dot_hi(expand, cum_n)[:, 0:STEP_T]
        s_n = jnp.concatenate(
            [_dot_nt(qbd_ref[hg], new_ref[:, 1024 + 256 * hg:1024 + 256 * (hg + 1)].astype(MXU))
             for hg in range(4)], axis=0) - ck_n
        valid = lax.broadcasted_iota(jnp.int32, (1, STEP_T), 1) <= rowi
        update(s_n, valid, lambda pb, hg: jnp.dot(
            pb, new_ref[:, 2048 + 256 * hg:2048 + 256 * (hg + 1)].astype(MXU), preferred_element_type=F32))
        o = acc_ref[...] / l_ref[...]
        for hg in range(4):
            om = jnp.where(bd, o[rows * hg:rows * (hg + 1)], 0.0)
            o_ref[:, 256 * hg:256 * (hg + 1)] = jnp.sum(om.reshape(4, STEP_T, 256), axis=0)


def mixer_c_step(proj, b_forget, kv_pool, lf_pool, page_table):
    b, t, _ = proj.shape
    n_pages = page_table.shape[1]
    npg = C_STEP_PAGES
    assert t == STEP_T and n_pages % npg == 0
    kvp = _token_minor(kv_pool, 1).reshape(kv_pool.shape[0], 2, D_MODEL, PAGE_SIZE)
    lfp = _token_minor(lf_pool, 1)
    bf = jnp.pad(b_forget.astype(F32), (0, 128 - N_HEADS)).reshape(1, 128)
    page = lambda shape, p: pl.BlockSpec(shape, lambda i, s, pt, p=p: (pt[i, s * npg + p],) + (0,) * (len(shape) - 1))
    o, logf = pl.pallas_call(
        _fox_step_kernel,
        out_shape=[jax.ShapeDtypeStruct((b, t, D_MODEL), F32), jax.ShapeDtypeStruct((b, t, 128), F32)],
        grid_spec=pltpu.PrefetchScalarGridSpec(
            num_scalar_prefetch=1,
            grid=(b, n_pages // npg),
            in_specs=[pl.BlockSpec((None, t, C_E), lambda i, s, pt: (i, 0, 0)),
                      pl.BlockSpec((1, 128), lambda i, s, pt: (0, 0))]
            + [page((None, 2, D_MODEL, PAGE_SIZE), p) for p in range(npg)]
            + [page((None, N_HEADS, PAGE_SIZE), p) for p in range(npg)],
            out_specs=[pl.BlockSpec((None, t, D_MODEL), lambda i, s, pt: (i, 0, 0)),
                       pl.BlockSpec((None, t, 128), lambda i, s, pt: (i, 0, 0))],
            scratch_shapes=[pltpu.VMEM((4, 4 * STEP_T, 256), MXU),
                            pltpu.VMEM((STEP_LANES, 1), F32), pltpu.VMEM((STEP_LANES, 1), F32),
                            pltpu.VMEM((STEP_LANES, 256), F32), pltpu.VMEM((N_HEADS, 1), F32)]),
        compiler_params=pltpu.CompilerParams(
            dimension_semantics=("arbitrary", "arbitrary"), vmem_limit_bytes=VMEM_LIMIT),
        name="fox_attention_step",
    )(page_table, proj, bf, *([kvp] * npg), *([lfp] * npg))
    kv_new = proj[..., 1024:3072].reshape(b, t, 2, N_HEADS, HEAD_DIM)
    return o, [kv_new, logf[..., :N_HEADS]]


B_STEP_PAGES = 8


def _nsa_cmp_pages_kernel(pt_ref, *refs):
    npg = B_STEP_PAGES
    pages = refs[:npg]
    pet_ref, w_ref, o_ref = refs[npg:]
    nb = npg * PAGE_SIZE // B_BLOCK
    pool = jnp.where(lax.broadcasted_iota(jnp.int32, (npg * PAGE_SIZE, 1), 0) // B_BLOCK
                     == lax.broadcasted_iota(jnp.int32, (1, 128), 1), 1.0 / B_BLOCK, 0.0).astype(MXU)
    outs = []
    for c in range(2):
        x = jnp.concatenate([pages[p][c] + pet_ref[c] for p in range(npg)], axis=1)
        hi = x.astype(MXU)
        lo = (x - hi.astype(F32)).astype(MXU)
        pooled = (jnp.dot(hi, pool, preferred_element_type=F32) + jnp.dot(lo, pool, preferred_element_type=F32))
        cmp_c = lax.dot_general(pooled, w_ref[c], (((0,), (0,)), ((), ())), preferred_element_type=F32,
                                precision=HIGHEST)
        outs.append(cmp_c[0:nb, :])
    o_ref[...] = jnp.concatenate(outs, axis=1)


def nsa_compress_pages(pool_t, page_table, pos_emb, w_cmp):
    b, n_pages = page_table.shape
    npg = B_STEP_PAGES
    nb = npg * PAGE_SIZE // B_BLOCK
    pet = pos_emb.transpose(0, 2, 3, 1).reshape(2, 256, B_BLOCK)
    pet = jnp.concatenate([pet] * (PAGE_SIZE // B_BLOCK), axis=2)
    _, wbd = _nsa_cmp_weights(pos_emb, w_cmp)
    wc = jnp.stack([wbd[0:256, 0:256], wbd[256:512, 256:512]])
    page = lambda p: pl.BlockSpec((None, 2, 256, PAGE_SIZE), lambda i, s, pt, p=p: (pt[i, s * npg + p], 0, 0, 0))
    return pl.pallas_call(
        _nsa_cmp_pages_kernel,
        out_shape=jax.ShapeDtypeStruct((b, n_pages * PAGE_SIZE // B_BLOCK, 512), F32),
        grid_spec=pltpu.PrefetchScalarGridSpec(
            num_scalar_prefetch=1,
            grid=(b, n_pages // npg),
            in_specs=[page(p) for p in range(npg)]
            + [pl.BlockSpec((2, 256, PAGE_SIZE), lambda i, s, pt: (0, 0, 0)),
               pl.BlockSpec((2, 256, 256), lambda i, s, pt: (0, 0, 0))],
            out_specs=pl.BlockSpec((None, nb, 512), lambda i, s, pt: (i, s, 0))),
        compiler_params=pltpu.CompilerParams(
            dimension_semantics=("arbitrary", "arbitrary"), vmem_limit_bytes=VMEM_LIMIT),
        name="nsa_compress_pages",
    )(page_table, *([pool_t] * npg), pet, wc)


def _nsa_step_kernel(pt_ref, slope_ref, new_ref, cmp_ref, win_ref, *refs, past):
    npg = B_STEP_PAGES
    pages = refs[:npg]
    o_ref, qbd_ref, sel_ref, oc_ref, m_ref, l_ref, acc_ref = refs[npg:]
    s_id = pl.program_id(1)
    rows = STEP_LANES
    ridx = lax.broadcasted_iota(jnp.int32, (rows, 1), 0)
    rowi = ridx & (STEP_T - 1)
    rowg = (ridx >> 3) & 3
    colg = lax.broadcasted_iota(jnp.int32, (1, 256), 1) >> 6
    bd = rowg == colg
    slope = slope_ref[...]
    pos = past + rowi
    ncb = past // B_BLOCK
    nsel = sel_ref.shape[1]
    kvo = B_KV_OFF

    def update(s, valid, pv):
        s = jnp.where(valid, s, NEG)
        m_old = m_ref[...]
        m_new = jnp.maximum(m_old, jnp.max(s, axis=1, keepdims=True))
        p = jnp.exp(s - m_new)
        alpha = jnp.exp(m_old - m_new)
        l_ref[...] = alpha * l_ref[...] + jnp.sum(p, axis=1, keepdims=True)
        acc_ref[...] = alpha * acc_ref[...] + pv(p.astype(MXU))
        m_ref[...] = m_new

    @pl.when(s_id == 0)
    def _():
        qbd = jnp.concatenate(
            [jnp.where(bd[0:32], _tile_rows(new_ref[:, 256 * r:256 * (r + 1)], 4), 0.0) for r in range(4)], axis=0)
        qbd_ref[...] = (qbd * (HEAD_DIM ** -0.5)).astype(MXU)
        blk_end = (lax.broadcasted_iota(jnp.int32, (1, ncb), 1) + 1) * B_BLOCK - 1
        s_c = _dot_nt(qbd_ref[...], cmp_ref[:, 0:256].astype(MXU)) - slope * (pos - blk_end).astype(F32)
        sm = jnp.where(blk_end <= pos, s_c, -jnp.inf)
        mx = jnp.max(sm, axis=1, keepdims=True)
        mx = jnp.where(mx > -jnp.inf, mx, 0.0)
        e = jnp.exp(sm - mx)
        den = jnp.sum(e, axis=1, keepdims=True)
        p_c = e / jnp.where(den > 0, den, 1.0)
        oc_ref[...] = jnp.dot(p_c.astype(MXU), cmp_ref[:, 256:512].astype(MXU), preferred_element_type=F32)
        cidx = lax.broadcasted_iota(jnp.int32, (1, rows), 1)
        same = jnp.where((((cidx >> 3) & 3) == rowg) & ((cidx & (STEP_T - 1)) == rowi), 1.0, 0.0)
        sc = _dot_hi(same, p_c)
        sc = jnp.concatenate([sc, jnp.zeros((rows, nsel - ncb), F32)], axis=1)
        nidx = lax.broadcasted_iota(jnp.int32, (1, nsel), 1)
        cur = pos // B_BLOCK
        forced = (nidx == 0) | (nidx >= cur - 1)
        score = jnp.where(nidx > cur, -jnp.inf, jnp.where(forced, jnp.inf, sc))
        rank = jnp.zeros((rows, nsel), F32)
        for n in range(ncb + 1):
            colv = score[:, n:n + 1]
            beats = (colv > score) | ((colv == score) & (nidx > n))
            rank = rank + jnp.where(beats, 1.0, 0.0)
        sel_ref[...] = jnp.where(rank < float(B_TOPK), 1.0, 0.0).astype(MXU)
        m_ref[...] = jnp.full(m_ref.shape, NEG, F32)
        l_ref[...] = jnp.zeros(l_ref.shape, F32)
        acc_ref[...] = jnp.zeros(acc_ref.shape, F32)

    nk = npg * PAGE_SIZE
    kbase = s_id * nk
    kt = jnp.concatenate([pages[p][0] for p in range(npg)], axis=1).astype(MXU)
    vt = jnp.concatenate([pages[p][1] for p in range(npg)], axis=1).astype(MXU)
    kpos = kbase + lax.broadcasted_iota(jnp.int32, (1, nk), 1)
    s = jnp.dot(qbd_ref[...], kt, preferred_element_type=F32) - slope * (pos - kpos).astype(F32)
    ex = jnp.where(lax.broadcasted_iota(jnp.int32, (nsel, 1), 0) == (kpos // B_BLOCK), 1.0, 0.0).astype(MXU)
    selk = jnp.dot(sel_ref[...], ex, preferred_element_type=F32)
    update(s, selk > 0.5, lambda pb: _dot_nt(pb, vt))

    @pl.when(s_id == pl.num_programs(1) - 1)
    def _():
        distn = rowi - lax.broadcasted_iota(jnp.int32, (1, STEP_T), 1)
        sn = _dot_nt(qbd_ref[...], new_ref[:, kvo + 512:kvo + 768].astype(MXU)) - slope * distn.astype(F32)
        vn = new_ref[:, kvo + 768:kvo + 1024].astype(MXU)
        seln = sel_ref[:, ncb:ncb + 1].astype(F32) > 0.5
        update(sn, (distn >= 0) & seln, lambda pb: jnp.dot(pb, vn, preferred_element_type=F32))
        o_s = acc_ref[...] / l_ref[...]
        lw = win_ref.shape[2]
        dist = pos - ((past - lw) + lax.broadcasted_iota(jnp.int32, (1, lw), 1))
        sw = jnp.dot(qbd_ref[...], win_ref[0].astype(MXU), preferred_element_type=F32) - slope * dist.astype(F32)
        sw = jnp.where((dist >= 0) & (dist <= B_WINDOW), sw, NEG)
        swn = _dot_nt(qbd_ref[...], new_ref[:, kvo + 1024:kvo + 1280].astype(MXU)) - slope * distn.astype(F32)
        swn = jnp.where(distn >= 0, swn, NEG)
        mw = jnp.maximum(_rowmax(sw), jnp.max(swn, axis=1, keepdims=True))
        pw = jnp.exp(sw - mw)
        pwn = jnp.exp(swn - mw)
        o_w = (_dot_nt(pw.astype(MXU), win_ref[1].astype(MXU))
               + jnp.dot(pwn.astype(MXU), new_ref[:, kvo + 1280:kvo + 1536].astype(MXU), preferred_element_type=F32))
        o_w = o_w / (jnp.sum(pw, axis=1, keepdims=True) + jnp.sum(pwn, axis=1, keepdims=True))
        pick = jnp.where(rowi == lax.broadcasted_iota(jnp.int32, (1, STEP_T), 1), 1.0, 0.0)
        gmat = _dot_hi(pick, _sigmoid(new_ref[:, B_BG_OFF:B_BG_OFF + 128]))
        gcol = (ridx >> 5) * 4 + rowg
        lane128 = lax.broadcasted_iota(jnp.int32, (1, 128), 1)
        gate = lambda br: jnp.sum(jnp.where(lane128 == gcol + 16 * br, gmat, 0.0), axis=1, keepdims=True)
        o_all = jnp.where(bd, gate(0) * oc_ref[...] + gate(1) * o_s + gate(2) * o_w, 0.0)
        for r in range(4):
            o_ref[:, 256 * r:256 * (r + 1)] = jnp.sum(o_all[32 * r:32 * (r + 1)].reshape(4, STEP_T, 256), axis=0)


def mixer_b_step(proj, pos_emb, w_cmp, kv_pool, win_cache, page_table):
    b, t, _ = proj.shape
    n_pages = page_table.shape[1]
    npg = B_STEP_PAGES
    past = n_pages * PAGE_SIZE
    assert t == STEP_T and n_pages % npg == 0
    ncb = past // B_BLOCK
    nsel = -(-(ncb + 1) // 128) * 128
    pool_t = _token_minor(kv_pool, 1).reshape(kv_pool.shape[0], 4, 256, PAGE_SIZE)
    cmp = nsa_compress_pages(pool_t, page_table, pos_emb, w_cmp)
    lw = win_cache.shape[1]
    win_t = _token_minor(win_cache, 1).reshape(b, 2, 256, lw)
    heads = np.array([4 * g + r for r in range(4) for g in range(4)])
    slopes = np.float32(2.0 ** (-8.0 * (heads + 1) / N_HEADS))
    slope_rows = jnp.asarray(np.repeat(slopes, STEP_T).reshape(STEP_LANES, 1), F32)
    page = lambda p: pl.BlockSpec((None, 2, 256, PAGE_SIZE), lambda i, s, pt, p=p: (pt[i, s * npg + p], 1, 0, 0))
    o = pl.pallas_call(
        functools.partial(_nsa_step_kernel, past=past),
        out_shape=jax.ShapeDtypeStruct((b, t, D_MODEL), F32),
        grid_spec=pltpu.PrefetchScalarGridSpec(
            num_scalar_prefetch=1,
            grid=(b, n_pages // npg),
            in_specs=[pl.BlockSpec((STEP_LANES, 1), lambda i, s, pt: (0, 0)),
                      pl.BlockSpec((None, t, B_E), lambda i, s, pt: (i, 0, 0)),
                      pl.BlockSpec((None, ncb, 512), lambda i, s, pt: (i, 0, 0)),
                      pl.BlockSpec((None, 2, 256, lw), lambda i, s, pt: (i, 0, 0, 0))]
            + [page(p) for p in range(npg)],
            out_specs=pl.BlockSpec((None, t, D_MODEL), lambda i, s, pt: (i, 0, 0)),
            scratch_shapes=[pltpu.VMEM((STEP_LANES, 256), MXU),
                            pltpu.VMEM((STEP_LANES, nsel), MXU),
                            pltpu.VMEM((STEP_LANES, 256), F32),
                            pltpu.VMEM((STEP_LANES, 1), F32), pltpu.VMEM((STEP_LANES, 1), F32),
                            pltpu.VMEM((STEP_LANES, 256), F32)]),
        compiler_params=pltpu.CompilerParams(
            dimension_semantics=("arbitrary", "arbitrary"), vmem_limit_bytes=VMEM_LIMIT),
        name="nsa_attention_step",
    )(page_table, slope_rows, proj, cmp, win_t, *([pool_t] * npg))
    kv_full_new = proj[..., B_KV_OFF:B_KV_OFF + 1024].reshape(b, t, 4, B_KV_HEADS, HEAD_DIM)
    return o, [kv_full_new, proj[..., B_KV_OFF + 1024:B_KV_OFF + 1536]]


def roll_b_window(win_caches, new_rows):
    nl, b, w = win_caches.shape[:3]
    ct = _token_minor(win_caches, 2).reshape(nl, b, 2 * B_KV_WIDTH, w)
    out = cache_roll(ct, jnp.stack(new_rows))
    return _token_major(out.reshape(nl, b, 2, B_KV_HEADS, HEAD_DIM, w), 2)


def alibi_slopes(n):
    return jnp.asarray(2.0 ** (-8.0 * np.arange(1, n + 1) / n), dtype=jnp.float32)


def masked_softmax(s, mask, axis=-1):
    s = jnp.where(mask, s, -jnp.inf)
    m = jnp.max(s, axis=axis, keepdims=True)
    m = jnp.where(jnp.isfinite(m), m, 0.0)
    e = jnp.exp(s - m)
    den = jnp.sum(e, axis=axis, keepdims=True)
    p = e / jnp.where(den > 0, den, 1.0)
    lse = jnp.squeeze(m + jnp.log(den), axis=axis)
    return p, lse


def gather_pages(pool, page_table):
    g = pool[page_table]
    return g.reshape((g.shape[0], g.shape[1] * g.shape[2]) + g.shape[3:])


def dilated_attn_prompt(q, k, v, window, dilation, slopes):
    b, t, h, hd = q.shape
    nk = window // dilation
    ln = t // dilation
    nb = -(-ln // nk)
    lp = nb * nk

    def streams(z):
        z = z.reshape(b, ln, dilation, h, hd).transpose(0, 2, 1, 3, 4)
        z = jnp.pad(z, ((0, 0), (0, 0), (0, lp - ln), (0, 0), (0, 0)))
        return z.reshape(b, dilation, nb, nk, h, hd)

    def with_prev(z):
        prev = jnp.pad(z, ((0, 0), (0, 0), (1, 0), (0, 0), (0, 0), (0, 0)))[:, :, :nb]
        return jnp.concatenate([prev, z], axis=3)

    qs = streams(q)
    kb = with_prev(streams(k))
    vb = with_prev(streams(v))
    s = jnp.einsum('brnqhd,brnkhd->brnhqk', qs, kb).astype(jnp.float32) * (hd ** -0.5)
    step = (nk + jnp.arange(nk))[:, None] - jnp.arange(2 * nk)[None, :]
    first = ((jnp.arange(nb) - 1)[:, None, None] * nk + jnp.arange(2 * nk)[None, None, :]) >= 0
    mask = (step >= 0) & (step <= nk) & first
    s = s - slopes[:, None, None] * (step * dilation).astype(jnp.float32)
    p, lse = masked_softmax(s, mask[None, None, :, None])
    o = jnp.einsum('brnhqk,brnkhd->brnqhd', p.astype(v.dtype), vb)
    o = o.reshape(b, dilation, lp, h, hd)[:, :, :ln].transpose(0, 2, 1, 3, 4).reshape(b, t, h, hd)
    lse = lse.transpose(0, 1, 2, 4, 3).reshape(b, dilation, lp, h)[:, :, :ln]
    lse = lse.transpose(0, 2, 1, 3).reshape(b, t, h)
    return o, lse


def dilated_attn_step(q, k_all, v_all, window, dilation, slopes):
    b, tq, h, hd = q.shape
    lk = k_all.shape[1]
    nk = window // dilation
    steps = jnp.arange(nk + 1)
    idx = (lk - tq + jnp.arange(tq))[:, None] - dilation * steps[None, :]
    valid = idx >= 0
    idx = jnp.maximum(idx, 0)
    kg = k_all[:, idx]
    vg = v_all[:, idx]
    s = jnp.einsum('bqhd,bqjhd->bhqj', q, kg).astype(jnp.float32) * (hd ** -0.5)
    s = s - slopes[:, None, None] * (dilation * steps).astype(jnp.float32)
    p, lse = masked_softmax(s, valid)
    o = jnp.einsum('bhqj,bqjhd->bqhd', p.astype(v_all.dtype), vg)
    return o, lse.transpose(0, 2, 1)


def mixer_a_jax(proj, cache):
    b, t, _ = proj.shape
    n_g = len(A_GROUPS)
    qkv = proj[..., :3 * n_g * A_WIDTH].reshape(b, t, n_g, 3, N_HEADS, HEAD_DIM)
    slopes = alibi_slopes(N_HEADS)
    outs, lses, new_state = [], [], []
    for g, (window, dilation) in enumerate(A_GROUPS):
        q = qkv[:, :, g, 0]
        kv_new = qkv[:, :, g, 1:]
        if cache is None:
            o, lse = dilated_attn_prompt(q, kv_new[:, :, 0], kv_new[:, :, 1], window, dilation, slopes)
            new_state.append(kv_new[:, t - min(window, t):])
        else:
            kv_all = jnp.concatenate([cache[g], kv_new], axis=1)
            o, lse = dilated_attn_step(q, kv_all[:, :, 0], kv_all[:, :, 1], window, dilation, slopes)
            new_state.append(kv_all[:, t:])
        outs.append(o)
        lses.append(lse)
    wts = jax.nn.softmax(jnp.stack(lses), axis=0)
    o = jnp.einsum('gbth,gbthd->bthd', wts, jnp.stack(outs).astype(jnp.float32))
    return o.reshape(b, t, A_WIDTH), new_state


def nsa_attention(q, kv_full, kv_win, gates, pos_emb, w_cmp):
    b, tq, hq, hd = q.shape
    l = kv_full.shape[1]
    lw = kv_win.shape[1]
    g = B_KV_HEADS
    r = hq // g
    scale = hd ** -0.5
    dt = q.dtype
    slopes = alibi_slopes(hq).reshape(g, r)
    nblk = -(-l // B_BLOCK)
    lp = nblk * B_BLOCK
    blocks = jnp.pad(kv_full, ((0, 0), (0, lp - l), (0, 0), (0, 0), (0, 0))).reshape(b, nblk, B_BLOCK, 4, g, hd)
    pooled = jnp.mean(blocks[:, :, :, :2] + pos_emb.transpose(1, 0, 2, 3), axis=2)
    cmp = jnp.einsum('bncgd,cgde->bncge', pooled, w_cmp)
    k_cmp, v_cmp = cmp[:, :, 0], cmp[:, :, 1]
    sel_blocks = blocks[:, :, :, 2:].transpose(0, 4, 1, 2, 3, 5)
    kw_p = jnp.pad(kv_win, ((0, 0), (B_WINDOW, 0), (0, 0), (0, 0), (0, 0)))
    blk_end = (jnp.arange(nblk) + 1) * B_BLOCK - 1
    bidx = jnp.arange(nblk)
    n_sel = min(B_TOPK, nblk)
    qb_size = B_QBLOCK if tq % B_QBLOCK == 0 else tq
    nq = tq // qb_size

    def one_block(args):
        qb, gb, j = args
        pos = l - tq + j * qb_size + jnp.arange(qb_size)
        qg = qb.reshape(b, qb_size, g, r, hd)
        s_c = jnp.einsum('bqgrd,bngd->bgrqn', qg, k_cmp).astype(jnp.float32) * scale
        s_c = s_c - slopes[:, :, None, None] * (pos[:, None] - blk_end[None, :]).astype(jnp.float32)
        p_c, _ = masked_softmax(s_c, blk_end[None, :] <= pos[:, None])
        o_c = jnp.einsum('bgrqn,bngd->bqgrd', p_c.astype(dt), v_cmp)
        cur = pos // B_BLOCK
        forced = (bidx[None, :] == 0) | (bidx[None, :] >= cur[:, None] - 1)
        score = jnp.where(bidx[None, :] > cur[:, None], -jnp.inf,
                          jnp.where(forced, jnp.inf, jnp.sum(p_c, axis=2)))
        _, sel = lax.top_k(score, n_sel)
        kv_sel = jax.vmap(jax.vmap(lambda blk, ix: blk[ix]))(sel_blocks, sel)
        kpos = sel[..., None] * B_BLOCK + jnp.arange(B_BLOCK)
        dist_s = pos[None, None, :, None, None] - kpos
        s_s = jnp.einsum('bqgrd,bgqnkd->bgrqnk', qg, kv_sel[..., 0, :]).astype(jnp.float32) * scale
        s_s = s_s - slopes[None, :, :, None, None, None] * dist_s[:, :, None].astype(jnp.float32)
        p_s, _ = masked_softmax(s_s, (dist_s >= 0)[:, :, None], axis=(-2, -1))
        o_s = jnp.einsum('bgrqnk,bgqnkd->bqgrd', p_s.astype(dt), kv_sel[..., 1, :])
        start = lw - tq + j * qb_size
        kvw = lax.dynamic_slice_in_dim(kw_p, start, B_WINDOW + qb_size, axis=1)
        kpos_w = l - tq + j * qb_size - B_WINDOW + jnp.arange(B_WINDOW + qb_size)
        dist_w = pos[:, None] - kpos_w[None, :]
        mask_w = (dist_w >= 0) & (dist_w <= B_WINDOW) & (kpos_w[None, :] >= l - lw)
        s_w = jnp.einsum('bqgrd,bkgd->bgrqk', qg, kvw[:, :, 0]).astype(jnp.float32) * scale
        s_w = s_w - slopes[:, :, None, None] * dist_w.astype(jnp.float32)
        p_w, _ = masked_softmax(s_w, mask_w)
        o_w = jnp.einsum('bgrqk,bkgd->bqgrd', p_w.astype(dt), kvw[:, :, 1])
        gb = gb.reshape(b, qb_size, g, r, 3)
        o = gb[..., 0:1] * o_c + gb[..., 1:2] * o_s + gb[..., 2:3] * o_w
        return o.reshape(b, qb_size, hq, hd)

    qs = q.reshape(b, nq, qb_size, hq, hd).transpose(1, 0, 2, 3, 4)
    gs = gates.reshape(b, nq, qb_size, hq, 3).transpose(1, 0, 2, 3, 4)
    out = lax.map(one_block, (qs, gs, jnp.arange(nq)))
    return out.transpose(1, 0, 2, 3, 4).reshape(b, tq, hq, hd)


def mixer_b_jax(proj, pos_emb, w_cmp, cache):
    b, t, _ = proj.shape
    q = proj[..., :D_MODEL].reshape(b, t, N_HEADS, HEAD_DIM)
    o1 = D_MODEL + 6 * B_KV_WIDTH
    kv = lax.optimization_barrier(proj[..., D_MODEL:o1].reshape(b, t, 6, B_KV_HEADS, HEAD_DIM))
    branch_gates = jax.nn.sigmoid(proj[..., o1:o1 + 3 * N_HEADS].reshape(b, t, N_HEADS, 3))
    kv_full_new, kv_win_new = kv[:, :, :4], kv[:, :, 4:]
    if cache is None:
        kv_full, kv_win = kv_full_new, kv_win_new
        new_win = kv_win_new[:, t - min(B_WINDOW, t):]
    else:
        kv_full = jnp.concatenate([cache[0], kv_full_new], axis=1)
        kv_win = jnp.concatenate([cache[1], kv_win_new], axis=1)
        new_win = kv_win[:, t:]
    o = nsa_attention(q, kv_full, kv_win, branch_gates, pos_emb, w_cmp)
    return o.reshape(b, t, D_MODEL), [kv_full_new, new_win]


def fox_attention(q, k, v, cum):
    b, tq, h, hd = q.shape
    l = k.shape[1]
    scale = hd ** -0.5
    qb_size = C_QBLOCK if tq % C_QBLOCK == 0 else tq
    nq = tq // qb_size
    cum_k = cum.transpose(0, 2, 1)
    cum_q = cum[:, l - tq:].reshape(b, nq, qb_size, h).transpose(1, 0, 3, 2)
    kpos = jnp.arange(l)

    def one_block(args):
        qb, cq, j = args
        pos = l - tq + j * qb_size + jnp.arange(qb_size)
        s = jnp.einsum('bqhd,bkhd->bhqk', qb, k).astype(jnp.float32) * scale
        s = s + cq[..., None] - cum_k[:, :, None, :]
        p, _ = masked_softmax(s, kpos[None, :] <= pos[:, None])
        return jnp.einsum('bhqk,bkhd->bqhd', p.astype(v.dtype), v)

    qs = q.reshape(b, nq, qb_size, h, hd).transpose(1, 0, 2, 3, 4)
    out = lax.map(one_block, (qs, cum_q, jnp.arange(nq)))
    return out.transpose(1, 0, 2, 3, 4).reshape(b, tq, h, hd)


def mixer_c_jax(proj, b_forget, cache):
    b, t, _ = proj.shape
    qkv = proj[..., :3 * D_MODEL].reshape(b, t, 3, N_HEADS, HEAD_DIM)
    logf = jax.nn.log_sigmoid(proj[..., 3 * D_MODEL:3 * D_MODEL + N_HEADS].astype(jnp.float32)
                              + b_forget.astype(jnp.float32))
    kv_new = qkv[:, :, 1:]
    if cache is None:
        kv_all, logf_all = kv_new, logf
    else:
        kv_all = jnp.concatenate([cache[0], kv_new], axis=1)
        logf_all = jnp.concatenate([cache[1].astype(jnp.float32), logf], axis=1)
    cum = jnp.cumsum(logf_all, axis=1)
    o = fox_attention(qkv[:, :, 0], kv_all[:, :, 0], kv_all[:, :, 1], cum)
    return o.reshape(b, t, D_MODEL), [kv_new, logf]


def _pad_cols(w, e_pad):
    return jnp.pad(w, ((0, 0), (0, e_pad - w.shape[1])))


def run_trunk(x, mods, layer_cache, weights, *, bt, tt):
    (norm_pre, norm_post, a_w_in, a_w_out, b_w_in, b_pos_emb, b_w_cmp, b_w_out,
     c_w_in, c_b_forget, c_w_out) = weights
    states = []
    for i in range(DEPTH):
        kind, j = i % N_MIXERS, i // N_MIXERS
        mod = mods[i][:, None, :]
        cache = layer_cache(i)
        if kind == 0:
            w_in = a_w_in[j].astype(BF16)
            proj = in_projection(x, mod, norm_pre[i], w_in, bt=bt, tt=tt, tn=1024)
            o, st = mixer_a_prompt(proj) if cache is None else mixer_a_step(proj, cache, j)
            gate_col, w_out = A_E // 1024 - 1, a_w_out[j]
        elif kind == 1:
            w_in = _pad_cols(b_w_in[j][:, _b_in_perm()], B_E).astype(BF16)
            proj = in_projection(x, mod, norm_pre[i], w_in, bt=bt, tt=tt, tn=1024)
            if cache is None:
                o, st = mixer_b_prompt(proj, b_pos_emb[j], b_w_cmp[j])
            else:
                o, st = mixer_b_step(proj, b_pos_emb[j], b_w_cmp[j], *cache)
            gate_col, w_out = 1, b_w_out[j][_b_out_perm(), :]
        else:
            w = c_w_in[j]
            w_in = jnp.concatenate([w[:, :3072], w[:, 3088:4112], _pad_cols(w[:, 3072:3088], 128)],
                                   axis=1).astype(BF16)
            proj = in_projection(x, mod, norm_pre[i], w_in, bt=bt, tt=tt, tn=C_E // 3)
            if cache is None:
                o, st = mixer_c_prompt(proj, c_b_forget[j])
            else:
                o, st = mixer_c_step(proj, c_b_forget[j], *cache)
            gate_col, w_out = 3, c_w_out[j]
        x = out_projection(o, proj, gate_col, x, mod, norm_post[i], w_out.astype(BF16), bt=bt, tt=tt)
        states.append(st)
    return x, states


def stack_state(states, kind, k):
    return jnp.stack([states[i][k] for i in range(DEPTH) if i % N_MIXERS == kind])


def kernel(x_prompt, x_sample, c_prompt, c_sample, cache_a_w128, cache_a_w512, cache_a_w2048,
           cache_b_kv, cache_b_win, cache_c_kv, cache_c_logf, page_table,
           ada_w, ada_b, norm_pre, norm_post, a_w_in, a_w_out, b_w_in, b_pos_emb, b_w_cmp, b_w_out,
           c_w_in, c_b_forget, c_w_out):
    weights = (norm_pre, norm_post, a_w_in, a_w_out, b_w_in, b_pos_emb, b_w_cmp, b_w_out,
               c_w_in, c_b_forget, c_w_out)
    nbp, nbs = x_prompt.shape[0], x_sample.shape[0]
    nb_pad = -(-(nbp + nbs) // 8) * 8
    c_all = jnp.concatenate([c_prompt, c_sample,
                             jnp.zeros((nb_pad - nbp - nbs, D_MODEL), F32)], axis=0)
    mods = ada_modulation(c_all, ada_w, ada_b)
    mods_p = mods[:, :nbp]
    mods_s = mods[:, nbp:nbp + nbs]

    def prompt_cache(i):
        return None

    def sample_cache(i):
        kind, j = i % N_MIXERS, i // N_MIXERS
        if kind == 0:
            return [cache_a_w128, cache_a_w512, cache_a_w2048]
        if kind == 1:
            return [cache_b_kv[j], cache_b_win[j], page_table]
        return [cache_c_kv[j], cache_c_logf[j], page_table]

    y_prompt, st_p = run_trunk(x_prompt, mods_p, prompt_cache, weights, bt=1, tt=1024)
    y_sample, st_s = run_trunk(x_sample, mods_s, sample_cache, weights, bt=nbs, tt=x_sample.shape[1])

    outs = [y_prompt, y_sample]
    outs += [stack_state(st_p, 0, 0), stack_state(st_p, 0, 1), stack_state(st_p, 0, 2),
             stack_state(st_p, 1, 0), stack_state(st_p, 1, 1),
             stack_state(st_p, 2, 0), stack_state(st_p, 2, 1)]
    layers = lambda kind: [i for i in range(DEPTH) if i % N_MIXERS == kind]
    outs += [roll_a_caches(c, [st_s[i][g] for i in layers(0)])
             for g, c in enumerate((cache_a_w128, cache_a_w512, cache_a_w2048))]
    outs += [stack_state(st_s, 1, 0), roll_b_window(cache_b_win, [st_s[i][1] for i in layers(1)]),
             stack_state(st_s, 2, 0), stack_state(st_s, 2, 1)]
    return tuple(outs)
```

```python
import functools

import jax
import jax.numpy as jnp
import numpy as np
from jax import lax
from jax.experimental import pallas as pl
from jax.experimental.pallas import tpu as pltpu

D_MODEL = 1024
DEPTH = 4
N_MIXERS = 3
HEAD_DIM = 64
RMS_EPS = 1e-6
N_HEADS = D_MODEL // HEAD_DIM
A_GROUPS = ((128, 1), (512, 4), (2048, 16))
B_KV_HEADS = N_HEADS // 4
B_KV_WIDTH = B_KV_HEADS * HEAD_DIM
B_BLOCK = 64
B_TOPK = 16
B_WINDOW = 512
PAGE_SIZE = 128

VMEM_LIMIT = 56 * 1024 * 1024

F32 = jnp.float32
BF16 = jnp.bfloat16


def _sigmoid(x):
    return 1.0 / (1.0 + jnp.exp(-x))


def _ada_kernel(c_ref, w_ref, b_ref, o_ref):
    c = c_ref[...]
    s = c * _sigmoid(c)
    o_ref[...] = jnp.dot(s, w_ref[...], preferred_element_type=F32,
                         precision=lax.Precision.HIGHEST) + b_ref[...]


def ada_modulation(c_all, ada_w, ada_b):
    nb = c_all.shape[0]
    tn = 1024
    return pl.pallas_call(
        _ada_kernel,
        out_shape=jax.ShapeDtypeStruct((DEPTH, nb, 3 * D_MODEL), F32),
        grid=(DEPTH, 3 * D_MODEL // tn),
        in_specs=[
            pl.BlockSpec((nb, D_MODEL), lambda i, n: (0, 0)),
            pl.BlockSpec((None, D_MODEL, tn), lambda i, n: (i, 0, n)),
            pl.BlockSpec((None, 1, tn), lambda i, n: (i, 0, n)),
        ],
        out_specs=pl.BlockSpec((None, nb, tn), lambda i, n: (i, 0, n)),
        compiler_params=pltpu.CompilerParams(vmem_limit_bytes=VMEM_LIMIT),
        name="ada_modulation",
    )(c_all, ada_w, ada_b.reshape(DEPTH, 1, 3 * D_MODEL))


def _inproj_kernel(x_ref, shift_ref, scale_ref, g_ref, w_ref, o_ref, h_ref):
    bt, tt, d = x_ref.shape

    @pl.when(pl.program_id(1) == 0)
    def _():
        x = x_ref[...]
        ms = jnp.mean(x * x, axis=-1, keepdims=True)
        y = x * lax.rsqrt(ms + RMS_EPS) * g_ref[...]
        h = y * (1.0 + scale_ref[...]) + shift_ref[...]
        h_ref[...] = h.reshape(bt * tt, d).astype(BF16)

    o_ref[...] = jnp.dot(h_ref[...], w_ref[...],
                         preferred_element_type=F32).reshape(o_ref.shape)


def in_projection(x, mod, g, w_bf16, *, bt, tt, tn):
    b, t, d = x.shape
    e = w_bf16.shape[1]
    assert b % bt == 0 and t % tt == 0 and e % tn == 0
    nt = t // tt
    return pl.pallas_call(
        _inproj_kernel,
        out_shape=jax.ShapeDtypeStruct((b, t, e), F32),
        grid=(b // bt * nt, e // tn),
        in_specs=[
            pl.BlockSpec((bt, tt, d), lambda m, n: (m // nt, m % nt, 0)),
            pl.BlockSpec((bt, 1, d), lambda m, n: (m // nt, 0, 0)),
            pl.BlockSpec((bt, 1, d), lambda m, n: (m // nt, 0, 1)),
            pl.BlockSpec((1, d), lambda m, n: (0, 0)),
            pl.BlockSpec((d, tn), lambda m, n: (0, n)),
        ],
        out_specs=pl.BlockSpec((bt, tt, tn), lambda m, n: (m // nt, m % nt, n)),
        scratch_shapes=[pltpu.VMEM((bt * tt, d), BF16)],
        compiler_params=pltpu.CompilerParams(
            dimension_semantics=("arbitrary", "arbitrary"),
            vmem_limit_bytes=VMEM_LIMIT),
        name="in_projection",
    )(x, mod, mod, g.reshape(1, d), w_bf16)


def _outproj_kernel(o_ref, gate_ref, x_ref, mg_ref, g_ref, w_ref, out_ref):
    bt, tt, d = x_ref.shape
    gt = gate_ref[...]
    og = (o_ref[...] * (gt * _sigmoid(gt))).reshape(bt * tt, d).astype(BF16)
    y = jnp.dot(og, w_ref[...], preferred_element_type=F32)
    ms = jnp.mean(y * y, axis=-1, keepdims=True)
    yn = (y * lax.rsqrt(ms + RMS_EPS) * g_ref[...]).reshape(bt, tt, d)
    out_ref[...] = x_ref[...] + mg_ref[...] * yn


def out_projection(o, proj, gate_col, x, mod, g, w_bf16, *, bt, tt):
    b, t, d = x.shape
    nt = t // tt
    return pl.pallas_call(
        _outproj_kernel,
        out_shape=jax.ShapeDtypeStruct((b, t, d), F32),
        grid=(b // bt * nt,),
        in_specs=[
            pl.BlockSpec((bt, tt, d), lambda m: (m // nt, m % nt, 0)),
            pl.BlockSpec((bt, tt, d), lambda m: (m // nt, m % nt, gate_col)),
            pl.BlockSpec((bt, tt, d), lambda m: (m // nt, m % nt, 0)),
            pl.BlockSpec((bt, 1, d), lambda m: (m // nt, 0, 2)),
            pl.BlockSpec((1, d), lambda m: (0, 0)),
            pl.BlockSpec((d, d), lambda m: (0, 0)),
        ],
        out_specs=pl.BlockSpec((bt, tt, d), lambda m: (m // nt, m % nt, 0)),
        compiler_params=pltpu.CompilerParams(
            dimension_semantics=("arbitrary",),
            vmem_limit_bytes=VMEM_LIMIT),
        name="out_projection",
    )(o, proj, x, mod, g.reshape(1, d), w_bf16)


MXU = BF16
NEG = -1e30
HIGHEST = lax.Precision.HIGHEST


def _slope(h):
    return float(np.float32(2.0 ** (-8.0 * (h + 1) / N_HEADS)))


def _dot_nt(a, b):
    return lax.dot_general(a, b, (((1,), (1,)), ((), ())), preferred_element_type=F32)


def _dot_tn(a, b):
    return lax.dot_general(a, b, (((0,), (0,)), ((), ())), preferred_element_type=F32)


def _rowmax(s):
    m = s[:, 0:128]
    for c in range(1, s.shape[1] // 128):
        m = jnp.maximum(m, s[:, c * 128:(c + 1) * 128])
    return jnp.max(m, axis=1, keepdims=True)


B_E = 4096
B_KV_OFF = 2048
B_BG_OFF = 3584


def _b_in_perm():
    perm = np.zeros(3632, np.int64)
    for r in range(4):
        for g in range(4):
            h = 4 * g + r
            for d in range(64):
                perm[r * 256 + g * 64 + d] = h * 64 + d
                perm[1024 + r * 256 + g * 64 + d] = 2608 + h * 64 + d
    perm[2048:3584] = 1024 + np.arange(1536)
    for br in range(3):
        for r in range(4):
            for g in range(4):
                perm[3584 + br * 16 + r * 4 + g] = 2560 + (4 * g + r) * 3 + br
    return perm


def _b_out_perm():
    perm = np.zeros(1024, np.int64)
    for r in range(4):
        for g in range(4):
            for d in range(64):
                perm[r * 256 + g * 64 + d] = (4 * g + r) * 64 + d
    return perm


def _nsa_cmp_kernel(x_ref, pe_ref, w_ref, o_ref):
    tt = x_ref.shape[0]
    nb = tt // B_BLOCK
    x = x_ref[...].reshape(nb, B_BLOCK, 512) + pe_ref[...][None]
    pooled = jnp.sum(x, axis=1) * (1.0 / B_BLOCK)
    o_ref[...] = jnp.dot(pooled, w_ref[...], preferred_element_type=F32, precision=HIGHEST)


def _nsa_cmp_weights(pos_emb, w_cmp):
    pe = pos_emb.transpose(1, 0, 2, 3).reshape(B_BLOCK, 512)
    wbd = jnp.zeros((512, 512), F32)
    for c in range(2):
        for g in range(4):
            o = c * 256 + g * 64
            wbd = wbd.at[o:o + 64, o:o + 64].set(w_cmp[c, g])
    return pe, wbd


def nsa_compress_prompt(proj, pe, wbd, *, tt=512):
    b, t, _ = proj.shape
    return pl.pallas_call(
        _nsa_cmp_kernel,
        out_shape=jax.ShapeDtypeStruct((b, t // B_BLOCK, 512), F32),
        grid=(b, t // tt),
        in_specs=[
            pl.BlockSpec((None, tt, 512), lambda i, j: (i, j, B_KV_OFF // 512)),
            pl.BlockSpec((B_BLOCK, 512), lambda i, j: (0, 0)),
            pl.BlockSpec((512, 512), lambda i, j: (0, 0)),
        ],
        out_specs=pl.BlockSpec((None, tt // B_BLOCK, 512), lambda i, j: (i, j, 0)),
        compiler_params=pltpu.CompilerParams(vmem_limit_bytes=VMEM_LIMIT),
        name="nsa_compress_prompt",
    )(proj, pe, wbd)


NSA_TK = 512


def _nsa_prompt_kernel(q_ref, ks_ref, vs_ref, kw_ref, vw_ref, cmp_ref, bg_ref, ex_ref, o_ref,
                       ksb, kwb, vsg, vwg, q4_ref, sel_ref, wmask_ref, oc_ref, m_ref, acc_ref, *, tq, t):
    qi = pl.program_id(1)
    nblk = t // B_BLOCK
    tk = NSA_TK
    q0 = qi * tq
    lane = lax.broadcasted_iota(jnp.int32, (1, 256), 1)
    gmasks = [(lane >= 64 * g) & (lane < 64 * (g + 1)) for g in range(4)]

    @pl.when(qi == 0)
    def _():
        ksb[...] = ks_ref[...].astype(MXU)
        kwb[...] = kw_ref[...].astype(MXU)
        ones = jnp.ones((t, 64), F32)
        for g in range(4):
            vsg[g] = jnp.concatenate([vs_ref[:, 64 * g:64 * (g + 1)], ones], axis=1).astype(MXU)
            vwg[g] = jnp.concatenate([vw_ref[:, 64 * g:64 * (g + 1)], ones], axis=1).astype(MXU)

    posq = q0 + lax.broadcasted_iota(jnp.int32, (1, tq), 1)
    posc = q0 + lax.broadcasted_iota(jnp.int32, (tq, 1), 0)
    bidx = lax.broadcasted_iota(jnp.int32, (nblk, 1), 0)
    blk_end = (bidx + 1) * B_BLOCK - 1
    cur = posq // B_BLOCK
    forced = (bidx == 0) | (bidx >= cur - 1)
    kc = cmp_ref[:, 0:256].astype(MXU)
    vc = cmp_ref[:, 256:512].astype(MXU)
    distc = (posq - blk_end).astype(F32)
    validc = blk_end <= posq

    for g in range(4):
        q4 = jnp.concatenate(
            [jnp.where(gmasks[g], q_ref[:, r * 256:(r + 1) * 256], 0.0) for r in range(4)], axis=0)
        q4_ref[g] = (q4 * (HEAD_DIM ** -0.5)).astype(MXU)

        s_ct = _dot_nt(kc, q4_ref[g])
        p_parts = []
        for r in range(4):
            s_r = s_ct[:, r * tq:(r + 1) * tq] - _slope(4 * g + r) * distc
            sm = jnp.where(validc, s_r, -jnp.inf)
            mx = jnp.max(sm, axis=0, keepdims=True)
            mx = jnp.where(mx > -jnp.inf, mx, 0.0)
            e = jnp.exp(sm - mx)
            den = jnp.sum(e, axis=0, keepdims=True)
            p_parts.append(e / jnp.where(den > 0, den, 1.0))
        oc_ref[g] = _dot_tn(jnp.concatenate(p_parts, axis=1).astype(MXU), vc)

        sc = p_parts[0] + p_parts[1] + p_parts[2] + p_parts[3]
        score = jnp.where(bidx > cur, -jnp.inf, jnp.where(forced, jnp.inf, sc))
        rank = jnp.zeros((nblk, tq), F32)
        for n in range(nblk):
            row = score[n:n + 1, :]
            beats = (row > score) | ((row == score) & (bidx > n))
            rank = rank + jnp.where(beats, 1.0, 0.0)
        sel_ref[g] = jnp.where(rank < float(min(B_TOPK, nblk)), 1.0, 0.0).T.astype(MXU)

    def tile(off, k_b, vg_ref, mask_fn):
        kt = k_b[pl.ds(off, tk), :]
        kq = (off - q0 + lax.broadcasted_iota(jnp.int32, (1, tk), 1)).astype(F32)
        ss = []
        for g in range(4):
            s = _dot_nt(q4_ref[g], kt)
            madd = mask_fn(g)
            ss.append(jnp.concatenate(
                [s[r * tq:(r + 1) * tq] + _slope(4 * g + r) * kq + madd for r in range(4)], axis=0))
        m_olds = [m_ref[g] for g in range(4)]
        m_news = [jnp.maximum(m_olds[g], _rowmax(ss[g])) for g in range(4)]
        ps = [jnp.exp(ss[g] - m_news[g]).astype(MXU) for g in range(4)]
        pvs = [jnp.dot(ps[g], vg_ref[g, pl.ds(off, tk), :], preferred_element_type=F32) for g in range(4)]
        for g in range(4):
            acc_ref[g] = jnp.exp(m_olds[g] - m_news[g]) * acc_ref[g] + pvs[g]
            m_ref[g] = m_news[g]

    def branch(k_b, vg_ref, j_lo, j_hi, mask_fn, last_mask_fn):
        m_ref[...] = jnp.full(m_ref.shape, NEG, F32)
        acc_ref[...] = jnp.zeros(acc_ref.shape, F32)

        def body(j, c):
            off = pl.multiple_of(j * tk, tk)
            tile(off, k_b, vg_ref, lambda g: mask_fn(g, off, j))
            return c

        lax.fori_loop(j_lo, j_hi, body, 0)
        off = pl.multiple_of(j_hi * tk, tk)
        tile(off, k_b, vg_ref, lambda g: last_mask_fn(g, off, j_hi))
        return [acc_ref[g][:, 0:64] / acc_ref[g][:, 64:65] for g in range(4)]

    j_hi = (q0 + tq - 1) // tk
    kcol = lax.broadcasted_iota(jnp.int32, (1, tk), 1)

    def sel_madd(g, off, j):
        sel_keys = jnp.dot(sel_ref[g], ex_ref[:, pl.ds(off, tk)], preferred_element_type=F32)
        return (sel_keys - 1.0) * (-NEG)

    def sel_madd_last(g, off, j):
        return sel_madd(g, off, j) + jnp.where(off + kcol <= posc, 0.0, NEG)

    o_s = branch(ksb, vsg, 0, j_hi, sel_madd, sel_madd_last)

    j_lo = jnp.maximum(q0 - B_WINDOW, 0) // tk
    for c in range(wmask_ref.shape[1] // tk):
        dist = posc - ((j_lo + c) * tk + kcol)
        wmask_ref[:, c * tk:(c + 1) * tk] = jnp.where((dist >= 0) & (dist <= B_WINDOW), 0.0, NEG)

    def win_madd(g, off, j):
        return wmask_ref[:, pl.ds(pl.multiple_of((j - j_lo) * tk, tk), tk)]

    o_w = branch(kwb, vwg, j_lo, j_hi, win_madd, win_madd)

    bgate = _sigmoid(bg_ref[...])
    for r in range(4):
        rs = slice(r * tq, (r + 1) * tq)
        parts = []
        for g in range(4):
            c0 = r * 4 + g
            parts.append(bgate[:, c0:c0 + 1] * oc_ref[g][rs, 64 * g:64 * (g + 1)]
                         + bgate[:, 16 + c0:17 + c0] * o_s[g][rs] + bgate[:, 32 + c0:33 + c0] * o_w[g][rs])
        o_ref[:, r * 256:(r + 1) * 256] = jnp.concatenate(parts, axis=1)


def _nsa_expand(t):
    nblk = t // B_BLOCK
    ex = (np.arange(t)[None, :] // B_BLOCK == np.arange(nblk)[:, None])
    return jnp.asarray(ex, MXU)


def nsa_attention_prompt(proj, cmp, *, tq=128):
    b, t, _ = proj.shape
    assert tq == 128 and t % tq == 0
    kv = lambda c: pl.BlockSpec((None, t, 256), lambda i, j, c=c: (i, 0, B_KV_OFF // 256 + c),
                                pipeline_mode=pl.Buffered(1))
    return pl.pallas_call(
        functools.partial(_nsa_prompt_kernel, tq=tq, t=t),
        out_shape=jax.ShapeDtypeStruct((b, t, D_MODEL), F32),
        grid=(b, t // tq),
        in_specs=[
            pl.BlockSpec((None, tq, 1024), lambda i, j: (i, j, 0)),
            kv(2), kv(3), kv(4), kv(5),
            pl.BlockSpec((None, t // B_BLOCK, 512), lambda i, j: (i, 0, 0)),
            pl.BlockSpec((None, tq, 128), lambda i, j: (i, j, B_BG_OFF // 128)),
            pl.BlockSpec((t // B_BLOCK, t), lambda i, j: (0, 0)),
        ],
        out_specs=pl.BlockSpec((None, tq, D_MODEL), lambda i, j: (i, j, 0)),
        scratch_shapes=[
            pltpu.VMEM((t, 256), MXU), pltpu.VMEM((t, 256), MXU),
            pltpu.VMEM((4, t, 128), MXU), pltpu.VMEM((4, t, 128), MXU),
            pltpu.VMEM((4, 4 * tq, 256), MXU),
            pltpu.VMEM((4, tq, t // B_BLOCK), MXU),
            pltpu.VMEM((tq, -(-(B_WINDOW + tq + NSA_TK - 128) // NSA_TK) * NSA_TK), F32),
            pltpu.VMEM((4, 4 * tq, 256), F32),
            pltpu.VMEM((4, 4 * tq, 1), F32),
            pltpu.VMEM((4, 4 * tq, 128), F32)],
        compiler_params=pltpu.CompilerParams(
            dimension_semantics=("arbitrary", "arbitrary"),
            vmem_limit_bytes=VMEM_LIMIT),
        name="nsa_attention_prompt",
    )(proj, proj, proj, proj, proj, cmp, proj, _nsa_expand(t))


def mixer_b_prompt(proj, pos_emb, w_cmp):
    b, t, _ = proj.shape
    pe, wbd = _nsa_cmp_weights(pos_emb, w_cmp)
    cmp = nsa_compress_prompt(proj, pe, wbd)
    o = nsa_attention_prompt(proj, cmp)
    kv_full_new = proj[..., B_KV_OFF:B_KV_OFF + 1024].reshape(b, t, 4, B_KV_HEADS, HEAD_DIM)
    new_win = proj[:, t - min(B_WINDOW, t):, B_KV_OFF + 1024:B_KV_OFF + 1536].reshape(
        b, min(B_WINDOW, t), 2, B_KV_HEADS, HEAD_DIM)
    return o, [kv_full_new, new_win]


C_E = 4224
C_F_OFF = 4096


def _log_sigmoid(x):
    return jnp.minimum(x, 0.0) - jnp.log(1.0 + jnp.exp(-jnp.abs(x)))


def _fox_prep_kernel(x_ref, bf_ref, logft_ref, cumt_ref, carry_ref):
    @pl.when(pl.program_id(1) == 0)
    def _():
        carry_ref[...] = jnp.zeros(carry_ref.shape, F32)

    n = x_ref.shape[0]
    logf = _log_sigmoid(x_ref[...] + bf_ref[...])
    tri = jnp.where(lax.broadcasted_iota(jnp.int32, (n, n), 1) <= lax.broadcasted_iota(jnp.int32, (n, n), 0),
                    1.0, 0.0)
    cum = jnp.dot(tri, logf, preferred_element_type=F32, precision=HIGHEST) + carry_ref[...]
    logft_ref[...] = logf.T[0:N_HEADS, :]
    cumt_ref[...] = cum.T[0:N_HEADS, :]
    carry_ref[...] = cum[n - 1:n, :]


def fox_prep(proj, b_forget, *, tt=128):
    b, t, _ = proj.shape
    bf = jnp.pad(b_forget.astype(F32), (0, 128 - N_HEADS)).reshape(1, 128)
    return pl.pallas_call(
        _fox_prep_kernel,
        out_shape=[jax.ShapeDtypeStruct((b, N_HEADS, t), F32), jax.ShapeDtypeStruct((b, N_HEADS, t), F32)],
        grid=(b, t // tt),
        in_specs=[pl.BlockSpec((None, tt, 128), lambda i, j: (i, j, C_F_OFF // 128)),
                  pl.BlockSpec((1, 128), lambda i, j: (0, 0))],
        out_specs=[pl.BlockSpec((None, N_HEADS, tt), lambda i, j: (i, 0, j)),
                   pl.BlockSpec((None, N_HEADS, tt), lambda i, j: (i, 0, j))],
        scratch_shapes=[pltpu.VMEM((1, 128), F32)],
        compiler_params=pltpu.CompilerParams(
            dimension_semantics=("arbitrary", "arbitrary"), vmem_limit_bytes=VMEM_LIMIT),
        name="fox_prep",
    )(proj, bf)


def _fox_prompt_kernel(q_ref, k_ref, v_ref, ck_ref, o_ref, kb, vb, qm_ref, m_ref, acc_ref, *, tq):
    hp = pl.program_id(1)
    qi = pl.program_id(2)

    lane = lax.broadcasted_iota(jnp.int32, (1, 128), 1)
    hmasks = [(lane >= 64 * hh) & (lane < 64 * (hh + 1)) for hh in range(2)]

    @pl.when(qi == 0)
    def _():
        kb[...] = k_ref[...].astype(MXU)
        for hh in range(2):
            vb[hh] = jnp.where(hmasks[hh], v_ref[...], 1.0).astype(MXU)

    for hh in range(2):
        qm_ref[hh] = (jnp.where(hmasks[hh], q_ref[...], 0.0) * (HEAD_DIM ** -0.5)).astype(MXU)
    m_ref[...] = jnp.full(m_ref.shape, NEG, F32)
    acc_ref[...] = jnp.zeros(acc_ref.shape, F32)

    def tile(j, diag):
        off = pl.multiple_of(j * tq, tq)
        kt = kb[pl.ds(off, tq), :]
        for hh in range(2):
            ck = ck_ref[pl.ds(2 * hp + hh, 1), pl.ds(off, tq)]
            s = _dot_nt(qm_ref[hh], kt) - ck
            if diag:
                causal = (lax.broadcasted_iota(jnp.int32, (1, tq), 1)
                          <= lax.broadcasted_iota(jnp.int32, (tq, 1), 0))
                s = jnp.where(causal, s, NEG)
            m_old = m_ref[hh]
            m_new = jnp.maximum(m_old, _rowmax(s))
            p = jnp.exp(s - m_new)
            acc_ref[hh] = jnp.exp(m_old - m_new) * acc_ref[hh] + jnp.dot(
                p.astype(MXU), vb[hh, pl.ds(off, tq), :], preferred_element_type=F32)
            m_ref[hh] = m_new

    def body(j, c):
        tile(j, False)
        return c

    lax.fori_loop(0, qi, body, 0)
    tile(qi, True)
    a0, a1 = acc_ref[0], acc_ref[1]
    o_ref[...] = jnp.where(lane < 64, a0 / a0[:, 64:65], a1 / a1[:, 0:1])


def fox_attention_prompt(proj, cumt, *, tq=512):
    b, t, _ = proj.shape
    tq = min(tq, t)
    return pl.pallas_call(
        functools.partial(_fox_prompt_kernel, tq=tq),
        out_shape=jax.ShapeDtypeStruct((b, t, D_MODEL), F32),
        grid=(b, N_HEADS // 2, t // tq),
        in_specs=[
            pl.BlockSpec((None, tq, 128), lambda i, p, j: (i, j, p)),
            pl.BlockSpec((None, t, 128), lambda i, p, j: (i, 0, 8 + p)),
            pl.BlockSpec((None, t, 128), lambda i, p, j: (i, 0, 16 + p)),
            pl.BlockSpec((None, N_HEADS, t), lambda i, p, j: (i, 0, 0)),
        ],
        out_specs=pl.BlockSpec((None, tq, 128), lambda i, p, j: (i, j, p)),
        scratch_shapes=[pltpu.VMEM((t, 128), MXU), pltpu.VMEM((2, t, 128), MXU),
                        pltpu.VMEM((2, tq, 128), MXU),
                        pltpu.VMEM((2, tq, 1), F32), pltpu.VMEM((2, tq, 128), F32)],
        compiler_params=pltpu.CompilerParams(
            dimension_semantics=("arbitrary", "arbitrary", "arbitrary"), vmem_limit_bytes=VMEM_LIMIT),
        name="fox_attention_prompt",
    )(proj, proj, proj, cumt)


def mixer_c_prompt(proj, b_forget):
    b, t, _ = proj.shape
    logft, cumt = fox_prep(proj, b_forget)
    o = fox_attention_prompt(proj, cumt)
    kv_new = proj[..., 1024:3072].reshape(b, t, 2, N_HEADS, HEAD_DIM)
    return o, [kv_new, logft.transpose(0, 2, 1)]


A_E = 10240
A_NK = 128
A_CHUNK = 2048
A_UNROLL = 2


def _dil_prompt_fused_kernel(slope_ref, *refs):
    groups = [refs[5 * g:5 * g + 5] for g in range(3)]
    o_ref, kcat, vcat, bias_ref, m0_ref, m1_ref, acc0_ref, acc1_ref = refs[15:]
    m_refs, acc_refs = (m0_ref, m1_ref), (acc0_ref, acc1_ref)
    ci = pl.program_id(2)
    nk = A_NK
    ch = o_ref.shape[0]
    lane = lax.broadcasted_iota(jnp.int32, (1, 128), 1)
    hmasks = [(lane >= 64 * hh) & (lane < 64 * (hh + 1)) for hh in range(2)]
    for hh in range(2):
        m_refs[hh][...] = jnp.full(m_refs[hh].shape, NEG, F32)
        acc_refs[hh][...] = jnp.zeros(acc_refs[hh].shape, F32)
    step = (nk + lax.broadcasted_iota(jnp.int32, (nk, 1), 0)) - lax.broadcasted_iota(jnp.int32, (1, 2 * nk), 1)
    kcol = lax.broadcasted_iota(jnp.int32, (1, 2 * nk), 1)
    in_band = (step >= 0) & (step <= nk)
    slopes = slope_ref[...]
    for g, (q_ref, kc_ref, kp_ref, vc_ref, vp_ref) in enumerate(groups):
        _, dil = A_GROUPS[g]
        pr = nk * dil
        kcat[0:pr, :] = kp_ref[...]
        kcat[pr:pr + ch, :] = kc_ref[...]
        vcat[0:pr, :] = vp_ref[...]
        vcat[pr:pr + ch, :] = vc_ref[...]
        distf = (step * dil).astype(F32)
        for hh in range(2):
            bias_ref[hh] = slopes[:, 64 * hh:64 * hh + 1] * distf

        def body(tj, c, q_ref=q_ref, dil=dil, pr=pr):
            work = []
            for u in range(A_UNROLL):
                ti = tj * A_UNROLL + u
                r = ti % dil
                n = ti // dil
                start = pr * n + r
                qs = pl.ds(start, nk, stride=dil)
                ks = pl.ds(start, 2 * nk, stride=dil)
                q2 = q_ref[qs, :]
                k2 = kcat[ks, :].astype(MXU)
                v2 = vcat[ks, :]
                valid = in_band & ((kcol >= nk) | (n > 0) | (ci > 0))
                for hh in range(2):
                    qm = (jnp.where(hmasks[hh], q2, 0.0) * (HEAD_DIM ** -0.5)).astype(MXU)
                    sm = jnp.where(valid, _dot_nt(qm, k2) - bias_ref[hh], NEG)
                    work.append((hh, qs, sm, v2))
            m_olds = [m_refs[hh][qs, :] for hh, qs, _, _ in work]
            m_news = [jnp.maximum(mo, _rowmax(sm)) for mo, (_, _, sm, _) in zip(m_olds, work)]
            ps = [jnp.exp(sm - mn).astype(MXU) for mn, (_, _, sm, _) in zip(m_news, work)]
            pvs = [jnp.dot(p, jnp.where(hmasks[hh], v2, 1.0).astype(MXU), preferred_element_type=F32)
                   for p, (hh, _, _, v2) in zip(ps, work)]
            for mo, mn, pv, (hh, qs, _, _) in zip(m_olds, m_news, pvs, work):
                acc_refs[hh][qs, :] = jnp.exp(mo - mn) * acc_refs[hh][qs, :] + pv
                m_refs[hh][qs, :] = mn
            return c

        lax.fori_loop(0, ch // nk // A_UNROLL, body, 0)
    a0, a1 = acc0_ref[...], acc1_ref[...]
    o_ref[...] = jnp.where(lane < 64, a0 / a0[:, 64:65], a1 / a1[:, 0:1])


def dilated_attention_prompt(proj):
    b, t, _ = proj.shape
    nk = A_NK
    ch = min(A_CHUNK, t)
    assert t % ch == 0 and all(w // d == nk and ch % (nk * d) == 0 for w, d in A_GROUPS)
    sl = np.float32(2.0 ** (-8.0 * np.arange(1, N_HEADS + 1) / N_HEADS))
    slope_rows = jnp.asarray(np.repeat(sl, 64).reshape(N_HEADS // 2, 1, 128), F32)
    in_specs = [pl.BlockSpec((None, 1, 128), lambda bi, hp, ci: (hp, 0, 0))]
    for g, (_, dil) in enumerate(A_GROUPS):
        pr = nk * dil
        cur = lambda part, g=g: pl.BlockSpec(
            (None, ch, 128), lambda bi, hp, ci, part=part, g=g: (bi, ci, (3072 * g + 1024 * part) // 128 + hp))
        prev = lambda part, g=g, pr=pr: pl.BlockSpec(
            (None, pr, 128), lambda bi, hp, ci, part=part, g=g, pr=pr:
            (bi, jnp.maximum(ci * (ch // pr) - 1, 0), (3072 * g + 1024 * part) // 128 + hp))
        in_specs += [cur(0), cur(1), prev(1), cur(2), prev(2)]
    return pl.pallas_call(
        _dil_prompt_fused_kernel,
        out_shape=jax.ShapeDtypeStruct((b, t, D_MODEL), F32),
        grid=(b, N_HEADS // 2, t // ch),
        in_specs=in_specs,
        out_specs=pl.BlockSpec((None, ch, 128), lambda bi, hp, ci: (bi, ci, hp)),
        scratch_shapes=[pltpu.VMEM((nk * A_GROUPS[-1][1] + ch, 128), F32),
                        pltpu.VMEM((nk * A_GROUPS[-1][1] + ch, 128), F32),
                        pltpu.VMEM((2, nk, 2 * nk), F32),
                        pltpu.VMEM((ch, 1), F32), pltpu.VMEM((ch, 1), F32),
                        pltpu.VMEM((ch, 128), F32), pltpu.VMEM((ch, 128), F32)],
        compiler_params=pltpu.CompilerParams(
            dimension_semantics=("arbitrary", "arbitrary", "arbitrary"), vmem_limit_bytes=VMEM_LIMIT),
        name="dilated_attention_prompt",
    )(slope_rows, *([proj] * 15))


def mixer_a_prompt(proj):
    b, t, _ = proj.shape
    o = dilated_attention_prompt(proj)
    new_state = []
    for g, (window, _) in enumerate(A_GROUPS):
        w = min(window, t)
        new_state.append(proj[:, t - w:, 3072 * g + 1024:3072 * g + 3072].reshape(b, w, 2, N_HEADS, HEAD_DIM))
    return o, new_state


STEP_T = 8
STEP_LANES = N_HEADS * STEP_T


def _tile_rows(x, n):
    return jnp.concatenate([x] * n, axis=0)


def _dot_hi(a, b):
    return jnp.dot(a, b, preferred_element_type=F32, precision=HIGHEST)


def _roll_kernel(c_ref, n_ref, o_ref):
    w = c_ref.shape[1]
    lane = lax.broadcasted_iota(jnp.int32, (1, 128), 1)
    place = jnp.where(lax.broadcasted_iota(jnp.int32, (STEP_T, 1), 0) == lane - (128 - STEP_T), 1.0, 0.0)
    new_t = lax.dot_general(n_ref[...], place, (((0,), (0,)), ((), ())), preferred_element_type=F32,
                            precision=HIGHEST)
    rolled = pltpu.roll(c_ref[...], w - STEP_T, axis=1)
    if w > 128:
        o_ref[:, 0:w - 128] = rolled[:, 0:w - 128]
    o_ref[:, w - 128:w] = jnp.where(lane >= 128 - STEP_T, new_t, rolled[:, w - 128:w])


def cache_roll(cache_t, new_rows, *, rb=512):
    nl, b, r, w = cache_t.shape
    return pl.pallas_call(
        _roll_kernel,
        out_shape=jax.ShapeDtypeStruct(cache_t.shape, F32),
        grid=(nl, b, r // rb),
        in_specs=[pl.BlockSpec((None, None, rb, w), lambda l, i, k: (l, i, k, 0)),
                  pl.BlockSpec((None, None, STEP_T, rb), lambda l, i, k: (l, i, 0, k))],
        out_specs=pl.BlockSpec((None, None, rb, w), lambda l, i, k: (l, i, k, 0)),
        compiler_params=pltpu.CompilerParams(
            dimension_semantics=("arbitrary", "arbitrary", "arbitrary"), vmem_limit_bytes=VMEM_LIMIT),
        name="cache_roll",
    )(cache_t, new_rows)


def _token_minor(x, token_axis):
    perm = [a for a in range(x.ndim) if a != token_axis] + [token_axis]
    return jnp.transpose(x, perm)


def _token_major(x, token_axis):
    perm = list(range(x.ndim - 1))
    perm.insert(token_axis, x.ndim - 1)
    return jnp.transpose(x, perm)


def _dil_step_kernel(slope_ref, new_ref, c0_ref, c1_ref, c2_ref, o_ref):
    hg = pl.program_id(1)
    rows = 4 * STEP_T
    rowi = lax.broadcasted_iota(jnp.int32, (rows, 1), 0) & (STEP_T - 1)
    bd = (lax.broadcasted_iota(jnp.int32, (rows, 1), 0) >> 3) == (lax.broadcasted_iota(jnp.int32, (1, 256), 1) >> 6)
    slope = slope_ref[...]
    pieces = []
    for g, c_ref in enumerate((c0_ref, c1_ref, c2_ref)):
        window, dil = A_GROUPS[g]
        col = lambda part: pl.ds(pl.multiple_of(3072 * g + 1024 * part + hg * 256, 256), 256)
        qbd = (jnp.where(bd, _tile_rows(new_ref[:, col(0)], 4), 0.0) * (HEAD_DIM ** -0.5)).astype(MXU)
        s = jnp.dot(qbd, c_ref[0].astype(MXU), preferred_element_type=F32)
        dist = (window + rowi) - lax.broadcasted_iota(jnp.int32, (1, window), 1)
        valid = (dist <= window) & ((dist & (dil - 1)) == 0)
        pieces.append((jnp.where(valid, s - slope * dist.astype(F32), NEG), c_ref[1].astype(MXU), True))
        sn = _dot_nt(qbd, new_ref[:, col(1)].astype(MXU))
        distn = rowi - lax.broadcasted_iota(jnp.int32, (1, STEP_T), 1)
        validn = (distn >= 0) & ((distn & (dil - 1)) == 0)
        pieces.append((jnp.where(validn, sn - slope * distn.astype(F32), NEG),
                       new_ref[:, col(2)].astype(MXU), False))
    m = pieces[0][0][:, 0:1]
    for s, _, _ in pieces:
        m = jnp.maximum(m, jnp.max(s, axis=1, keepdims=True))
    den = jnp.zeros((rows, 1), F32)
    acc = jnp.zeros((rows, 256), F32)
    for s, v, transposed in pieces:
        p = jnp.exp(s - m)
        den = den + jnp.sum(p, axis=1, keepdims=True)
        pb = p.astype(MXU)
        acc = acc + (_dot_nt(pb, v) if transposed else jnp.dot(pb, v, preferred_element_type=F32))
    o = jnp.where(bd, acc / den, 0.0)
    o_ref[...] = jnp.sum(o.reshape(4, STEP_T, 256), axis=0)


def mixer_a_step(proj, caches, j):
    b, t, _ = proj.shape
    assert t == STEP_T
    views = [_token_minor(c, 2).reshape(c.shape[0], b, 2, D_MODEL, c.shape[2]) for c in caches]
    slopes = np.float32(2.0 ** (-8.0 * np.arange(1, N_HEADS + 1) / N_HEADS))
    slope_rows = jnp.asarray(np.repeat(slopes, STEP_T).reshape(4, 4 * STEP_T, 1), F32)
    cspec = lambda w: pl.BlockSpec((None, None, 2, 256, w), lambda i, h: (j, i, 0, h, 0))
    o = pl.pallas_call(
        _dil_step_kernel,
        out_shape=jax.ShapeDtypeStruct((b, t, D_MODEL), F32),
        grid=(b, 4),
        in_specs=[pl.BlockSpec((None, 4 * STEP_T, 1), lambda i, h: (h, 0, 0)),
                  pl.BlockSpec((None, t, A_E), lambda i, h: (i, 0, 0))]
        + [cspec(v.shape[-1]) for v in views],
        out_specs=pl.BlockSpec((None, t, 256), lambda i, h: (i, 0, h)),
        compiler_params=pltpu.CompilerParams(
            dimension_semantics=("arbitrary", "arbitrary"), vmem_limit_bytes=VMEM_LIMIT),
        name="dilated_attention_step",
    )(slope_rows, proj, *views)
    return o, [proj[..., 3072 * g + 1024:3072 * g + 3072] for g in range(3)]


def roll_a_caches(caches, new_rows):
    nl, b, w = caches.shape[:3]
    ct = _token_minor(caches, 2).reshape(nl, b, 2 * D_MODEL, w)
    out = cache_roll(ct, jnp.stack(new_rows))
    return _token_major(out.reshape(nl, b, 2, N_HEADS, HEAD_DIM, w), 2)


C_STEP_PAGES = 8


def _fox_step_kernel(pt_ref, new_ref, bf_ref, *refs):
    npg = C_STEP_PAGES
    kv_refs = refs[:npg]
    lf_refs = refs[npg:2 * npg]
    o_ref, lfo_ref, qbd_ref, m_ref, l_ref, acc_ref, carry_ref = refs[2 * npg:]
    s_id = pl.program_id(1)
    rows = 4 * STEP_T
    rowi = lax.broadcasted_iota(jnp.int32, (STEP_LANES, 1), 0) & (STEP_T - 1)
    bd = (lax.broadcasted_iota(jnp.int32, (rows, 1), 0) >> 3) == (lax.broadcasted_iota(jnp.int32, (1, 256), 1) >> 6)
    expand = jnp.where((lax.broadcasted_iota(jnp.int32, (STEP_LANES, 1), 0) >> 3)
                       == lax.broadcasted_iota(jnp.int32, (1, N_HEADS), 1), 1.0, 0.0)

    @pl.when(s_id == 0)
    def _():
        m_ref[...] = jnp.full(m_ref.shape, NEG, F32)
        l_ref[...] = jnp.zeros(l_ref.shape, F32)
        acc_ref[...] = jnp.zeros(acc_ref.shape, F32)
        carry_ref[...] = jnp.zeros(carry_ref.shape, F32)
        for hg in range(4):
            q = new_ref[:, 256 * hg:256 * (hg + 1)]
            qbd_ref[hg] = (jnp.where(bd, _tile_rows(q, 4), 0.0) * (HEAD_DIM ** -0.5)).astype(MXU)

    def update(s, valid, pv_fn):
        if valid is not None:
            s = jnp.where(valid, s, NEG)
        m_old = m_ref[...]
        m_new = jnp.maximum(m_old, _rowmax(s) if s.shape[1] % 128 == 0 else jnp.max(s, axis=1, keepdims=True))
        p = jnp.exp(s - m_new)
        alpha = jnp.exp(m_old - m_new)
        l_ref[...] = alpha * l_ref[...] + jnp.sum(p, axis=1, keepdims=True)
        pb = p.astype(MXU)
        pv = jnp.concatenate([pv_fn(pb[rows * hg:rows * (hg + 1)], hg) for hg in range(4)], axis=0)
        acc_ref[...] = alpha * acc_ref[...] + pv
        m_ref[...] = m_new

    triu = jnp.where(lax.broadcasted_iota(jnp.int32, (PAGE_SIZE, 1), 0)
                     <= lax.broadcasted_iota(jnp.int32, (1, PAGE_SIZE), 1), 1.0, 0.0)
    carry = carry_ref[...]
    cks = []
    for p in range(npg):
        cum = _dot_hi(lf_refs[p][...], triu) + carry
        carry = cum[:, PAGE_SIZE - 1:PAGE_SIZE]
        cks.append(cum)
    carry_ref[...] = carry
    ck = _dot_hi(expand, jnp.concatenate(cks, axis=1))
    s = jnp.concatenate(
        [jnp.dot(qbd_ref[hg],
                 jnp.concatenate([kv_refs[p][0, 256 * hg:256 * (hg + 1), :] for p in range(npg)], axis=1).astype(MXU),
                 preferred_element_type=F32) for hg in range(4)], axis=0) - ck
    update(s, None, lambda pb, hg: _dot_nt(
        pb, jnp.concatenate([kv_refs[p][1, 256 * hg:256 * (hg + 1), :] for p in range(npg)], axis=1).astype(MXU)))

    @pl.when(s_id == pl.num_programs(1) - 1)
    def _():
        logf = _log_sigmoid(new_ref[:, C_F_OFF:C_F_OFF + 128] + bf_ref[...])
        lfo_ref[...] = logf
        tri8 = jnp.where(lax.broadcasted_iota(jnp.int32, (STEP_T, 1), 0)
                         <= lax.broadcasted_iota(jnp.int32, (1, 128), 1), 1.0, 0.0)
        cum_n = lax.dot_general(logf, tri8, (((0,), (0,)), ((), ())), preferred_element_type=F32,
                                precision=HIGHEST)[0:N_HEADS, :] + carry_ref[...]
        ck_n = _dot_hi(expand, cum_n)[:, 0:STEP_T]
        s_n = jnp.concatenate(
            [_dot_nt(qbd_ref[hg], new_ref[:, 1024 + 256 * hg:1024 + 256 * (hg + 1)].astype(MXU))
             for hg in range(4)], axis=0) - ck_n
        valid = lax.broadcasted_iota(jnp.int32, (1, STEP_T), 1) <= rowi
        update(s_n, valid, lambda pb, hg: jnp.dot(
            pb, new_ref[:, 2048 + 256 * hg:2048 + 256 * (hg + 1)].astype(MXU), preferred_element_type=F32))
        o = acc_ref[...] / l_ref[...]
        for hg in range(4):
            om = jnp.where(bd, o[rows * hg:rows * (hg + 1)], 0.0)
            o_ref[:, 256 * hg:256 * (hg + 1)] = jnp.sum(om.reshape(4, STEP_T, 256), axis=0)


def mixer_c_step(proj, b_forget, kv_pool, lf_pool, page_table):
    b, t, _ = proj.shape
    n_pages = page_table.shape[1]
    npg = C_STEP_PAGES
    assert t == STEP_T and n_pages % npg == 0
    kvp = _token_minor(kv_pool, 1).reshape(kv_pool.shape[0], 2, D_MODEL, PAGE_SIZE)
    lfp = _token_minor(lf_pool, 1)
    bf = jnp.pad(b_forget.astype(F32), (0, 128 - N_HEADS)).reshape(1, 128)
    page = lambda shape, p: pl.BlockSpec(shape, lambda i, s, pt, p=p: (pt[i, s * npg + p],) + (0,) * (len(shape) - 1))
    o, logf = pl.pallas_call(
        _fox_step_kernel,
        out_shape=[jax.ShapeDtypeStruct((b, t, D_MODEL), F32), jax.ShapeDtypeStruct((b, t, 128), F32)],
        grid_spec=pltpu.PrefetchScalarGridSpec(
            num_scalar_prefetch=1,
            grid=(b, n_pages // npg),
            in_specs=[pl.BlockSpec((None, t, C_E), lambda i, s, pt: (i, 0, 0)),
                      pl.BlockSpec((1, 128), lambda i, s, pt: (0, 0))]
            + [page((None, 2, D_MODEL, PAGE_SIZE), p) for p in range(npg)]
            + [page((None, N_HEADS, PAGE_SIZE), p) for p in range(npg)],
            out_specs=[pl.BlockSpec((None, t, D_MODEL), lambda i, s, pt: (i, 0, 0)),
                       pl.BlockSpec((None, t, 128), lambda i, s, pt: (i, 0, 0))],
            scratch_shapes=[pltpu.VMEM((4, 4 * STEP_T, 256), MXU),
                            pltpu.VMEM((STEP_LANES, 1), F32), pltpu.VMEM((STEP_LANES, 1), F32),
                            pltpu.VMEM((STEP_LANES, 256), F32), pltpu.VMEM((N_HEADS, 1), F32)]),
        compiler_params=pltpu.CompilerParams(
            dimension_semantics=("arbitrary", "arbitrary"), vmem_limit_bytes=VMEM_LIMIT),
        name="fox_attention_step",
    )(page_table, proj, bf, *([kvp] * npg), *([lfp] * npg))
    kv_new = proj[..., 1024:3072].reshape(b, t, 2, N_HEADS, HEAD_DIM)
    return o, [kv_new, logf[..., :N_HEADS]]


B_STEP_PAGES = 8


def _nsa_cmp_pages_kernel(pt_ref, *refs):
    npg = B_STEP_PAGES
    pages = refs[:npg]
    pet_ref, w_ref, o_ref = refs[npg:]
    nb = npg * PAGE_SIZE // B_BLOCK
    pool = jnp.where(lax.broadcasted_iota(jnp.int32, (npg * PAGE_SIZE, 1), 0) // B_BLOCK
                     == lax.broadcasted_iota(jnp.int32, (1, 128), 1), 1.0 / B_BLOCK, 0.0).astype(MXU)
    outs = []
    for c in range(2):
        x = jnp.concatenate([pages[p][c] + pet_ref[c] for p in range(npg)], axis=1)
        hi = x.astype(MXU)
        lo = (x - hi.astype(F32)).astype(MXU)
        pooled = (jnp.dot(hi, pool, preferred_element_type=F32) + jnp.dot(lo, pool, preferred_element_type=F32))
        cmp_c = lax.dot_general(pooled, w_ref[c], (((0,), (0,)), ((), ())), preferred_element_type=F32,
                                precision=HIGHEST)
        outs.append(cmp_c[0:nb, :])
    o_ref[...] = jnp.concatenate(outs, axis=1)


def nsa_compress_pages(pool_t, page_table, pos_emb, w_cmp):
    b, n_pages = page_table.shape
    npg = B_STEP_PAGES
    nb = npg * PAGE_SIZE // B_BLOCK
    pet = pos_emb.transpose(0, 2, 3, 1).reshape(2, 256, B_BLOCK)
    pet = jnp.concatenate([pet] * (PAGE_SIZE // B_BLOCK), axis=2)
    _, wbd = _nsa_cmp_weights(pos_emb, w_cmp)
    wc = jnp.stack([wbd[0:256, 0:256], wbd[256:512, 256:512]])
    page = lambda p: pl.BlockSpec((None, 2, 256, PAGE_SIZE), lambda i, s, pt, p=p: (pt[i, s * npg + p], 0, 0, 0))
    return pl.pallas_call(
        _nsa_cmp_pages_kernel,
        out_shape=jax.ShapeDtypeStruct((b, n_pages * PAGE_SIZE // B_BLOCK, 512), F32),
        grid_spec=pltpu.PrefetchScalarGridSpec(
            num_scalar_prefetch=1,
            grid=(b, n_pages // npg),
            in_specs=[page(p) for p in range(npg)]
            + [pl.BlockSpec((2, 256, PAGE_SIZE), lambda i, s, pt: (0, 0, 0)),
               pl.BlockSpec((2, 256, 256), lambda i, s, pt: (0, 0, 0))],
            out_specs=pl.BlockSpec((None, nb, 512), lambda i, s, pt: (i, s, 0))),
        compiler_params=pltpu.CompilerParams(
            dimension_semantics=("arbitrary", "arbitrary"), vmem_limit_bytes=VMEM_LIMIT),
        name="nsa_compress_pages",
    )(page_table, *([pool_t] * npg), pet, wc)


def _nsa_step_kernel(pt_ref, slope_ref, new_ref, cmp_ref, win_ref, *refs, past):
    npg = B_STEP_PAGES
    pages = refs[:npg]
    o_ref, qbd_ref, sel_ref, oc_ref, m_ref, l_ref, acc_ref = refs[npg:]
    s_id = pl.program_id(1)
    rows = STEP_LANES
    ridx = lax.broadcasted_iota(jnp.int32, (rows, 1), 0)
    rowi = ridx & (STEP_T - 1)
    rowg = (ridx >> 3) & 3
    colg = lax.broadcasted_iota(jnp.int32, (1, 256), 1) >> 6
    bd = rowg == colg
    slope = slope_ref[...]
    pos = past + rowi
    ncb = past // B_BLOCK
    nsel = sel_ref.shape[1]
    kvo = B_KV_OFF

    def update(s, valid, pv):
        s = jnp.where(valid, s, NEG)
        m_old = m_ref[...]
        m_new = jnp.maximum(m_old, jnp.max(s, axis=1, keepdims=True))
        p = jnp.exp(s - m_new)
        alpha = jnp.exp(m_old - m_new)
        l_ref[...] = alpha * l_ref[...] + jnp.sum(p, axis=1, keepdims=True)
        acc_ref[...] = alpha * acc_ref[...] + pv(p.astype(MXU))
        m_ref[...] = m_new

    @pl.when(s_id == 0)
    def _():
        qbd = jnp.concatenate(
            [jnp.where(bd[0:32], _tile_rows(new_ref[:, 256 * r:256 * (r + 1)], 4), 0.0) for r in range(4)], axis=0)
        qbd_ref[...] = (qbd * (HEAD_DIM ** -0.5)).astype(MXU)
        blk_end = (lax.broadcasted_iota(jnp.int32, (1, ncb), 1) + 1) * B_BLOCK - 1
        s_c = _dot_nt(qbd_ref[...], cmp_ref[:, 0:256].astype(MXU)) - slope * (pos - blk_end).astype(F32)
        sm = jnp.where(blk_end <= pos, s_c, -jnp.inf)
        mx = jnp.max(sm, axis=1, keepdims=True)
        mx = jnp.where(mx > -jnp.inf, mx, 0.0)
        e = jnp.exp(sm - mx)
        den = jnp.sum(e, axis=1, keepdims=True)
        p_c = e / jnp.where(den > 0, den, 1.0)
        oc_ref[...] = jnp.dot(p_c.astype(MXU), cmp_ref[:, 256:512].astype(MXU), preferred_element_type=F32)
        cidx = lax.broadcasted_iota(jnp.int32, (1, rows), 1)
        same = jnp.where((((cidx >> 3) & 3) == rowg) & ((cidx & (STEP_T - 1)) == rowi), 1.0, 0.0)
        sc = _dot_hi(same, p_c)
        sc = jnp.concatenate([sc, jnp.zeros((rows, nsel - ncb), F32)], axis=1)
        nidx = lax.broadcasted_iota(jnp.int32, (1, nsel), 1)
        cur = pos // B_BLOCK
        forced = (nidx == 0) | (nidx >= cur - 1)
        score = jnp.where(nidx > cur, -jnp.inf, jnp.where(forced, jnp.inf, sc))
        rank = jnp.zeros((rows, nsel), F32)
        for n in range(ncb + 1):
            colv = score[:, n:n + 1]
            beats = (colv > score) | ((colv == score) & (nidx > n))
            rank = rank + jnp.where(beats, 1.0, 0.0)
        sel_ref[...] = jnp.where(rank < float(B_TOPK), 1.0, 0.0).astype(MXU)
        m_ref[...] = jnp.full(m_ref.shape, NEG, F32)
        l_ref[...] = jnp.zeros(l_ref.shape, F32)
        acc_ref[...] = jnp.zeros(acc_ref.shape, F32)

    nk = npg * PAGE_SIZE
    kbase = s_id * nk
    kt = jnp.concatenate([pages[p][0] for p in range(npg)], axis=1).astype(MXU)
    vt = jnp.concatenate([pages[p][1] for p in range(npg)], axis=1).astype(MXU)
    kpos = kbase + lax.broadcasted_iota(jnp.int32, (1, nk), 1)
    s = jnp.dot(qbd_ref[...], kt, preferred_element_type=F32) - slope * (pos - kpos).astype(F32)
    ex = jnp.where(lax.broadcasted_iota(jnp.int32, (nsel, 1), 0) == (kpos // B_BLOCK), 1.0, 0.0).astype(MXU)
    selk = jnp.dot(sel_ref[...], ex, preferred_element_type=F32)
    update(s, selk > 0.5, lambda pb: _dot_nt(pb, vt))

    @pl.when(s_id == pl.num_programs(1) - 1)
    def _():
        distn = rowi - lax.broadcasted_iota(jnp.int32, (1, STEP_T), 1)
        sn = _dot_nt(qbd_ref[...], new_ref[:, kvo + 512:kvo + 768].astype(MXU)) - slope * distn.astype(F32)
        vn = new_ref[:, kvo + 768:kvo + 1024].astype(MXU)
        seln = sel_ref[:, ncb:ncb + 1].astype(F32) > 0.5
        update(sn, (distn >= 0) & seln, lambda pb: jnp.dot(pb, vn, preferred_element_type=F32))
        o_s = acc_ref[...] / l_ref[...]
        lw = win_ref.shape[2]
        dist = pos - ((past - lw) + lax.broadcasted_iota(jnp.int32, (1, lw), 1))
        sw = jnp.dot(qbd_ref[...], win_ref[0].astype(MXU), preferred_element_type=F32) - slope * dist.astype(F32)
        sw = jnp.where((dist >= 0) & (dist <= B_WINDOW), sw, NEG)
        swn = _dot_nt(qbd_ref[...], new_ref[:, kvo + 1024:kvo + 1280].astype(MXU)) - slope * distn.astype(F32)
        swn = jnp.where(distn >= 0, swn, NEG)
        mw = jnp.maximum(_rowmax(sw), jnp.max(swn, axis=1, keepdims=True))
        pw = jnp.exp(sw - mw)
        pwn = jnp.exp(swn - mw)
        o_w = (_dot_nt(pw.astype(MXU), win_ref[1].astype(MXU))
               + jnp.dot(pwn.astype(MXU), new_ref[:, kvo + 1280:kvo + 1536].astype(MXU), preferred_element_type=F32))
        o_w = o_w / (jnp.sum(pw, axis=1, keepdims=True) + jnp.sum(pwn, axis=1, keepdims=True))
        pick = jnp.where(rowi == lax.broadcasted_iota(jnp.int32, (1, STEP_T), 1), 1.0, 0.0)
        gmat = _dot_hi(pick, _sigmoid(new_ref[:, B_BG_OFF:B_BG_OFF + 128]))
        gcol = (ridx >> 5) * 4 + rowg
        lane128 = lax.broadcasted_iota(jnp.int32, (1, 128), 1)
        gate = lambda br: jnp.sum(jnp.where(lane128 == gcol + 16 * br, gmat, 0.0), axis=1, keepdims=True)
        o_all = jnp.where(bd, gate(0) * oc_ref[...] + gate(1) * o_s + gate(2) * o_w, 0.0)
        for r in range(4):
            o_ref[:, 256 * r:256 * (r + 1)] = jnp.sum(o_all[32 * r:32 * (r + 1)].reshape(4, STEP_T, 256), axis=0)


def mixer_b_step(proj, pos_emb, w_cmp, kv_pool, win_cache, page_table):
    b, t, _ = proj.shape
    n_pages = page_table.shape[1]
    npg = B_STEP_PAGES
    past = n_pages * PAGE_SIZE
    assert t == STEP_T and n_pages % npg == 0
    ncb = past // B_BLOCK
    nsel = -(-(ncb + 1) // 128) * 128
    pool_t = _token_minor(kv_pool, 1).reshape(kv_pool.shape[0], 4, 256, PAGE_SIZE)
    cmp = nsa_compress_pages(pool_t, page_table, pos_emb, w_cmp)
    lw = win_cache.shape[1]
    win_t = _token_minor(win_cache, 1).reshape(b, 2, 256, lw)
    heads = np.array([4 * g + r for r in range(4) for g in range(4)])
    slopes = np.float32(2.0 ** (-8.0 * (heads + 1) / N_HEADS))
    slope_rows = jnp.asarray(np.repeat(slopes, STEP_T).reshape(STEP_LANES, 1), F32)
    page = lambda p: pl.BlockSpec((None, 2, 256, PAGE_SIZE), lambda i, s, pt, p=p: (pt[i, s * npg + p], 1, 0, 0))
    o = pl.pallas_call(
        functools.partial(_nsa_step_kernel, past=past),
        out_shape=jax.ShapeDtypeStruct((b, t, D_MODEL), F32),
        grid_spec=pltpu.PrefetchScalarGridSpec(
            num_scalar_prefetch=1,
            grid=(b, n_pages // npg),
            in_specs=[pl.BlockSpec((STEP_LANES, 1), lambda i, s, pt: (0, 0)),
                      pl.BlockSpec((None, t, B_E), lambda i, s, pt: (i, 0, 0)),
                      pl.BlockSpec((None, ncb, 512), lambda i, s, pt: (i, 0, 0)),
                      pl.BlockSpec((None, 2, 256, lw), lambda i, s, pt: (i, 0, 0, 0))]
            + [page(p) for p in range(npg)],
            out_specs=pl.BlockSpec((None, t, D_MODEL), lambda i, s, pt: (i, 0, 0)),
            scratch_shapes=[pltpu.VMEM((STEP_LANES, 256), MXU),
                            pltpu.VMEM((STEP_LANES, nsel), MXU),
                            pltpu.VMEM((STEP_LANES, 256), F32),
                            pltpu.VMEM((STEP_LANES, 1), F32), pltpu.VMEM((STEP_LANES, 1), F32),
                            pltpu.VMEM((STEP_LANES, 256), F32)]),
        compiler_params=pltpu.CompilerParams(
            dimension_semantics=("arbitrary", "arbitrary"), vmem_limit_bytes=VMEM_LIMIT),
        name="nsa_attention_step",
    )(page_table, slope_rows, proj, cmp, win_t, *([pool_t] * npg))
    kv_full_new = proj[..., B_KV_OFF:B_KV_OFF + 1024].reshape(b, t, 4, B_KV_HEADS, HEAD_DIM)
    return o, [kv_full_new, proj[..., B_KV_OFF + 1024:B_KV_OFF + 1536]]


def roll_b_window(win_caches, new_rows):
    nl, b, w = win_caches.shape[:3]
    ct = _token_minor(win_caches, 2).reshape(nl, b, 2 * B_KV_WIDTH, w)
    out = cache_roll(ct, jnp.stack(new_rows))
    return _token_major(out.reshape(nl, b, 2, B_KV_HEADS, HEAD_DIM, w), 2)


def _pad_cols(w, e_pad):
    return jnp.pad(w, ((0, 0), (0, e_pad - w.shape[1])))


def run_trunk(x, mods, layer_cache, weights, *, bt, tt):
    (norm_pre, norm_post, a_w_in, a_w_out, b_w_in, b_pos_emb, b_w_cmp, b_w_out,
     c_w_in, c_b_forget, c_w_out) = weights
    states = []
    for i in range(DEPTH):
        kind, j = i % N_MIXERS, i // N_MIXERS
        mod = mods[i][:, None, :]
        cache = layer_cache(i)
        if kind == 0:
            w_in = a_w_in[j].astype(BF16)
            proj = in_projection(x, mod, norm_pre[i], w_in, bt=bt, tt=tt, tn=1024)
            o, st = mixer_a_prompt(proj) if cache is None else mixer_a_step(proj, cache, j)
            gate_col, w_out = A_E // 1024 - 1, a_w_out[j]
        elif kind == 1:
            w_in = _pad_cols(b_w_in[j][:, _b_in_perm()], B_E).astype(BF16)
            proj = in_projection(x, mod, norm_pre[i], w_in, bt=bt, tt=tt, tn=1024)
            if cache is None:
                o, st = mixer_b_prompt(proj, b_pos_emb[j], b_w_cmp[j])
            else:
                o, st = mixer_b_step(proj, b_pos_emb[j], b_w_cmp[j], *cache)
            gate_col, w_out = 1, b_w_out[j][_b_out_perm(), :]
        else:
            w = c_w_in[j]
            w_in = jnp.concatenate([w[:, :3072], w[:, 3088:4112], _pad_cols(w[:, 3072:3088], 128)],
                                   axis=1).astype(BF16)
            proj = in_projection(x, mod, norm_pre[i], w_in, bt=bt, tt=tt, tn=C_E // 3)
            if cache is None:
                o, st = mixer_c_prompt(proj, c_b_forget[j])
            else:
                o, st = mixer_c_step(proj, c_b_forget[j], *cache)
            gate_col, w_out = 3, c_w_out[j]
        x = out_projection(o, proj, gate_col, x, mod, norm_post[i], w_out.astype(BF16), bt=bt, tt=tt)
        states.append(st)
    return x, states


def stack_state(states, kind, k):
    return jnp.stack([states[i][k] for i in range(DEPTH) if i % N_MIXERS == kind])


def kernel(x_prompt, x_sample, c_prompt, c_sample, cache_a_w128, cache_a_w512, cache_a_w2048,
           cache_b_kv, cache_b_win, cache_c_kv, cache_c_logf, page_table,
           ada_w, ada_b, norm_pre, norm_post, a_w_in, a_w_out, b_w_in, b_pos_emb, b_w_cmp, b_w_out,
           c_w_in, c_b_forget, c_w_out):
    weights = (norm_pre, norm_post, a_w_in, a_w_out, b_w_in, b_pos_emb, b_w_cmp, b_w_out,
               c_w_in, c_b_forget, c_w_out)
    nbp, nbs = x_prompt.shape[0], x_sample.shape[0]
    nb_pad = -(-(nbp + nbs) // 8) * 8
    c_all = jnp.concatenate([c_prompt, c_sample,
                             jnp.zeros((nb_pad - nbp - nbs, D_MODEL), F32)], axis=0)
    mods = ada_modulation(c_all, ada_w, ada_b)
    mods_p = mods[:, :nbp]
    mods_s = mods[:, nbp:nbp + nbs]

    def prompt_cache(i):
        return None

    def sample_cache(i):
        kind, j = i % N_MIXERS, i // N_MIXERS
        if kind == 0:
            return [cache_a_w128, cache_a_w512, cache_a_w2048]
        if kind == 1:
            return [cache_b_kv[j], cache_b_win[j], page_table]
        return [cache_c_kv[j], cache_c_logf[j], page_table]

    y_prompt, st_p = run_trunk(x_prompt, mods_p, prompt_cache, weights, bt=1, tt=1024)
    y_sample, st_s = run_trunk(x_sample, mods_s, sample_cache, weights, bt=nbs, tt=x_sample.shape[1])

    outs = [y_prompt, y_sample]
    outs += [stack_state(st_p, 0, 0), stack_state(st_p, 0, 1), stack_state(st_p, 0, 2),
             stack_state(st_p, 1, 0), stack_state(st_p, 1, 1),
             stack_state(st_p, 2, 0), stack_state(st_p, 2, 1)]
    layers = lambda kind: [i for i in range(DEPTH) if i % N_MIXERS == kind]
    outs += [roll_a_caches(c, [st_s[i][g] for i in layers(0)])
             for g, c in enumerate((cache_a_w128, cache_a_w512, cache_a_w2048))]
    outs += [stack_state(st_s, 1, 0), roll_b_window(cache_b_win, [st_s[i][1] for i in layers(1)]),
             stack_state(st_s, 2, 0), stack_state(st_s, 2, 1)]
    return tuple(outs)
```

```python
import functools

import jax
import jax.numpy as jnp
import numpy as np
from jax import lax
from jax.experimental import pallas as pl
from jax.experimental.pallas import tpu as pltpu

D_MODEL = 1024
DEPTH = 4
N_MIXERS = 3
HEAD_DIM = 64
RMS_EPS = 1e-6
N_HEADS = D_MODEL // HEAD_DIM
A_GROUPS = ((128, 1), (512, 4), (2048, 16))
B_KV_HEADS = N_HEADS // 4
B_KV_WIDTH = B_KV_HEADS * HEAD_DIM
B_BLOCK = 64
B_TOPK = 16
B_WINDOW = 512
PAGE_SIZE = 128

VMEM_LIMIT = 56 * 1024 * 1024

F32 = jnp.float32
BF16 = jnp.bfloat16


def _sigmoid(x):
    return 1.0 / (1.0 + jnp.exp(-x))


def _ada_kernel(c_ref, w_ref, b_ref, o_ref):
    c = c_ref[...]
    s = c * _sigmoid(c)
    o_ref[...] = jnp.dot(s, w_ref[...], preferred_element_type=F32,
                         precision=lax.Precision.HIGHEST) + b_ref[...]


def ada_modulation(c_all, ada_w, ada_b):
    nb = c_all.shape[0]
    tn = 1024
    return pl.pallas_call(
        _ada_kernel,
        out_shape=jax.ShapeDtypeStruct((DEPTH, nb, 3 * D_MODEL), F32),
        grid=(DEPTH, 3 * D_MODEL // tn),
        in_specs=[
            pl.BlockSpec((nb, D_MODEL), lambda i, n: (0, 0)),
            pl.BlockSpec((None, D_MODEL, tn), lambda i, n: (i, 0, n)),
            pl.BlockSpec((None, 1, tn), lambda i, n: (i, 0, n)),
        ],
        out_specs=pl.BlockSpec((None, nb, tn), lambda i, n: (i, 0, n)),
        compiler_params=pltpu.CompilerParams(vmem_limit_bytes=VMEM_LIMIT),
        name="ada_modulation",
    )(c_all, ada_w, ada_b.reshape(DEPTH, 1, 3 * D_MODEL))


def _inproj_kernel(x_ref, shift_ref, scale_ref, g_ref, w_ref, o_ref, h_ref):
    bt, tt, d = x_ref.shape

    @pl.when(pl.program_id(1) == 0)
    def _():
        x = x_ref[...]
        ms = jnp.mean(x * x, axis=-1, keepdims=True)
        y = x * lax.rsqrt(ms + RMS_EPS) * g_ref[...]
        h = y * (1.0 + scale_ref[...]) + shift_ref[...]
        h_ref[...] = h.reshape(bt * tt, d).astype(BF16)

    o_ref[...] = jnp.dot(h_ref[...], w_ref[...],
                         preferred_element_type=F32).reshape(o_ref.shape)


def in_projection(x, mod, g, w_bf16, *, bt, tt, tn):
    b, t, d = x.shape
    e = w_bf16.shape[1]
    assert b % bt == 0 and t % tt == 0 and e % tn == 0
    nt = t // tt
    return pl.pallas_call(
        _inproj_kernel,
        out_shape=jax.ShapeDtypeStruct((b, t, e), F32),
        grid=(b // bt * nt, e // tn),
        in_specs=[
            pl.BlockSpec((bt, tt, d), lambda m, n: (m // nt, m % nt, 0)),
            pl.BlockSpec((bt, 1, d), lambda m, n: (m // nt, 0, 0)),
            pl.BlockSpec((bt, 1, d), lambda m, n: (m // nt, 0, 1)),
            pl.BlockSpec((1, d), lambda m, n: (0, 0)),
            pl.BlockSpec((d, tn), lambda m, n: (0, n)),
        ],
        out_specs=pl.BlockSpec((bt, tt, tn), lambda m, n: (m // nt, m % nt, n)),
        scratch_shapes=[pltpu.VMEM((bt * tt, d), BF16)],
        compiler_params=pltpu.CompilerParams(
            dimension_semantics=("arbitrary", "arbitrary"),
            vmem_limit_bytes=VMEM_LIMIT),
        name="in_projection",
    )(x, mod, mod, g.reshape(1, d), w_bf16)


def _outproj_kernel(o_ref, gate_ref, x_ref, mg_ref, g_ref, w_ref, out_ref):
    bt, tt, d = x_ref.shape
    gt = gate_ref[...]
    og = (o_ref[...] * (gt * _sigmoid(gt))).reshape(bt * tt, d).astype(BF16)
    y = jnp.dot(og, w_ref[...], preferred_element_type=F32)
    ms = jnp.mean(y * y, axis=-1, keepdims=True)
    yn = (y * lax.rsqrt(ms + RMS_EPS) * g_ref[...]).reshape(bt, tt, d)
    out_ref[...] = x_ref[...] + mg_ref[...] * yn


def out_projection(o, proj, gate_col, x, mod, g, w_bf16, *, bt, tt):
    b, t, d = x.shape
    nt = t // tt
    return pl.pallas_call(
        _outproj_kernel,
        out_shape=jax.ShapeDtypeStruct((b, t, d), F32),
        grid=(b // bt * nt,),
        in_specs=[
            pl.BlockSpec((bt, tt, d), lambda m: (m // nt, m % nt, 0)),
            pl.BlockSpec((bt, tt, d), lambda m: (m // nt, m % nt, gate_col)),
            pl.BlockSpec((bt, tt, d), lambda m: (m // nt, m % nt, 0)),
            pl.BlockSpec((bt, 1, d), lambda m: (m // nt, 0, 2)),
            pl.BlockSpec((1, d), lambda m: (0, 0)),
            pl.BlockSpec((d, d), lambda m: (0, 0)),
        ],
        out_specs=pl.BlockSpec((bt, tt, d), lambda m: (m // nt, m % nt, 0)),
        compiler_params=pltpu.CompilerParams(
            dimension_semantics=("arbitrary",),
            vmem_limit_bytes=VMEM_LIMIT),
        name="out_projection",
    )(o, proj, x, mod, g.reshape(1, d), w_bf16)


MXU = BF16
NEG = -1e30
HIGHEST = lax.Precision.HIGHEST


def _slope(h):
    return float(np.float32(2.0 ** (-8.0 * (h + 1) / N_HEADS)))


def _dot_nt(a, b):
    return lax.dot_general(a, b, (((1,), (1,)), ((), ())), preferred_element_type=F32)


def _dot_tn(a, b):
    return lax.dot_general(a, b, (((0,), (0,)), ((), ())), preferred_element_type=F32)


def _rowmax(s):
    m = s[:, 0:128]
    for c in range(1, s.shape[1] // 128):
        m = jnp.maximum(m, s[:, c * 128:(c + 1) * 128])
    return jnp.max(m, axis=1, keepdims=True)


B_E = 4096
B_KV_OFF = 2048
B_BG_OFF = 3584


def _b_in_perm():
    perm = np.zeros(3632, np.int64)
    for r in range(4):
        for g in range(4):
            h = 4 * g + r
            for d in range(64):
                perm[r * 256 + g * 64 + d] = h * 64 + d
                perm[1024 + r * 256 + g * 64 + d] = 2608 + h * 64 + d
    perm[2048:3584] = 1024 + np.arange(1536)
    for br in range(3):
        for r in range(4):
            for g in range(4):
                perm[3584 + br * 16 + r * 4 + g] = 2560 + (4 * g + r) * 3 + br
    return perm


def _b_out_perm():
    perm = np.zeros(1024, np.int64)
    for r in range(4):
        for g in range(4):
            for d in range(64):
                perm[r * 256 + g * 64 + d] = (4 * g + r) * 64 + d
    return perm


def _nsa_cmp_kernel(x_ref, pe_ref, w_ref, o_ref):
    tt = x_ref.shape[0]
    nb = tt // B_BLOCK
    x = x_ref[...].reshape(nb, B_BLOCK, 512) + pe_ref[...][None]
    pooled = jnp.sum(x, axis=1) * (1.0 / B_BLOCK)
    o_ref[...] = jnp.dot(pooled, w_ref[...], preferred_element_type=F32, precision=HIGHEST)


def _nsa_cmp_weights(pos_emb, w_cmp):
    pe = pos_emb.transpose(1, 0, 2, 3).reshape(B_BLOCK, 512)
    wbd = jnp.zeros((512, 512), F32)
    for c in range(2):
        for g in range(4):
            o = c * 256 + g * 64
            wbd = wbd.at[o:o + 64, o:o + 64].set(w_cmp[c, g])
    return pe, wbd


def nsa_compress_prompt(proj, pe, wbd, *, tt=512):
    b, t, _ = proj.shape
    return pl.pallas_call(
        _nsa_cmp_kernel,
        out_shape=jax.ShapeDtypeStruct((b, t // B_BLOCK, 512), F32),
        grid=(b, t // tt),
        in_specs=[
            pl.BlockSpec((None, tt, 512), lambda i, j: (i, j, B_KV_OFF // 512)),
            pl.BlockSpec((B_BLOCK, 512), lambda i, j: (0, 0)),
            pl.BlockSpec((512, 512), lambda i, j: (0, 0)),
        ],
        out_specs=pl.BlockSpec((None, tt // B_BLOCK, 512), lambda i, j: (i, j, 0)),
        compiler_params=pltpu.CompilerParams(vmem_limit_bytes=VMEM_LIMIT),
        name="nsa_compress_prompt",
    )(proj, pe, wbd)


NSA_TK = 512


def _nsa_prompt_kernel(q_ref, ks_ref, vs_ref, kw_ref, vw_ref, cmp_ref, bg_ref, ex_ref, o_ref,
                       ksb, kwb, vsg, vwg, q4_ref, sel_ref, oc_ref, m_ref, acc_ref, *, tq, t):
    qi = pl.program_id(1)
    nblk = t // B_BLOCK
    tk = NSA_TK
    q0 = qi * tq
    lane = lax.broadcasted_iota(jnp.int32, (1, 256), 1)
    gmasks = [(lane >= 64 * g) & (lane < 64 * (g + 1)) for g in range(4)]

    @pl.when(qi == 0)
    def _():
        ksb[...] = ks_ref[...].astype(MXU)
        kwb[...] = kw_ref[...].astype(MXU)
        ones = jnp.ones((t, 64), F32)
        for g in range(4):
            vsg[g] = jnp.concatenate([vs_ref[:, 64 * g:64 * (g + 1)], ones], axis=1).astype(MXU)
            vwg[g] = jnp.concatenate([vw_ref[:, 64 * g:64 * (g + 1)], ones], axis=1).astype(MXU)

    posq = q0 + lax.broadcasted_iota(jnp.int32, (1, tq), 1)
    posc = q0 + lax.broadcasted_iota(jnp.int32, (tq, 1), 0)
    bidx = lax.broadcasted_iota(jnp.int32, (nblk, 1), 0)
    blk_end = (bidx + 1) * B_BLOCK - 1
    cur = posq // B_BLOCK
    forced = (bidx == 0) | (bidx >= cur - 1)
    kc = cmp_ref[:, 0:256].astype(MXU)
    vc = cmp_ref[:, 256:512].astype(MXU)
    distc = (posq - blk_end).astype(F32)
    validc = blk_end <= posq

    for g in range(4):
        q4 = jnp.concatenate(
            [jnp.where(gmasks[g], q_ref[:, r * 256:(r + 1) * 256], 0.0) for r in range(4)], axis=0)
        q4_ref[g] = (q4 * (HEAD_DIM ** -0.5)).astype(MXU)

        s_ct = _dot_nt(kc, q4_ref[g])
        p_parts = []
        for r in range(4):
            s_r = s_ct[:, r * tq:(r + 1) * tq] - _slope(4 * g + r) * distc
            sm = jnp.where(validc, s_r, -jnp.inf)
            mx = jnp.max(sm, axis=0, keepdims=True)
            mx = jnp.where(mx > -jnp.inf, mx, 0.0)
            e = jnp.exp(sm - mx)
            den = jnp.sum(e, axis=0, keepdims=True)
            p_parts.append(e / jnp.where(den > 0, den, 1.0))
        oc_ref[g] = _dot_tn(jnp.concatenate(p_parts, axis=1).astype(MXU), vc)

        sc = p_parts[0] + p_parts[1] + p_parts[2] + p_parts[3]
        score = jnp.where(bidx > cur, -jnp.inf, jnp.where(forced, jnp.inf, sc))
        rank = jnp.zeros((nblk, tq), F32)
        for n in range(nblk):
            row = score[n:n + 1, :]
            beats = (row > score) | ((row == score) & (bidx > n))
            rank = rank + jnp.where(beats, 1.0, 0.0)
        sel_ref[g] = jnp.where(rank < float(min(B_TOPK, nblk)), 1.0, 0.0).T.astype(MXU)

    def tile(off, k_b, vg_ref, mask_fn):
        kt = k_b[pl.ds(off, tk), :]
        kq = (off - q0 + lax.broadcasted_iota(jnp.int32, (1, tk), 1)).astype(F32)
        ss = []
        for g in range(4):
            s = _dot_nt(q4_ref[g], kt)
            madd = mask_fn(g)
            ss.append(jnp.concatenate(
                [s[r * tq:(r + 1) * tq] + _slope(4 * g + r) * kq + madd for r in range(4)], axis=0))
        m_olds = [m_ref[g] for g in range(4)]
        m_news = [jnp.maximum(m_olds[g], _rowmax(ss[g])) for g in range(4)]
        ps = [jnp.exp(ss[g] - m_news[g]).astype(MXU) for g in range(4)]
        pvs = [jnp.dot(ps[g], vg_ref[g, pl.ds(off, tk), :], preferred_element_type=F32) for g in range(4)]
        for g in range(4):
            acc_ref[g] = jnp.exp(m_olds[g] - m_news[g]) * acc_ref[g] + pvs[g]
            m_ref[g] = m_news[g]

    def branch(k_b, vg_ref, j_lo, j_hi, mask_fn, last_mask_fn):
        m_ref[...] = jnp.full(m_ref.shape, NEG, F32)
        acc_ref[...] = jnp.zeros(acc_ref.shape, F32)

        def body(j, c):
            off = pl.multiple_of(j * tk, tk)
            tile(off, k_b, vg_ref, lambda g: mask_fn(g, off, j))
            return c

        lax.fori_loop(j_lo, j_hi, body, 0)
        off = pl.multiple_of(j_hi * tk, tk)
        tile(off, k_b, vg_ref, lambda g: last_mask_fn(g, off, j_hi))
        return [acc_ref[g][:, 0:64] / acc_ref[g][:, 64:65] for g in range(4)]

    j_hi = (q0 + tq - 1) // tk
    kcol = lax.broadcasted_iota(jnp.int32, (1, tk), 1)

    def sel_madd(g, off, j):
        sel_keys = jnp.dot(sel_ref[g], ex_ref[:, pl.ds(off, tk)], preferred_element_type=F32)
        return (sel_keys - 1.0) * (-NEG)

    def sel_madd_last(g, off, j):
        return sel_madd(g, off, j) + jnp.where(off + kcol <= posc, 0.0, NEG)

    o_s = branch(ksb, vsg, 0, j_hi, sel_madd, sel_madd_last)

    wk = B_WINDOW + tq
    w0 = pl.multiple_of(jnp.maximum(q0 - B_WINDOW, 0), 128)
    kw_t = kwb[pl.ds(w0, wk), :]
    wpos = w0 + lax.broadcasted_iota(jnp.int32, (1, wk), 1)
    wdist = posc - wpos
    wmadd = jnp.where((wdist >= 0) & (wdist <= B_WINDOW), 0.0, NEG)
    wq = (wpos - q0).astype(F32)
    o_w = []
    for g in range(4):
        s = _dot_nt(q4_ref[g], kw_t)
        s = jnp.concatenate(
            [s[r * tq:(r + 1) * tq] + _slope(4 * g + r) * wq + wmadd for r in range(4)], axis=0)
        p = jnp.exp(s - _rowmax(s)).astype(MXU)
        acc = jnp.dot(p, vwg[g, pl.ds(w0, wk), :], preferred_element_type=F32)
        o_w.append(acc[:, 0:64] / acc[:, 64:65])

    bgate = _sigmoid(bg_ref[...])
    for r in range(4):
        rs = slice(r * tq, (r + 1) * tq)
        parts = []
        for g in range(4):
            c0 = r * 4 + g
            parts.append(bgate[:, c0:c0 + 1] * oc_ref[g][rs, 64 * g:64 * (g + 1)]
                         + bgate[:, 16 + c0:17 + c0] * o_s[g][rs] + bgate[:, 32 + c0:33 + c0] * o_w[g][rs])
        o_ref[:, r * 256:(r + 1) * 256] = jnp.concatenate(parts, axis=1)


def _nsa_expand(t):
    nblk = t // B_BLOCK
    ex = (np.arange(t)[None, :] // B_BLOCK == np.arange(nblk)[:, None])
    return jnp.asarray(ex, MXU)


def nsa_attention_prompt(proj, cmp, *, tq=128):
    b, t, _ = proj.shape
    assert tq == 128 and t % tq == 0
    kv = lambda c: pl.BlockSpec((None, t, 256), lambda i, j, c=c: (i, 0, B_KV_OFF // 256 + c),
                                pipeline_mode=pl.Buffered(1))
    return pl.pallas_call(
        functools.partial(_nsa_prompt_kernel, tq=tq, t=t),
        out_shape=jax.ShapeDtypeStruct((b, t, D_MODEL), F32),
        grid=(b, t // tq),
        in_specs=[
            pl.BlockSpec((None, tq, 1024), lambda i, j: (i, j, 0)),
            kv(2), kv(3), kv(4), kv(5),
            pl.BlockSpec((None, t // B_BLOCK, 512), lambda i, j: (i, 0, 0)),
            pl.BlockSpec((None, tq, 128), lambda i, j: (i, j, B_BG_OFF // 128)),
            pl.BlockSpec((t // B_BLOCK, t), lambda i, j: (0, 0)),
        ],
        out_specs=pl.BlockSpec((None, tq, D_MODEL), lambda i, j: (i, j, 0)),
        scratch_shapes=[
            pltpu.VMEM((t, 256), MXU), pltpu.VMEM((t, 256), MXU),
            pltpu.VMEM((4, t, 128), MXU), pltpu.VMEM((4, t, 128), MXU),
            pltpu.VMEM((4, 4 * tq, 256), MXU),
            pltpu.VMEM((4, tq, t // B_BLOCK), MXU),
            pltpu.VMEM((4, 4 * tq, 256), F32),
            pltpu.VMEM((4, 4 * tq, 1), F32),
            pltpu.VMEM((4, 4 * tq, 128), F32)],
        compiler_params=pltpu.CompilerParams(
            dimension_semantics=("arbitrary", "arbitrary"),
            vmem_limit_bytes=VMEM_LIMIT),
        name="nsa_attention_prompt",
    )(proj, proj, proj, proj, proj, cmp, proj, _nsa_expand(t))


def mixer_b_prompt(proj, pos_emb, w_cmp):
    b, t, _ = proj.shape
    pe, wbd = _nsa_cmp_weights(pos_emb, w_cmp)
    cmp = nsa_compress_prompt(proj, pe, wbd)
    o = nsa_attention_prompt(proj, cmp)
    kv_full_new = proj[..., B_KV_OFF:B_KV_OFF + 1024].reshape(b, t, 4, B_KV_HEADS, HEAD_DIM)
    new_win = proj[:, t - min(B_WINDOW, t):, B_KV_OFF + 1024:B_KV_OFF + 1536].reshape(
        b, min(B_WINDOW, t), 2, B_KV_HEADS, HEAD_DIM)
    return o, [kv_full_new, new_win]


C_E = 4224
C_F_OFF = 4096


def _log_sigmoid(x):
    return jnp.minimum(x, 0.0) - jnp.log(1.0 + jnp.exp(-jnp.abs(x)))


def _fox_prep_kernel(x_ref, bf_ref, logft_ref, cumt_ref, carry_ref):
    @pl.when(pl.program_id(1) == 0)
    def _():
        carry_ref[...] = jnp.zeros(carry_ref.shape, F32)

    n = x_ref.shape[0]
    logf = _log_sigmoid(x_ref[...] + bf_ref[...])
    tri = jnp.where(lax.broadcasted_iota(jnp.int32, (n, n), 1) <= lax.broadcasted_iota(jnp.int32, (n, n), 0),
                    1.0, 0.0)
    cum = jnp.dot(tri, logf, preferred_element_type=F32, precision=HIGHEST) + carry_ref[...]
    logft_ref[...] = logf.T[0:N_HEADS, :]
    cumt_ref[...] = cum.T[0:N_HEADS, :]
    carry_ref[...] = cum[n - 1:n, :]


def fox_prep(proj, b_forget, *, tt=128):
    b, t, _ = proj.shape
    bf = jnp.pad(b_forget.astype(F32), (0, 128 - N_HEADS)).reshape(1, 128)
    return pl.pallas_call(
        _fox_prep_kernel,
        out_shape=[jax.ShapeDtypeStruct((b, N_HEADS, t), F32), jax.ShapeDtypeStruct((b, N_HEADS, t), F32)],
        grid=(b, t // tt),
        in_specs=[pl.BlockSpec((None, tt, 128), lambda i, j: (i, j, C_F_OFF // 128)),
                  pl.BlockSpec((1, 128), lambda i, j: (0, 0))],
        out_specs=[pl.BlockSpec((None, N_HEADS, tt), lambda i, j: (i, 0, j)),
                   pl.BlockSpec((None, N_HEADS, tt), lambda i, j: (i, 0, j))],
        scratch_shapes=[pltpu.VMEM((1, 128), F32)],
        compiler_params=pltpu.CompilerParams(
            dimension_semantics=("arbitrary", "arbitrary"), vmem_limit_bytes=VMEM_LIMIT),
        name="fox_prep",
    )(proj, bf)


def _fox_prompt_kernel(q_ref, k_ref, v_ref, ck_ref, o_ref, kb, vb, qm_ref, m_ref, acc_ref, *, tq):
    hp = pl.program_id(1)
    qi = pl.program_id(2)

    lane = lax.broadcasted_iota(jnp.int32, (1, 128), 1)
    hmasks = [(lane >= 64 * hh) & (lane < 64 * (hh + 1)) for hh in range(2)]

    @pl.when(qi == 0)
    def _():
        kb[...] = k_ref[...].astype(MXU)
        for hh in range(2):
            vb[hh] = jnp.where(hmasks[hh], v_ref[...], 1.0).astype(MXU)

    for hh in range(2):
        qm_ref[hh] = (jnp.where(hmasks[hh], q_ref[...], 0.0) * (HEAD_DIM ** -0.5)).astype(MXU)
    m_ref[...] = jnp.full(m_ref.shape, NEG, F32)
    acc_ref[...] = jnp.zeros(acc_ref.shape, F32)

    def tile(j, diag):
        off = pl.multiple_of(j * tq, tq)
        kt = kb[pl.ds(off, tq), :]
        for hh in range(2):
            ck = ck_ref[pl.ds(2 * hp + hh, 1), pl.ds(off, tq)]
            s = _dot_nt(qm_ref[hh], kt) - ck
            if diag:
                causal = (lax.broadcasted_iota(jnp.int32, (1, tq), 1)
                          <= lax.broadcasted_iota(jnp.int32, (tq, 1), 0))
                s = jnp.where(causal, s, NEG)
            m_old = m_ref[hh]
            m_new = jnp.maximum(m_old, _rowmax(s))
            p = jnp.exp(s - m_new)
            acc_ref[hh] = jnp.exp(m_old - m_new) * acc_ref[hh] + jnp.dot(
                p.astype(MXU), vb[hh, pl.ds(off, tq), :], preferred_element_type=F32)
            m_ref[hh] = m_new

    def body(j, c):
        tile(j, False)
        return c

    lax.fori_loop(0, qi, body, 0)
    tile(qi, True)
    a0, a1 = acc_ref[0], acc_ref[1]
    o_ref[...] = jnp.where(lane < 64, a0 / a0[:, 64:65], a1 / a1[:, 0:1])


def fox_attention_prompt(proj, cumt, *, tq=512):
    b, t, _ = proj.shape
    tq = min(tq, t)
    return pl.pallas_call(
        functools.partial(_fox_prompt_kernel, tq=tq),
        out_shape=jax.ShapeDtypeStruct((b, t, D_MODEL), F32),
        grid=(b, N_HEADS // 2, t // tq),
        in_specs=[
            pl.BlockSpec((None, tq, 128), lambda i, p, j: (i, j, p)),
            pl.BlockSpec((None, t, 128), lambda i, p, j: (i, 0, 8 + p)),
            pl.BlockSpec((None, t, 128), lambda i, p, j: (i, 0, 16 + p)),
            pl.BlockSpec((None, N_HEADS, t), lambda i, p, j: (i, 0, 0)),
        ],
        out_specs=pl.BlockSpec((None, tq, 128), lambda i, p, j: (i, j, p)),
        scratch_shapes=[pltpu.VMEM((t, 128), MXU), pltpu.VMEM((2, t, 128), MXU),
                        pltpu.VMEM((2, tq, 128), MXU),
                        pltpu.VMEM((2, tq, 1), F32), pltpu.VMEM((2, tq, 128), F32)],
        compiler_params=pltpu.CompilerParams(
            dimension_semantics=("arbitrary", "arbitrary", "arbitrary"), vmem_limit_bytes=VMEM_LIMIT),
        name="fox_attention_prompt",
    )(proj, proj, proj, cumt)


def mixer_c_prompt(proj, b_forget):
    b, t, _ = proj.shape
    logft, cumt = fox_prep(proj, b_forget)
    o = fox_attention_prompt(proj, cumt)
    kv_new = proj[..., 1024:3072].reshape(b, t, 2, N_HEADS, HEAD_DIM)
    return o, [kv_new, logft.transpose(0, 2, 1)]


A_E = 10240
A_NK = 128
A_CHUNK = 2048
A_UNROLL = 2


def _dil_prompt_fused_kernel(slope_ref, *refs):
    groups = [refs[5 * g:5 * g + 5] for g in range(3)]
    o_ref, kcat, vcat, bias_ref, m0_ref, m1_ref, acc0_ref, acc1_ref = refs[15:]
    m_refs, acc_refs = (m0_ref, m1_ref), (acc0_ref, acc1_ref)
    ci = pl.program_id(2)
    nk = A_NK
    ch = o_ref.shape[0]
    lane = lax.broadcasted_iota(jnp.int32, (1, 128), 1)
    hmasks = [(lane >= 64 * hh) & (lane < 64 * (hh + 1)) for hh in range(2)]
    for hh in range(2):
        m_refs[hh][...] = jnp.full(m_refs[hh].shape, NEG, F32)
        acc_refs[hh][...] = jnp.zeros(acc_refs[hh].shape, F32)
    step = (nk + lax.broadcasted_iota(jnp.int32, (nk, 1), 0)) - lax.broadcasted_iota(jnp.int32, (1, 2 * nk), 1)
    kcol = lax.broadcasted_iota(jnp.int32, (1, 2 * nk), 1)
    in_band = (step >= 0) & (step <= nk)
    slopes = slope_ref[...]
    for g, (q_ref, kc_ref, kp_ref, vc_ref, vp_ref) in enumerate(groups):
        _, dil = A_GROUPS[g]
        pr = nk * dil
        kcat[0:pr, :] = kp_ref[...]
        kcat[pr:pr + ch, :] = kc_ref[...]
        vcat[0:pr, :] = vp_ref[...]
        vcat[pr:pr + ch, :] = vc_ref[...]
        distf = (step * dil).astype(F32)
        for hh in range(2):
            bias_ref[hh] = slopes[:, 64 * hh:64 * hh + 1] * distf

        def body(tj, c, q_ref=q_ref, dil=dil, pr=pr):
            work = []
            for u in range(A_UNROLL):
                ti = tj * A_UNROLL + u
                r = ti % dil
                n = ti // dil
                start = pr * n + r
                qs = pl.ds(start, nk, stride=dil)
                ks = pl.ds(start, 2 * nk, stride=dil)
                q2 = q_ref[qs, :]
                k2 = kcat[ks, :].astype(MXU)
                v2 = vcat[ks, :]
                valid = in_band & ((kcol >= nk) | (n > 0) | (ci > 0))
                for hh in range(2):
                    qm = (jnp.where(hmasks[hh], q2, 0.0) * (HEAD_DIM ** -0.5)).astype(MXU)
                    sm = jnp.where(valid, _dot_nt(qm, k2) - bias_ref[hh], NEG)
                    work.append((hh, qs, sm, v2))
            m_olds = [m_refs[hh][qs, :] for hh, qs, _, _ in work]
            m_news = [jnp.maximum(mo, _rowmax(sm)) for mo, (_, _, sm, _) in zip(m_olds, work)]
            ps = [jnp.exp(sm - mn).astype(MXU) for mn, (_, _, sm, _) in zip(m_news, work)]
            pvs = [jnp.dot(p, jnp.where(hmasks[hh], v2, 1.0).astype(MXU), preferred_element_type=F32)
                   for p, (hh, _, _, v2) in zip(ps, work)]
            for mo, mn, pv, (hh, qs, _, _) in zip(m_olds, m_news, pvs, work):
                acc_refs[hh][qs, :] = jnp.exp(mo - mn) * acc_refs[hh][qs, :] + pv
                m_refs[hh][qs, :] = mn
            return c

        lax.fori_loop(0, ch // nk // A_UNROLL, body, 0)
    a0, a1 = acc0_ref[...], acc1_ref[...]
    o_ref[...] = jnp.where(lane < 64, a0 / a0[:, 64:65], a1 / a1[:, 0:1])


def dilated_attention_prompt(proj):
    b, t, _ = proj.shape
    nk = A_NK
    ch = min(A_CHUNK, t)
    assert t % ch == 0 and all(w // d == nk and ch % (nk * d) == 0 for w, d in A_GROUPS)
    sl = np.float32(2.0 ** (-8.0 * np.arange(1, N_HEADS + 1) / N_HEADS))
    slope_rows = jnp.asarray(np.repeat(sl, 64).reshape(N_HEADS // 2, 1, 128), F32)
    in_specs = [pl.BlockSpec((None, 1, 128), lambda bi, hp, ci: (hp, 0, 0))]
    for g, (_, dil) in enumerate(A_GROUPS):
        pr = nk * dil
        cur = lambda part, g=g: pl.BlockSpec(
            (None, ch, 128), lambda bi, hp, ci, part=part, g=g: (bi, ci, (3072 * g + 1024 * part) // 128 + hp))
        prev = lambda part, g=g, pr=pr: pl.BlockSpec(
            (None, pr, 128), lambda bi, hp, ci, part=part, g=g, pr=pr:
            (bi, jnp.maximum(ci * (ch // pr) - 1, 0), (3072 * g + 1024 * part) // 128 + hp))
        in_specs += [cur(0), cur(1), prev(1), cur(2), prev(2)]
    return pl.pallas_call(
        _dil_prompt_fused_kernel,
        out_shape=jax.ShapeDtypeStruct((b, t, D_MODEL), F32),
        grid=(b, N_HEADS // 2, t // ch),
        in_specs=in_specs,
        out_specs=pl.BlockSpec((None, ch, 128), lambda bi, hp, ci: (bi, ci, hp)),
        scratch_shapes=[pltpu.VMEM((nk * A_GROUPS[-1][1] + ch, 128), F32),
                        pltpu.VMEM((nk * A_GROUPS[-1][1] + ch, 128), F32),
                        pltpu.VMEM((2, nk, 2 * nk), F32),
                        pltpu.VMEM((ch, 1), F32), pltpu.VMEM((ch, 1), F32),
                        pltpu.VMEM((ch, 128), F32), pltpu.VMEM((ch, 128), F32)],
        compiler_params=pltpu.CompilerParams(
            dimension_semantics=("arbitrary", "arbitrary", "arbitrary"), vmem_limit_bytes=VMEM_LIMIT),
        name="dilated_attention_prompt",
    )(slope_rows, *([proj] * 15))


def mixer_a_prompt(proj):
    b, t, _ = proj.shape
    o = dilated_attention_prompt(proj)
    new_state = []
    for g, (window, _) in enumerate(A_GROUPS):
        w = min(window, t)
        new_state.append(proj[:, t - w:, 3072 * g + 1024:3072 * g + 3072].reshape(b, w, 2, N_HEADS, HEAD_DIM))
    return o, new_state


STEP_T = 8
STEP_LANES = N_HEADS * STEP_T


def _tile_rows(x, n):
    return jnp.concatenate([x] * n, axis=0)


def _dot_hi(a, b):
    return jnp.dot(a, b, preferred_element_type=F32, precision=HIGHEST)


def _roll_kernel(c_ref, n_ref, o_ref):
    w = c_ref.shape[1]
    lane = lax.broadcasted_iota(jnp.int32, (1, 128), 1)
    place = jnp.where(lax.broadcasted_iota(jnp.int32, (STEP_T, 1), 0) == lane - (128 - STEP_T), 1.0, 0.0)
    new_t = lax.dot_general(n_ref[...], place, (((0,), (0,)), ((), ())), preferred_element_type=F32,
                            precision=HIGHEST)
    rolled = pltpu.roll(c_ref[...], w - STEP_T, axis=1)
    if w > 128:
        o_ref[:, 0:w - 128] = rolled[:, 0:w - 128]
    o_ref[:, w - 128:w] = jnp.where(lane >= 128 - STEP_T, new_t, rolled[:, w - 128:w])


def cache_roll(cache_t, new_rows, *, rb=512):
    nl, b, r, w = cache_t.shape
    return pl.pallas_call(
        _roll_kernel,
        out_shape=jax.ShapeDtypeStruct(cache_t.shape, F32),
        grid=(nl, b, r // rb),
        in_specs=[pl.BlockSpec((None, None, rb, w), lambda l, i, k: (l, i, k, 0)),
                  pl.BlockSpec((None, None, STEP_T, rb), lambda l, i, k: (l, i, 0, k))],
        out_specs=pl.BlockSpec((None, None, rb, w), lambda l, i, k: (l, i, k, 0)),
        compiler_params=pltpu.CompilerParams(
            dimension_semantics=("arbitrary", "arbitrary", "arbitrary"), vmem_limit_bytes=VMEM_LIMIT),
        name="cache_roll",
    )(cache_t, new_rows)


def _token_minor(x, token_axis):
    perm = [a for a in range(x.ndim) if a != token_axis] + [token_axis]
    return jnp.transpose(x, perm)


def _token_major(x, token_axis):
    perm = list(range(x.ndim - 1))
    perm.insert(token_axis, x.ndim - 1)
    return jnp.transpose(x, perm)


def _dil_step_kernel(slope_ref, new_ref, c0_ref, c1_ref, c2_ref, o_ref):
    hg = pl.program_id(1)
    rows = 4 * STEP_T
    rowi = lax.broadcasted_iota(jnp.int32, (rows, 1), 0) & (STEP_T - 1)
    bd = (lax.broadcasted_iota(jnp.int32, (rows, 1), 0) >> 3) == (lax.broadcasted_iota(jnp.int32, (1, 256), 1) >> 6)
    slope = slope_ref[...]
    pieces = []
    for g, c_ref in enumerate((c0_ref, c1_ref, c2_ref)):
        window, dil = A_GROUPS[g]
        col = lambda part: pl.ds(pl.multiple_of(3072 * g + 1024 * part + hg * 256, 256), 256)
        qbd = (jnp.where(bd, _tile_rows(new_ref[:, col(0)], 4), 0.0) * (HEAD_DIM ** -0.5)).astype(MXU)
        s = jnp.dot(qbd, c_ref[0].astype(MXU), preferred_element_type=F32)
        dist = (window + rowi) - lax.broadcasted_iota(jnp.int32, (1, window), 1)
        valid = (dist <= window) & ((dist & (dil - 1)) == 0)
        pieces.append((jnp.where(valid, s - slope * dist.astype(F32), NEG), c_ref[1].astype(MXU), True))
        sn = _dot_nt(qbd, new_ref[:, col(1)].astype(MXU))
        distn = rowi - lax.broadcasted_iota(jnp.int32, (1, STEP_T), 1)
        validn = (distn >= 0) & ((distn & (dil - 1)) == 0)
        pieces.append((jnp.where(validn, sn - slope * distn.astype(F32), NEG),
                       new_ref[:, col(2)].astype(MXU), False))
    m = pieces[0][0][:, 0:1]
    for s, _, _ in pieces:
        m = jnp.maximum(m, jnp.max(s, axis=1, keepdims=True))
    den = jnp.zeros((rows, 1), F32)
    acc = jnp.zeros((rows, 256), F32)
    for s, v, transposed in pieces:
        p = jnp.exp(s - m)
        den = den + jnp.sum(p, axis=1, keepdims=True)
        pb = p.astype(MXU)
        acc = acc + (_dot_nt(pb, v) if transposed else jnp.dot(pb, v, preferred_element_type=F32))
    o = jnp.where(bd, acc / den, 0.0)
    o_ref[...] = jnp.sum(o.reshape(4, STEP_T, 256), axis=0)


def mixer_a_step(proj, caches, j):
    b, t, _ = proj.shape
    assert t == STEP_T
    views = [_token_minor(c, 2).reshape(c.shape[0], b, 2, D_MODEL, c.shape[2]) for c in caches]
    slopes = np.float32(2.0 ** (-8.0 * np.arange(1, N_HEADS + 1) / N_HEADS))
    slope_rows = jnp.asarray(np.repeat(slopes, STEP_T).reshape(4, 4 * STEP_T, 1), F32)
    cspec = lambda w: pl.BlockSpec((None, None, 2, 256, w), lambda i, h: (j, i, 0, h, 0))
    o = pl.pallas_call(
        _dil_step_kernel,
        out_shape=jax.ShapeDtypeStruct((b, t, D_MODEL), F32),
        grid=(b, 4),
        in_specs=[pl.BlockSpec((None, 4 * STEP_T, 1), lambda i, h: (h, 0, 0)),
                  pl.BlockSpec((None, t, A_E), lambda i, h: (i, 0, 0))]
        + [cspec(v.shape[-1]) for v in views],
        out_specs=pl.BlockSpec((None, t, 256), lambda i, h: (i, 0, h)),
        compiler_params=pltpu.CompilerParams(
            dimension_semantics=("arbitrary", "arbitrary"), vmem_limit_bytes=VMEM_LIMIT),
        name="dilated_attention_step",
    )(slope_rows, proj, *views)
    return o, [proj[..., 3072 * g + 1024:3072 * g + 3072] for g in range(3)]


def roll_a_caches(caches, new_rows):
    nl, b, w = caches.shape[:3]
    ct = _token_minor(caches, 2).reshape(nl, b, 2 * D_MODEL, w)
    out = cache_roll(ct, jnp.stack(new_rows))
    return _token_major(out.reshape(nl, b, 2, N_HEADS, HEAD_DIM, w), 2)


C_STEP_PAGES = 8


def _fox_step_kernel(pt_ref, new_ref, bf_ref, *refs):
    npg = C_STEP_PAGES
    kv_refs = refs[:npg]
    lf_refs = refs[npg:2 * npg]
    o_ref, lfo_ref, qbd_ref, m_ref, l_ref, acc_ref, carry_ref = refs[2 * npg:]
    s_id = pl.program_id(1)
    rows = 4 * STEP_T
    rowi = lax.broadcasted_iota(jnp.int32, (STEP_LANES, 1), 0) & (STEP_T - 1)
    bd = (lax.broadcasted_iota(jnp.int32, (rows, 1), 0) >> 3) == (lax.broadcasted_iota(jnp.int32, (1, 256), 1) >> 6)
    expand = jnp.where((lax.broadcasted_iota(jnp.int32, (STEP_LANES, 1), 0) >> 3)
                       == lax.broadcasted_iota(jnp.int32, (1, N_HEADS), 1), 1.0, 0.0)

    @pl.when(s_id == 0)
    def _():
        m_ref[...] = jnp.full(m_ref.shape, NEG, F32)
        l_ref[...] = jnp.zeros(l_ref.shape, F32)
        acc_ref[...] = jnp.zeros(acc_ref.shape, F32)
        carry_ref[...] = jnp.zeros(carry_ref.shape, F32)
        for hg in range(4):
            q = new_ref[:, 256 * hg:256 * (hg + 1)]
            qbd_ref[hg] = (jnp.where(bd, _tile_rows(q, 4), 0.0) * (HEAD_DIM ** -0.5)).astype(MXU)

    def update(s, valid, pv_fn):
        if valid is not None:
            s = jnp.where(valid, s, NEG)
        m_old = m_ref[...]
        m_new = jnp.maximum(m_old, _rowmax(s) if s.shape[1] % 128 == 0 else jnp.max(s, axis=1, keepdims=True))
        p = jnp.exp(s - m_new)
        alpha = jnp.exp(m_old - m_new)
        l_ref[...] = alpha * l_ref[...] + jnp.sum(p, axis=1, keepdims=True)
        pb = p.astype(MXU)
        pv = jnp.concatenate([pv_fn(pb[rows * hg:rows * (hg + 1)], hg) for hg in range(4)], axis=0)
        acc_ref[...] = alpha * acc_ref[...] + pv
        m_ref[...] = m_new

    triu = jnp.where(lax.broadcasted_iota(jnp.int32, (PAGE_SIZE, 1), 0)
                     <= lax.broadcasted_iota(jnp.int32, (1, PAGE_SIZE), 1), 1.0, 0.0)
    carry = carry_ref[...]
    cks = []
    for p in range(npg):
        cum = _dot_hi(lf_refs[p][...], triu) + carry
        carry = cum[:, PAGE_SIZE - 1:PAGE_SIZE]
        cks.append(cum)
    carry_ref[...] = carry
    ck = _dot_hi(expand, jnp.concatenate(cks, axis=1))
    s = jnp.concatenate(
        [jnp.dot(qbd_ref[hg],
                 jnp.concatenate([kv_refs[p][0, 256 * hg:256 * (hg + 1), :] for p in range(npg)], axis=1).astype(MXU),
                 preferred_element_type=F32) for hg in range(4)], axis=0) - ck
    update(s, None, lambda pb, hg: _dot_nt(
        pb, jnp.concatenate([kv_refs[p][1, 256 * hg:256 * (hg + 1), :] for p in range(npg)], axis=1).astype(MXU)))

    @pl.when(s_id == pl.num_programs(1) - 1)
    def _():
        logf = _log_sigmoid(new_ref[:, C_F_OFF:C_F_OFF + 128] + bf_ref[...])
        lfo_ref[...] = logf
        tri8 = jnp.where(lax.broadcasted_iota(jnp.int32, (STEP_T, 1), 0)
                         <= lax.broadcasted_iota(jnp.int32, (1, 128), 1), 1.0, 0.0)
        cum_n = lax.dot_general(logf, tri8, (((0,), (0,)), ((), ())), preferred_element_type=F32,
                                precision=HIGHEST)[0:N_HEADS, :] + carry_ref[...]
        ck_n = _dot_hi(expand, cum_n)[:, 0:STEP_T]
        s_n = jnp.concatenate(
            [_dot_nt(qbd_ref[hg], new_ref[:, 1024 + 256 * hg:1024 + 256 * (hg + 1)].astype(MXU))
             for hg in range(4)], axis=0) - ck_n
        valid = lax.broadcasted_iota(jnp.int32, (1, STEP_T), 1) <= rowi
        update(s_n, valid, lambda pb, hg: jnp.dot(
            pb, new_ref[:, 2048 + 256 * hg:2048 + 256 * (hg + 1)].astype(MXU), preferred_element_type=F32))
        o = acc_ref[...] / l_ref[...]
        for hg in range(4):
            om = jnp.where(bd, o[rows * hg:rows * (hg + 1)], 0.0)
            o_ref[:, 256 * hg:256 * (hg + 1)] = jnp.sum(om.reshape(4, STEP_T, 256), axis=0)


def mixer_c_step(proj, b_forget, kv_pool, lf_pool, page_table):
    b, t, _ = proj.shape
    n_pages = page_table.shape[1]
    npg = C_STEP_PAGES
    assert t == STEP_T and n_pages % npg == 0
    kvp = _token_minor(kv_pool, 1).reshape(kv_pool.shape[0], 2, D_MODEL, PAGE_SIZE)
    lfp = _token_minor(lf_pool, 1)
    bf = jnp.pad(b_forget.astype(F32), (0, 128 - N_HEADS)).reshape(1, 128)
    page = lambda shape, p: pl.BlockSpec(shape, lambda i, s, pt, p=p: (pt[i, s * npg + p],) + (0,) * (len(shape) - 1))
    o, logf = pl.pallas_call(
        _fox_step_kernel,
        out_shape=[jax.ShapeDtypeStruct((b, t, D_MODEL), F32), jax.ShapeDtypeStruct((b, t, 128), F32)],
        grid_spec=pltpu.PrefetchScalarGridSpec(
            num_scalar_prefetch=1,
            grid=(b, n_pages // npg),
            in_specs=[pl.BlockSpec((None, t, C_E), lambda i, s, pt: (i, 0, 0)),
                      pl.BlockSpec((1, 128), lambda i, s, pt: (0, 0))]
            + [page((None, 2, D_MODEL, PAGE_SIZE), p) for p in range(npg)]
            + [page((None, N_HEADS, PAGE_SIZE), p) for p in range(npg)],
            out_specs=[pl.BlockSpec((None, t, D_MODEL), lambda i, s, pt: (i, 0, 0)),
                       pl.BlockSpec((None, t, 128), lambda i, s, pt: (i, 0, 0))],
            scratch_shapes=[pltpu.VMEM((4, 4 * STEP_T, 256), MXU),
                            pltpu.VMEM((STEP_LANES, 1), F32), pltpu.VMEM((STEP_LANES, 1), F32),
                            pltpu.VMEM((STEP_LANES, 256), F32), pltpu.VMEM((N_HEADS, 1), F32)]),
        compiler_params=pltpu.CompilerParams(
            dimension_semantics=("arbitrary", "arbitrary"), vmem_limit_bytes=VMEM_LIMIT),
        name="fox_attention_step",
    )(page_table, proj, bf, *([kvp] * npg), *([lfp] * npg))
    kv_new = proj[..., 1024:3072].reshape(b, t, 2, N_HEADS, HEAD_DIM)
    return o, [kv_new, logf[..., :N_HEADS]]


B_STEP_PAGES = 8


def _nsa_cmp_pages_kernel(pt_ref, *refs):
    npg = B_STEP_PAGES
    pages = refs[:npg]
    pet_ref, w_ref, o_ref = refs[npg:]
    nb = npg * PAGE_SIZE // B_BLOCK
    pool = jnp.where(lax.broadcasted_iota(jnp.int32, (npg * PAGE_SIZE, 1), 0) // B_BLOCK
                     == lax.broadcasted_iota(jnp.int32, (1, 128), 1), 1.0 / B_BLOCK, 0.0).astype(MXU)
    outs = []
    for c in range(2):
        x = jnp.concatenate([pages[p][c] + pet_ref[c] for p in range(npg)], axis=1)
        hi = x.astype(MXU)
        lo = (x - hi.astype(F32)).astype(MXU)
        pooled = (jnp.dot(hi, pool, preferred_element_type=F32) + jnp.dot(lo, pool, preferred_element_type=F32))
        cmp_c = lax.dot_general(pooled, w_ref[c], (((0,), (0,)), ((), ())), preferred_element_type=F32,
                                precision=HIGHEST)
        outs.append(cmp_c[0:nb, :])
    o_ref[...] = jnp.concatenate(outs, axis=1)


def nsa_compress_pages(pool_t, page_table, pos_emb, w_cmp):
    b, n_pages = page_table.shape
    npg = B_STEP_PAGES
    nb = npg * PAGE_SIZE // B_BLOCK
    pet = pos_emb.transpose(0, 2, 3, 1).reshape(2, 256, B_BLOCK)
    pet = jnp.concatenate([pet] * (PAGE_SIZE // B_BLOCK), axis=2)
    _, wbd = _nsa_cmp_weights(pos_emb, w_cmp)
    wc = jnp.stack([wbd[0:256, 0:256], wbd[256:512, 256:512]])
    page = lambda p: pl.BlockSpec((None, 2, 256, PAGE_SIZE), lambda i, s, pt, p=p: (pt[i, s * npg + p], 0, 0, 0))
    return pl.pallas_call(
        _nsa_cmp_pages_kernel,
        out_shape=jax.ShapeDtypeStruct((b, n_pages * PAGE_SIZE // B_BLOCK, 512), F32),
        grid_spec=pltpu.PrefetchScalarGridSpec(
            num_scalar_prefetch=1,
            grid=(b, n_pages // npg),
            in_specs=[page(p) for p in range(npg)]
            + [pl.BlockSpec((2, 256, PAGE_SIZE), lambda i, s, pt: (0, 0, 0)),
               pl.BlockSpec((2, 256, 256), lambda i, s, pt: (0, 0, 0))],
            out_specs=pl.BlockSpec((None, nb, 512), lambda i, s, pt: (i, s, 0))),
        compiler_params=pltpu.CompilerParams(
            dimension_semantics=("arbitrary", "arbitrary"), vmem_limit_bytes=VMEM_LIMIT),
        name="nsa_compress_pages",
    )(page_table, *([pool_t] * npg), pet, wc)


def _nsa_step_kernel(pt_ref, slope_ref, new_ref, cmp_ref, win_ref, *refs, past):
    npg = B_STEP_PAGES
    pages = refs[:npg]
    o_ref, qbd_ref, sel_ref, oc_ref, m_ref, l_ref, acc_ref = refs[npg:]
    s_id = pl.program_id(1)
    rows = STEP_LANES
    ridx = lax.broadcasted_iota(jnp.int32, (rows, 1), 0)
    rowi = ridx & (STEP_T - 1)
    rowg = (ridx >> 3) & 3
    colg = lax.broadcasted_iota(jnp.int32, (1, 256), 1) >> 6
    bd = rowg == colg
    slope = slope_ref[...]
    pos = past + rowi
    ncb = past // B_BLOCK
    nsel = sel_ref.shape[1]
    kvo = B_KV_OFF

    def update(s, valid, pv):
        s = jnp.where(valid, s, NEG)
        m_old = m_ref[...]
        m_new = jnp.maximum(m_old, jnp.max(s, axis=1, keepdims=True))
        p = jnp.exp(s - m_new)
        alpha = jnp.exp(m_old - m_new)
        l_ref[...] = alpha * l_ref[...] + jnp.sum(p, axis=1, keepdims=True)
        acc_ref[...] = alpha * acc_ref[...] + pv(p.astype(MXU))
        m_ref[...] = m_new

    @pl.when(s_id == 0)
    def _():
        qbd = jnp.concatenate(
            [jnp.where(bd[0:32], _tile_rows(new_ref[:, 256 * r:256 * (r + 1)], 4), 0.0) for r in range(4)], axis=0)
        qbd_ref[...] = (qbd * (HEAD_DIM ** -0.5)).astype(MXU)
        blk_end = (lax.broadcasted_iota(jnp.int32, (1, ncb), 1) + 1) * B_BLOCK - 1
        s_c = _dot_nt(qbd_ref[...], cmp_ref[:, 0:256].astype(MXU)) - slope * (pos - blk_end).astype(F32)
        sm = jnp.where(blk_end <= pos, s_c, -jnp.inf)
        mx = jnp.max(sm, axis=1, keepdims=True)
        mx = jnp.where(mx > -jnp.inf, mx, 0.0)
        e = jnp.exp(sm - mx)
        den = jnp.sum(e, axis=1, keepdims=True)
        p_c = e / jnp.where(den > 0, den, 1.0)
        oc_ref[...] = jnp.dot(p_c.astype(MXU), cmp_ref[:, 256:512].astype(MXU), preferred_element_type=F32)
        cidx = lax.broadcasted_iota(jnp.int32, (1, rows), 1)
        same = jnp.where((((cidx >> 3) & 3) == rowg) & ((cidx & (STEP_T - 1)) == rowi), 1.0, 0.0)
        sc = _dot_hi(same, p_c)
        sc = jnp.concatenate([sc, jnp.zeros((rows, nsel - ncb), F32)], axis=1)
        nidx = lax.broadcasted_iota(jnp.int32, (1, nsel), 1)
        cur = pos // B_BLOCK
        forced = (nidx == 0) | (nidx >= cur - 1)
        score = jnp.where(nidx > cur, -jnp.inf, jnp.where(forced, jnp.inf, sc))
        rank = jnp.zeros((rows, nsel), F32)
        for n in range(ncb + 1):
            colv = score[:, n:n + 1]
            beats = (colv > score) | ((colv == score) & (nidx > n))
            rank = rank + jnp.where(beats, 1.0, 0.0)
        sel_ref[...] = jnp.where(rank < float(B_TOPK), 1.0, 0.0).astype(MXU)
        m_ref[...] = jnp.full(m_ref.shape, NEG, F32)
        l_ref[...] = jnp.zeros(l_ref.shape, F32)
        acc_ref[...] = jnp.zeros(acc_ref.shape, F32)

    nk = npg * PAGE_SIZE
    kbase = s_id * nk
    kt = jnp.concatenate([pages[p][0] for p in range(npg)], axis=1).astype(MXU)
    vt = jnp.concatenate([pages[p][1] for p in range(npg)], axis=1).astype(MXU)
    kpos = kbase + lax.broadcasted_iota(jnp.int32, (1, nk), 1)
    s = jnp.dot(qbd_ref[...], kt, preferred_element_type=F32) - slope * (pos - kpos).astype(F32)
    ex = jnp.where(lax.broadcasted_iota(jnp.int32, (nsel, 1), 0) == (kpos // B_BLOCK), 1.0, 0.0).astype(MXU)
    selk = jnp.dot(sel_ref[...], ex, preferred_element_type=F32)
    update(s, selk > 0.5, lambda pb: _dot_nt(pb, vt))

    @pl.when(s_id == pl.num_programs(1) - 1)
    def _():
        distn = rowi - lax.broadcasted_iota(jnp.int32, (1, STEP_T), 1)
        sn = _dot_nt(qbd_ref[...], new_ref[:, kvo + 512:kvo + 768].astype(MXU)) - slope * distn.astype(F32)
        vn = new_ref[:, kvo + 768:kvo + 1024].astype(MXU)
        seln = sel_ref[:, ncb:ncb + 1].astype(F32) > 0.5
        update(sn, (distn >= 0) & seln, lambda pb: jnp.dot(pb, vn, preferred_element_type=F32))
        o_s = acc_ref[...] / l_ref[...]
        lw = win_ref.shape[2]
        dist = pos - ((past - lw) + lax.broadcasted_iota(jnp.int32, (1, lw), 1))
        sw = jnp.dot(qbd_ref[...], win_ref[0].astype(MXU), preferred_element_type=F32) - slope * dist.astype(F32)
        sw = jnp.where((dist >= 0) & (dist <= B_WINDOW), sw, NEG)
        swn = _dot_nt(qbd_ref[...], new_ref[:, kvo + 1024:kvo + 1280].astype(MXU)) - slope * distn.astype(F32)
        swn = jnp.where(distn >= 0, swn, NEG)
        mw = jnp.maximum(_rowmax(sw), jnp.max(swn, axis=1, keepdims=True))
        pw = jnp.exp(sw - mw)
        pwn = jnp.exp(swn - mw)
        o_w = (_dot_nt(pw.astype(MXU), win_ref[1].astype(MXU))
               + jnp.dot(pwn.astype(MXU), new_ref[:, kvo + 1280:kvo + 1536].astype(MXU), preferred_element_type=F32))
        o_w = o_w / (jnp.sum(pw, axis=1, keepdims=True) + jnp.sum(pwn, axis=1, keepdims=True))
        pick = jnp.where(rowi == lax.broadcasted_iota(jnp.int32, (1, STEP_T), 1), 1.0, 0.0)
        gmat = _dot_hi(pick, _sigmoid(new_ref[:, B_BG_OFF:B_BG_OFF + 128]))
        gcol = (ridx >> 5) * 4 + rowg
        lane128 = lax.broadcasted_iota(jnp.int32, (1, 128), 1)
        gate = lambda br: jnp.sum(jnp.where(lane128 == gcol + 16 * br, gmat, 0.0), axis=1, keepdims=True)
        o_all = jnp.where(bd, gate(0) * oc_ref[...] + gate(1) * o_s + gate(2) * o_w, 0.0)
        for r in range(4):
            o_ref[:, 256 * r:256 * (r + 1)] = jnp.sum(o_all[32 * r:32 * (r + 1)].reshape(4, STEP_T, 256), axis=0)


def mixer_b_step(proj, pos_emb, w_cmp, kv_pool, win_cache, page_table):
    b, t, _ = proj.shape
    n_pages = page_table.shape[1]
    npg = B_STEP_PAGES
    past = n_pages * PAGE_SIZE
    assert t == STEP_T and n_pages % npg == 0
    ncb = past // B_BLOCK
    nsel = -(-(ncb + 1) // 128) * 128
    pool_t = _token_minor(kv_pool, 1).reshape(kv_pool.shape[0], 4, 256, PAGE_SIZE)
    cmp = nsa_compress_pages(pool_t, page_table, pos_emb, w_cmp)
    lw = win_cache.shape[1]
    win_t = _token_minor(win_cache, 1).reshape(b, 2, 256, lw)
    heads = np.array([4 * g + r for r in range(4) for g in range(4)])
    slopes = np.float32(2.0 ** (-8.0 * (heads + 1) / N_HEADS))
    slope_rows = jnp.asarray(np.repeat(slopes, STEP_T).reshape(STEP_LANES, 1), F32)
    page = lambda p: pl.BlockSpec((None, 2, 256, PAGE_SIZE), lambda i, s, pt, p=p: (pt[i, s * npg + p], 1, 0, 0))
    o = pl.pallas_call(
        functools.partial(_nsa_step_kernel, past=past),
        out_shape=jax.ShapeDtypeStruct((b, t, D_MODEL), F32),
        grid_spec=pltpu.PrefetchScalarGridSpec(
            num_scalar_prefetch=1,
            grid=(b, n_pages // npg),
            in_specs=[pl.BlockSpec((STEP_LANES, 1), lambda i, s, pt: (0, 0)),
                      pl.BlockSpec((None, t, B_E), lambda i, s, pt: (i, 0, 0)),
                      pl.BlockSpec((None, ncb, 512), lambda i, s, pt: (i, 0, 0)),
                      pl.BlockSpec((None, 2, 256, lw), lambda i, s, pt: (i, 0, 0, 0))]
            + [page(p) for p in range(npg)],
            out_specs=pl.BlockSpec((None, t, D_MODEL), lambda i, s, pt: (i, 0, 0)),
            scratch_shapes=[pltpu.VMEM((STEP_LANES, 256), MXU),
                            pltpu.VMEM((STEP_LANES, nsel), MXU),
                            pltpu.VMEM((STEP_LANES, 256), F32),
                            pltpu.VMEM((STEP_LANES, 1), F32), pltpu.VMEM((STEP_LANES, 1), F32),
                            pltpu.VMEM((STEP_LANES, 256), F32)]),
        compiler_params=pltpu.CompilerParams(
            dimension_semantics=("arbitrary", "arbitrary"), vmem_limit_bytes=VMEM_LIMIT),
        name="nsa_attention_step",
    )(page_table, slope_rows, proj, cmp, win_t, *([pool_t] * npg))
    kv_full_new = proj[..., B_KV_OFF:B_KV_OFF + 1024].reshape(b, t, 4, B_KV_HEADS, HEAD_DIM)
    return o, [kv_full_new, proj[..., B_KV_OFF + 1024:B_KV_OFF + 1536]]


def roll_b_window(win_caches, new_rows):
    nl, b, w = win_caches.shape[:3]
    ct = _token_minor(win_caches, 2).reshape(nl, b, 2 * B_KV_WIDTH, w)
    out = cache_roll(ct, jnp.stack(new_rows))
    return _token_major(out.reshape(nl, b, 2, B_KV_HEADS, HEAD_DIM, w), 2)


def _pad_cols(w, e_pad):
    return jnp.pad(w, ((0, 0), (0, e_pad - w.shape[1])))


def run_trunk(x, mods, layer_cache, weights, *, bt, tt):
    (norm_pre, norm_post, a_w_in, a_w_out, b_w_in, b_pos_emb, b_w_cmp, b_w_out,
     c_w_in, c_b_forget, c_w_out) = weights
    states = []
    for i in range(DEPTH):
        kind, j = i % N_MIXERS, i // N_MIXERS
        mod = mods[i][:, None, :]
        cache = layer_cache(i)
        if kind == 0:
            w_in = a_w_in[j].astype(BF16)
            proj = in_projection(x, mod, norm_pre[i], w_in, bt=bt, tt=tt, tn=1024)
            o, st = mixer_a_prompt(proj) if cache is None else mixer_a_step(proj, cache, j)
            gate_col, w_out = A_E // 1024 - 1, a_w_out[j]
        elif kind == 1:
            w_in = _pad_cols(b_w_in[j][:, _b_in_perm()], B_E).astype(BF16)
            proj = in_projection(x, mod, norm_pre[i], w_in, bt=bt, tt=tt, tn=1024)
            if cache is None:
                o, st = mixer_b_prompt(proj, b_pos_emb[j], b_w_cmp[j])
            else:
                o, st = mixer_b_step(proj, b_pos_emb[j], b_w_cmp[j], *cache)
            gate_col, w_out = 1, b_w_out[j][_b_out_perm(), :]
        else:
            w = c_w_in[j]
            w_in = jnp.concatenate([w[:, :3072], w[:, 3088:4112], _pad_cols(w[:, 3072:3088], 128)],
                                   axis=1).astype(BF16)
            proj = in_projection(x, mod, norm_pre[i], w_in, bt=bt, tt=tt, tn=C_E // 3)
            if cache is None:
                o, st = mixer_c_prompt(proj, c_b_forget[j])
            else:
                o, st = mixer_c_step(proj, c_b_forget[j], *cache)
            gate_col, w_out = 3, c_w_out[j]
        x = out_projection(o, proj, gate_col, x, mod, norm_post[i], w_out.astype(BF16), bt=bt, tt=tt)
        states.append(st)
    return x, states


def stack_state(states, kind, k):
    return jnp.stack([states[i][k] for i in range(DEPTH) if i % N_MIXERS == kind])


def kernel(x_prompt, x_sample, c_prompt, c_sample, cache_a_w128, cache_a_w512, cache_a_w2048,
           cache_b_kv, cache_b_win, cache_c_kv, cache_c_logf, page_table,
           ada_w, ada_b, norm_pre, norm_post, a_w_in, a_w_out, b_w_in, b_pos_emb, b_w_cmp, b_w_out,
           c_w_in, c_b_forget, c_w_out):
    weights = (norm_pre, norm_post, a_w_in, a_w_out, b_w_in, b_pos_emb, b_w_cmp, b_w_out,
               c_w_in, c_b_forget, c_w_out)
    nbp, nbs = x_prompt.shape[0], x_sample.shape[0]
    nb_pad = -(-(nbp + nbs) // 8) * 8
    c_all = jnp.concatenate([c_prompt, c_sample,
                             jnp.zeros((nb_pad - nbp - nbs, D_MODEL), F32)], axis=0)
    mods = ada_modulation(c_all, ada_w, ada_b)
    mods_p = mods[:, :nbp]
    mods_s = mods[:, nbp:nbp + nbs]

    def prompt_cache(i):
        return None

    def sample_cache(i):
        kind, j = i % N_MIXERS, i // N_MIXERS
        if kind == 0:
            return [cache_a_w128, cache_a_w512, cache_a_w2048]
        if kind == 1:
            return [cache_b_kv[j], cache_b_win[j], page_table]
        return [cache_c_kv[j], cache_c_logf[j], page_table]

    y_prompt, st_p = run_trunk(x_prompt, mods_p, prompt_cache, weights, bt=1, tt=1024)
    y_sample, st_s = run_trunk(x_sample, mods_s, sample_cache, weights, bt=nbs, tt=x_sample.shape[1])

    outs = [y_prompt, y_sample]
    outs += [stack_state(st_p, 0, 0), stack_state(st_p, 0, 1), stack_state(st_p, 0, 2),
             stack_state(st_p, 1, 0), stack_state(st_p, 1, 1),
             stack_state(st_p, 2, 0), stack_state(st_p, 2, 1)]
    layers = lambda kind: [i for i in range(DEPTH) if i % N_MIXERS == kind]
    outs += [roll_a_caches(c, [st_s[i][g] for i in layers(0)])
             for g, c in enumerate((cache_a_w128, cache_a_w512, cache_a_w2048))]
    outs += [stack_state(st_s, 1, 0), roll_b_window(cache_b_win, [st_s[i][1] for i in layers(1)]),
             stack_state(st_s, 2, 0), stack_state(st_s, 2, 1)]
    return tuple(outs)
```
